```python
import math
import jax, jax.numpy as jnp
from jax import lax
import numpy as np

D_MODEL = 2048
BATCH = 4
SEQ = 2048
DEPTH = 2
DEC_BATCH = 128
DEC_SEQ = 8
PAST_LEN = 16384
PAGE_SIZE = 128

S5_WIDTH = D_MODEL // 2
S5_GROUP = 16
S5_GROUPS = S5_WIDTH // S5_GROUP
S5_STATE = 64
LRU_WIDTH = D_MODEL // 2
LRU_BLOCKS = 8
LRU_BLOCK = LRU_WIDTH // LRU_BLOCKS
CONV_WIDTH = 4
LRU_C = 8.0
RET_HEADS = 8
RET_DK = 128
RET_DV = 128
RET_WIDTH = RET_HEADS * RET_DV
RET_CHUNK = 128
ROPE_BASE = 10000.0
NORM_EPS = 1e-6
GN_EPS = 1e-5
IN_SIZES = (S5_WIDTH, S5_WIDTH, LRU_WIDTH, LRU_WIDTH, RET_HEADS * RET_DK, RET_HEADS * RET_DK,
            RET_WIDTH, RET_WIDTH, D_MODEL, D_MODEL, D_MODEL)
N_IN = sum(IN_SIZES)

kernel_name = 'hybrid_s5_rglru_retention_step'


def _rmsnorm(x, g):
    xf = x.astype(jnp.float32)
    y = xf * lax.rsqrt(jnp.mean(xf * xf, axis=-1, keepdims=True) + NORM_EPS)
    return (y * g.astype(jnp.float32)).astype(x.dtype)


def _combine(e1, e2):
    a1, b1 = e1
    a2, b2 = e2
    return a1 * a2, a2 * b1 + b2


def _s5(u, h0_re, h0_im, lam_re, lam_im, log_dt, b_re, b_im, c_re, c_im, d, w_glu, b_glu):
    bsz, s = u.shape[0], u.shape[1]
    ug = u.reshape(bsz, s, S5_GROUPS, S5_GROUP)
    lam = lax.complex(lam_re.astype(jnp.float32), lam_im.astype(jnp.float32))
    dt = jnp.exp(log_dt.astype(jnp.float32))[:, None]
    lam_bar = jnp.exp(lam * dt)
    b = lax.complex(b_re.astype(jnp.float32), b_im.astype(jnp.float32))
    b_bar = ((lam_bar - 1.0) / lam)[..., None] * b
    bu = jnp.einsum('bsgc,gpc->bsgp', ug.astype(jnp.complex64), b_bar)
    h0 = lax.complex(h0_re.astype(jnp.float32), h0_im.astype(jnp.float32))
    bu = bu.at[:, 0].add(lam_bar * h0)
    a = jnp.broadcast_to(lam_bar, bu.shape)
    _, h = lax.associative_scan(_combine, (a, bu), axis=1)
    y = (jnp.einsum('bsgp,gcp->bsgc', h.real, c_re.astype(jnp.float32))
         - jnp.einsum('bsgp,gcp->bsgc', h.imag, c_im.astype(jnp.float32))
         + d.astype(jnp.float32).reshape(S5_GROUPS, S5_GROUP) * ug)
    y = jax.nn.gelu(y.reshape(bsz, s, S5_WIDTH))
    y = y * jax.nn.sigmoid(y @ w_glu.astype(jnp.float32) + b_glu.astype(jnp.float32))
    h_last = h[:, -1]
    return y, h_last.real, h_last.imag


def _rglru(u, h0, conv_buf, conv_w, conv_b, w_a, b_a, w_x, b_x, lam):
    bsz, s = u.shape[0], u.shape[1]
    xp = jnp.concatenate([conv_buf.astype(jnp.float32), u], axis=1)
    cw = conv_w.astype(jnp.float32)
    xc = sum(xp[:, k:k + s] * cw[k] for k in range(CONV_WIDTH)) + conv_b.astype(jnp.float32)
    new_buf = xp[:, -(CONV_WIDTH - 1):]
    xb = xc.reshape(bsz, s, LRU_BLOCKS, LRU_BLOCK)
    r = jax.nn.sigmoid(jnp.einsum('bsnc,ncd->bsnd', xb, w_a.astype(jnp.float32)).reshape(bsz, s, LRU_WIDTH)
                       + b_a.astype(jnp.float32))
    i = jax.nn.sigmoid(jnp.einsum('bsnc,ncd->bsnd', xb, w_x.astype(jnp.float32)).reshape(bsz, s, LRU_WIDTH)
                       + b_x.astype(jnp.float32))
    log_a = -LRU_C * r * jax.nn.softplus(-lam.astype(jnp.float32))
    a = jnp.exp(log_a)
    bx = jnp.sqrt(-jnp.expm1(2.0 * log_a)) * (i * xc)
    bx = bx.at[:, 0].add(a[:, 0] * h0.astype(jnp.float32))
    _, h = lax.associative_scan(_combine, (a, bx), axis=1)
    return h, h[:, -1], new_buf


def _rope(t, pos):
    half = t.shape[-1] // 2
    freq = ROPE_BASE ** (-jnp.arange(half, dtype=jnp.float32) / half)
    ang = pos[:, None] * freq[None, :]
    cos = jnp.cos(ang)[None, :, None, :]
    sin = jnp.sin(ang)[None, :, None, :]
    t1, t2 = t[..., :half], t[..., half:]
    return jnp.concatenate([t1 * cos - t2 * sin, t1 * sin + t2 * cos], axis=-1)


def _retention(q, k, v, r0, pos0, gn_g):
    bsz, s = q.shape[0], q.shape[1]
    pos = jnp.arange(s, dtype=jnp.float32) + float(pos0)
    q = _rope(q.reshape(bsz, s, RET_HEADS, RET_DK), pos)
    k = _rope(k.reshape(bsz, s, RET_HEADS, RET_DK), pos) * (RET_DK ** -0.5)
    v = v.reshape(bsz, s, RET_HEADS, RET_DV)
    c = math.gcd(s, RET_CHUNK)
    nc = s // c
    log_g = jnp.log1p(-jnp.exp2(-5.0 - jnp.arange(RET_HEADS, dtype=jnp.float32)))
    idx = jnp.arange(c, dtype=jnp.float32)
    diff = idx[:, None] - idx[None, :]
    dmask = jnp.where(diff[None] >= 0, jnp.exp(jnp.maximum(diff, 0.0)[None] * log_g[:, None, None]), 0.0)
    xi = jnp.exp((idx[:, None] + 1.0) * log_g[None, :])
    zeta = jnp.exp((c - 1.0 - idx[:, None]) * log_g[None, :])
    g_chunk = jnp.exp(c * log_g)[None, :, None, None]

    def to_chunks(t):
        return jnp.moveaxis(t.reshape(bsz, nc, c, RET_HEADS, t.shape[-1]), 1, 0)

    def step(r, inp):
        qc, kc, vc = inp
        sc = jnp.einsum('bihd,bjhd->bhij', qc, kc) * dmask[None]
        inner = jnp.einsum('bhij,bjhe->bihe', sc, vc)
        cross = jnp.einsum('bihd,bhde->bihe', qc, r) * xi[None, :, :, None]
        r_new = r * g_chunk + jnp.einsum('bjhd,bjhe->bhde', kc * zeta[None, :, :, None], vc)
        return r_new, inner + cross

    r_last, o = lax.scan(step, r0.astype(jnp.float32), (to_chunks(q), to_chunks(k), to_chunks(v)))
    o = jnp.moveaxis(o, 0, 1).reshape(bsz, s, RET_HEADS, RET_DV)
    mu = jnp.mean(o, axis=-1, keepdims=True)
    var = jnp.mean(jnp.square(o - mu), axis=-1, keepdims=True)
    o = ((o - mu) * lax.rsqrt(var + GN_EPS)).reshape(bsz, s, RET_WIDTH) * gn_g.astype(jnp.float32)
    return o, r_last


def _layer(x, pos0, h_s5_re, h_s5_im, h_lru, conv_buf, r_ret, lp):
    xn = _rmsnorm(x, lp['norm_g'])
    proj = (xn @ lp['w_in']).astype(jnp.float32)
    splits = np.cumsum(IN_SIZES)[:-1].tolist()
    u_s5, z_s5, u_lru, z_lru, q, k, v, z_ret, g_s5, g_lru, g_ret = jnp.split(proj, splits, axis=-1)
    y_s5, ns_re, ns_im = _s5(u_s5, h_s5_re, h_s5_im, lp['s5_lambda_re'], lp['s5_lambda_im'], lp['s5_log_dt'],
                             lp['s5_b_re'], lp['s5_b_im'], lp['s5_c_re'], lp['s5_c_im'], lp['s5_d'],
                             lp['s5_w_glu'], lp['s5_b_glu'])
    y_lru, n_lru, n_conv = _rglru(u_lru, h_lru, conv_buf, lp['lru_conv_w'], lp['lru_conv_b'], lp['lru_w_a'],
                                  lp['lru_b_a'], lp['lru_w_x'], lp['lru_b_x'], lp['lru_lambda'])
    y_ret, n_ret = _retention(q, k, v, r_ret, pos0, lp['ret_gn_g'])
    b_s5 = (y_s5 * jax.nn.silu(z_s5)) @ lp['w_branch_s5'].astype(jnp.float32)
    b_lru = (y_lru * jax.nn.silu(z_lru)) @ lp['w_branch_lru'].astype(jnp.float32)
    b_ret = (y_ret * jax.nn.silu(z_ret)) @ lp['w_branch_ret'].astype(jnp.float32)
    merged = jax.nn.sigmoid(g_s5) * b_s5 + jax.nn.sigmoid(g_lru) * b_lru + jax.nn.sigmoid(g_ret) * b_ret
    out = merged @ lp['w_out'].astype(jnp.float32)
    return (x + out.astype(x.dtype)), ns_re, ns_im, n_lru, n_conv, n_ret


def setup_inputs(seed: int = 0) -> dict:
    key = jax.random.key(seed)
    it = iter(jax.random.split(key, 48))
    nrm = lambda shape, scale: jax.random.normal(next(it), shape, jnp.float32) * scale
    L = DEPTH
    inp = {}
    inp['x_prompt'] = nrm((BATCH, SEQ, D_MODEL), 1.0)
    inp['x_sample'] = nrm((DEC_BATCH, DEC_SEQ, D_MODEL), 1.0)
    inp['state_s5_re'] = nrm((L, DEC_BATCH, S5_GROUPS, S5_STATE), 0.3)
    inp['state_s5_im'] = nrm((L, DEC_BATCH, S5_GROUPS, S5_STATE), 0.3)
    inp['state_lru'] = nrm((L, DEC_BATCH, LRU_WIDTH), 0.5)
    inp['state_conv'] = nrm((L, DEC_BATCH, CONV_WIDTH - 1, LRU_WIDTH), 1.0)
    inp['state_ret'] = nrm((L, DEC_BATCH, RET_HEADS, RET_DK, RET_DV), 0.3)
    inp['norm_g'] = 1.0 + nrm((L, D_MODEL), 0.02)
    inp['w_in'] = nrm((L, D_MODEL, N_IN), D_MODEL ** -0.5)
    inp['s5_lambda_re'] = -0.5 + nrm((L, S5_GROUPS, S5_STATE), 0.01)
    inp['s5_lambda_im'] = (jnp.pi * jnp.arange(S5_STATE, dtype=jnp.float32))[None, None, :] + nrm((L, S5_GROUPS, S5_STATE), 0.01)
    inp['s5_log_dt'] = jax.random.uniform(next(it), (L, S5_GROUPS), jnp.float32, math.log(1e-3), math.log(1e-1))
    inp['s5_b_re'] = nrm((L, S5_GROUPS, S5_STATE, S5_GROUP), (2 * S5_GROUP) ** -0.5)
    inp['s5_b_im'] = nrm((L, S5_GROUPS, S5_STATE, S5_GROUP), (2 * S5_GROUP) ** -0.5)
    inp['s5_c_re'] = nrm((L, S5_GROUPS, S5_GROUP, S5_STATE), S5_STATE ** -0.5)
    inp['s5_c_im'] = nrm((L, S5_GROUPS, S5_GROUP, S5_STATE), S5_STATE ** -0.5)
    inp['s5_d'] = nrm((L, S5_WIDTH), 1.0)
    inp['s5_w_glu'] = nrm((L, S5_WIDTH, S5_WIDTH), S5_WIDTH ** -0.5)
    inp['s5_b_glu'] = nrm((L, S5_WIDTH), 0.01)
    inp['lru_conv_w'] = nrm((L, CONV_WIDTH, LRU_WIDTH), CONV_WIDTH ** -0.5)
    inp['lru_conv_b'] = nrm((L, LRU_WIDTH), 0.01)
    inp['lru_w_a'] = nrm((L, LRU_BLOCKS, LRU_BLOCK, LRU_BLOCK), LRU_BLOCK ** -0.5)
    inp['lru_b_a'] = nrm((L, LRU_WIDTH), 0.01)
    inp['lru_w_x'] = nrm((L, LRU_BLOCKS, LRU_BLOCK, LRU_BLOCK), LRU_BLOCK ** -0.5)
    inp['lru_b_x'] = nrm((L, LRU_WIDTH), 0.01)
    a8 = jax.random.uniform(next(it), (L, LRU_WIDTH), jnp.float32, 0.9, 0.999)
    a_base = a8 ** (1.0 / LRU_C)
    inp['lru_lambda'] = jnp.log(a_base) - jnp.log1p(-a_base)
    inp['ret_gn_g'] = 1.0 + nrm((L, RET_WIDTH), 0.02)
    inp['w_branch_s5'] = nrm((L, S5_WIDTH, D_MODEL), S5_WIDTH ** -0.5)
    inp['w_branch_lru'] = nrm((L, LRU_WIDTH, D_MODEL), LRU_WIDTH ** -0.5)
    inp['w_branch_ret'] = nrm((L, RET_WIDTH, D_MODEL), RET_WIDTH ** -0.5)
    inp['w_out'] = nrm((L, D_MODEL, D_MODEL), D_MODEL ** -0.5)
    inp['final_norm_g'] = 1.0 + nrm((D_MODEL,), 0.02)
    return inp


def reference(x_prompt, x_sample, state_s5_re, state_s5_im, state_lru, state_conv, state_ret,
              norm_g, w_in, s5_lambda_re, s5_lambda_im, s5_log_dt, s5_b_re, s5_b_im, s5_c_re, s5_c_im,
              s5_d, s5_w_glu, s5_b_glu, lru_conv_w, lru_conv_b, lru_w_a, lru_b_a, lru_w_x, lru_b_x,
              lru_lambda, ret_gn_g, w_branch_s5, w_branch_lru, w_branch_ret, w_out, final_norm_g):
    f32 = jnp.float32
    bp = x_prompt.shape[0]
    xp, xs = x_prompt, x_sample
    new_p = [[] for _ in range(5)]
    new_s = [[] for _ in range(5)]
    for l in range(DEPTH):
        lp = dict(norm_g=norm_g[l], w_in=w_in[l], s5_lambda_re=s5_lambda_re[l], s5_lambda_im=s5_lambda_im[l],
                  s5_log_dt=s5_log_dt[l], s5_b_re=s5_b_re[l], s5_b_im=s5_b_im[l], s5_c_re=s5_c_re[l],
                  s5_c_im=s5_c_im[l], s5_d=s5_d[l], s5_w_glu=s5_w_glu[l], s5_b_glu=s5_b_glu[l],
                  lru_conv_w=lru_conv_w[l], lru_conv_b=lru_conv_b[l], lru_w_a=lru_w_a[l], lru_b_a=lru_b_a[l],
                  lru_w_x=lru_w_x[l], lru_b_x=lru_b_x[l], lru_lambda=lru_lambda[l], ret_gn_g=ret_gn_g[l],
                  w_branch_s5=w_branch_s5[l], w_branch_lru=w_branch_lru[l], w_branch_ret=w_branch_ret[l],
                  w_out=w_out[l])
        xp, *sp = _layer(xp, 0,
                         jnp.zeros((bp, S5_GROUPS, S5_STATE), f32), jnp.zeros((bp, S5_GROUPS, S5_STATE), f32),
                         jnp.zeros((bp, LRU_WIDTH), f32), jnp.zeros((bp, CONV_WIDTH - 1, LRU_WIDTH), f32),
                         jnp.zeros((bp, RET_HEADS, RET_DK, RET_DV), f32), lp)
        xs, *ss = _layer(xs, PAST_LEN, state_s5_re[l], state_s5_im[l], state_lru[l], state_conv[l],
                         state_ret[l], lp)
        for j in range(5):
            new_p[j].append(sp[j])
            new_s[j].append(ss[j])
    y_prompt = _rmsnorm(xp, final_norm_g)
    y_sample = _rmsnorm(xs, final_norm_g)
    s5_re_p, s5_im_p, lru_p, conv_p, ret_p = [jnp.stack(t, axis=0) for t in new_p]
    s5_re_s, s5_im_s, lru_s, conv_s, ret_s = [jnp.stack(t, axis=0) for t in new_s]
    return (y_prompt, y_sample, s5_re_p, s5_im_p, lru_p, conv_p, ret_p, s5_re_s, s5_im_s, lru_s, conv_s, ret_s)
```

```python
import functools
import math

import jax
import jax.numpy as jnp
from jax import lax
from jax.experimental import pallas as pl
from jax.experimental.pallas import tpu as pltpu

F32 = jnp.float32
BF16 = jnp.bfloat16

D_MODEL = 2048
BATCH = 4
SEQ = 2048
DEPTH = 2
DEC_BATCH = 128
DEC_SEQ = 8
PAST_LEN = 16384
S5_WIDTH = 1024
S5_GROUP = 16
S5_GROUPS = 64
S5_STATE = 64
S5_LANES = S5_GROUPS * S5_STATE
LRU_WIDTH = 1024
LRU_BLOCKS = 8
LRU_BLOCK = 128
CONV_WIDTH = 4
LRU_C = 8.0
RET_HEADS = 8
RET_DK = 128
RET_DV = 128
RET_WIDTH = 1024
RET_CHUNK = 128
ROPE_BASE = 10000.0
NORM_EPS = 1e-6
GN_EPS = 1e-5
N_IN = 14336

N_PROMPT = BATCH * SEQ
N_SAMPLE = DEC_BATCH * DEC_SEQ
N_TOK = N_PROMPT + N_SAMPLE

SUBLANES = 8
LANES = 128
VMEM_LIMIT = 56 * 1024 * 1024

COL_U_S5, COL_Z_S5, COL_U_LRU, COL_Z_LRU, COL_Q, COL_K, COL_V, COL_Z_RET = range(8)
COL_GATES = 4

S5_KB = 4
S5_KW = S5_WIDTH // S5_KB
S5_NW = S5_LANES // S5_KB


def _params(sem, vmem=VMEM_LIMIT):
    return pltpu.CompilerParams(dimension_semantics=sem, vmem_limit_bytes=vmem)


def _sigmoid(x):
    return jax.nn.sigmoid(x)


def _silu(x):
    return x * jax.nn.sigmoid(x)


def _last_rows(x, nseq):
    return x.reshape(nseq, DEC_SEQ, x.shape[-1])[:, DEC_SEQ - 1, :]


def _norm_kernel(x_ref, g_ref, o_ref):
    x = x_ref[...]
    ms = jnp.mean(x * x, axis=-1, keepdims=True)
    o_ref[...] = (x * lax.rsqrt(ms + NORM_EPS) * g_ref[...]).astype(o_ref.dtype)


def _rmsnorm_call(x, g, out_dtype, tm=512):
    n = x.shape[0]
    return pl.pallas_call(
        _norm_kernel,
        grid=(n // tm,),
        in_specs=[pl.BlockSpec((tm, D_MODEL), lambda i: (i, 0)),
                  pl.BlockSpec((1, D_MODEL), lambda i: (0, 0))],
        out_specs=pl.BlockSpec((tm, D_MODEL), lambda i: (i, 0)),
        out_shape=jax.ShapeDtypeStruct((n, D_MODEL), out_dtype),
        compiler_params=_params(("parallel",)),
        name="rmsnorm",
    )(x, g.reshape(1, D_MODEL))


def _inproj_kernel(xn_ref, w_ref, o_ref, wbf_ref):
    @pl.when(pl.program_id(1) == 0)
    def _():
        wbf_ref[...] = w_ref[...].astype(BF16)

    o_ref[...] = jnp.dot(xn_ref[...], wbf_ref[...], preferred_element_type=F32).astype(o_ref.dtype)


def _inproj_call(xn, w_in, layer, tm=1024, tn=1024):
    n = xn.shape[0]
    return pl.pallas_call(
        _inproj_kernel,
        grid=(N_IN // tn, n // tm),
        in_specs=[pl.BlockSpec((tm, D_MODEL), lambda j, i: (i, 0)),
                  pl.BlockSpec((None, D_MODEL, tn), lambda j, i: (layer, 0, j))],
        out_specs=pl.BlockSpec((tm, tn), lambda j, i: (i, j)),
        out_shape=jax.ShapeDtypeStruct((n, N_IN), BF16),
        scratch_shapes=[pltpu.VMEM((D_MODEL, tn), BF16)],
        compiler_params=_params(("parallel", "arbitrary")),
        name="inproj",
    )(xn, w_in)


def _s5_prep(lam_re, lam_im, log_dt, b_re, b_im, c_re, c_im):
    dt = jnp.exp(log_dt)[:, None]
    e = jnp.exp(lam_re * dt)
    lbr = e * jnp.cos(lam_im * dt)
    lbi = e * jnp.sin(lam_im * dt)
    nr, ni = lbr - 1.0, lbi
    den = lam_re * lam_re + lam_im * lam_im
    cr = (nr * lam_re + ni * lam_im) / den
    ci = (ni * lam_re - nr * lam_im) / den
    bbr = cr[..., None] * b_re - ci[..., None] * b_im
    bbi = cr[..., None] * b_im + ci[..., None] * b_re
    eye = jnp.eye(S5_GROUPS // S5_KB, dtype=F32)

    def bblk(bb):
        t = bb.reshape(S5_KB, 16, S5_STATE, S5_GROUP)
        t = jnp.einsum('kgpc,gh->kgchp', t, eye)
        return t.reshape(S5_KB, S5_KW, S5_NW)

    def cblk(cc):
        t = cc.reshape(S5_KB, 16, S5_GROUP, S5_STATE)
        t = jnp.einsum('kgcp,gh->kgphc', t, eye)
        return t.reshape(S5_KB, S5_NW, S5_KW)

    b_blk = jnp.concatenate([bblk(bbr), bblk(bbi)], axis=2).astype(BF16)
    c_blk = jnp.concatenate([cblk(c_re), -cblk(c_im)], axis=1).astype(BF16)

    pr, pi = [lbr.reshape(-1)], [lbi.reshape(-1)]
    for _ in range(7):
        r_, i_ = pr[-1], pi[-1]
        pr.append(r_ * pr[0] - i_ * pi[0])
        pi.append(r_ * pi[0] + i_ * pr[0])
    row = jnp.arange(SUBLANES)[:, None]
    ak = jnp.stack([jnp.stack([jnp.where(row >= k, pr[k - 1][None, :], 0.0),
                               jnp.where(row >= k, pi[k - 1][None, :], 0.0)]) for k in (1, 2, 4)])
    pw = jnp.stack([jnp.stack(pr), jnp.stack(pi)])
    l8 = jnp.stack([jnp.broadcast_to(pr[7], (SUBLANES, S5_LANES)),
                    jnp.broadcast_to(pi[7], (SUBLANES, S5_LANES))])
    return b_blk, c_blk, ak, pw, l8


S5_SCAN_W = 256


def _s5_kernel(chained, tt, *refs):
    if chained:
        (u_ref, z_ref, bblk_ref, cblk_ref, ak_ref, pw_ref, l8_ref, d_ref, wglu_ref, bglu_ref,
         y_ref, hr_ref, hi_ref, scr, car) = refs
    else:
        (u_ref, z_ref, h0r_ref, h0i_ref, bblk_ref, cblk_ref, ak_ref, pw_ref, l8_ref, d_ref,
         wglu_ref, bglu_ref, y_ref, hr_ref, hi_ref, scr) = refs
    t = pl.program_id(1) if chained else None
    w = S5_SCAN_W

    if chained:
        @pl.when(t == 0)
        def _():
            car[...] = jnp.zeros_like(car)

    for kb in range(S5_KB):
        res = jnp.dot(u_ref[:, kb * S5_KW:(kb + 1) * S5_KW], bblk_ref[kb], preferred_element_type=F32)
        scr[:, kb * S5_NW:(kb + 1) * S5_NW] = res[:, :S5_NW]
        scr[:, S5_LANES + kb * S5_NW:S5_LANES + (kb + 1) * S5_NW] = res[:, S5_NW:]

    for c in range(S5_LANES // w):
        sl_re = slice(c * w, (c + 1) * w)
        sl_im = slice(S5_LANES + c * w, S5_LANES + (c + 1) * w)
        coef = [(ak_ref[k, 0, :, sl_re], ak_ref[k, 1, :, sl_re]) for k in range(3)]
        pwr, pwi = pw_ref[0, :, sl_re], pw_ref[1, :, sl_re]
        l8r, l8i = l8_ref[0, :, sl_re], l8_ref[1, :, sl_re]

        def body(r, carry, sl_re=sl_re, sl_im=sl_im, coef=coef, pwr=pwr, pwi=pwi, l8r=l8r, l8i=l8i):
            row = pl.multiple_of(r * SUBLANES, SUBLANES)
            xr = scr[pl.ds(row, SUBLANES), sl_re]
            xi = scr[pl.ds(row, SUBLANES), sl_im]
            for k, (ar, ai) in zip((1, 2, 4), coef):
                sr = pltpu.roll(xr, k, 0)
                si = pltpu.roll(xi, k, 0)
                xr, xi = xr + (ar * sr - ai * si), xi + (ar * si + ai * sr)
            if chained:
                cr, ci = carry
            else:
                cr = jnp.broadcast_to(h0r_ref[pl.ds(r, 1), sl_re], (SUBLANES, w))
                ci = jnp.broadcast_to(h0i_ref[pl.ds(r, 1), sl_re], (SUBLANES, w))
            scr[pl.ds(row, SUBLANES), sl_re] = xr + (pwr * cr - pwi * ci)
            scr[pl.ds(row, SUBLANES), sl_im] = xi + (pwr * ci + pwi * cr)
            if not chained:
                return carry
            l7r = jnp.broadcast_to(xr[SUBLANES - 1:SUBLANES, :], (SUBLANES, w))
            l7i = jnp.broadcast_to(xi[SUBLANES - 1:SUBLANES, :], (SUBLANES, w))
            return l7r + (l8r * cr - l8i * ci), l7i + (l8r * ci + l8i * cr)

        if chained:
            cf = lax.fori_loop(0, tt // SUBLANES, body, (car[0, :, sl_re], car[1, :, sl_re]))
            car[0, :, sl_re] = cf[0]
            car[1, :, sl_re] = cf[1]
        else:
            lax.fori_loop(0, tt // SUBLANES, body, 0)

    u = u_ref[...].astype(F32)
    parts = []
    for kb in range(S5_KB):
        hcat = jnp.concatenate(
            [scr[:, kb * S5_NW:(kb + 1) * S5_NW],
             scr[:, S5_LANES + kb * S5_NW:S5_LANES + (kb + 1) * S5_NW]], axis=1).astype(BF16)
        parts.append(jnp.dot(hcat, cblk_ref[kb], preferred_element_type=F32))
    y = jnp.concatenate(parts, axis=1) + d_ref[...] * u
    y = jax.nn.gelu(y, approximate=True)
    glu = jnp.dot(y.astype(BF16), wglu_ref[...], preferred_element_type=F32) + bglu_ref[...]
    y = y * _sigmoid(glu)
    y_ref[...] = (y * _silu(z_ref[...].astype(F32))).astype(y_ref.dtype)

    if chained:
        @pl.when(t == pl.num_programs(1) - 1)
        def _():
            hr_ref[0] = car[0, 0:1, :]
            hi_ref[0] = car[1, 0:1, :]
    else:
        nseq = tt // DEC_SEQ
        hr_ref[...] = _last_rows(scr[:, 0:S5_LANES], nseq)
        hi_ref[...] = _last_rows(scr[:, S5_LANES:2 * S5_LANES], nseq)


def _s5_call(proj, prep, d, wglu, bglu, h0=None, tt=256):
    b_blk, c_blk, ak, pw, l8 = prep
    chained = h0 is None
    const = lambda shape: pl.BlockSpec(shape, lambda *_: (0,) * len(shape))
    w_specs = [const((S5_KB, S5_KW, 2 * S5_NW)), const((S5_KB, 2 * S5_NW, S5_KW)),
               const((3, 2, SUBLANES, S5_LANES)), const((2, SUBLANES, S5_LANES)),
               const((2, SUBLANES, S5_LANES)), const((1, S5_WIDTH)),
               const((S5_WIDTH, S5_WIDTH)), const((1, S5_WIDTH))]
    w_args = [b_blk, c_blk, ak, pw, l8, d.reshape(1, -1), wglu, bglu.reshape(1, -1)]
    scratch = [pltpu.VMEM((tt, 2 * S5_LANES), F32)]
    if chained:
        nt = SEQ // tt
        grid = (BATCH, nt)
        row = lambda b, t: b * nt + t
        in_specs = [pl.BlockSpec((tt, 1024), lambda b, t: (row(b, t), COL_U_S5)),
                    pl.BlockSpec((tt, 1024), lambda b, t: (row(b, t), COL_Z_S5))] + w_specs
        args = [proj, proj] + w_args
        out_specs = [pl.BlockSpec((tt, 1024), lambda b, t: (row(b, t), 0)),
                     pl.BlockSpec((1, 1, S5_LANES), lambda b, t: (b, 0, 0)),
                     pl.BlockSpec((1, 1, S5_LANES), lambda b, t: (b, 0, 0))]
        out_shape = [jax.ShapeDtypeStruct((N_PROMPT, 1024), BF16),
                     jax.ShapeDtypeStruct((BATCH, 1, S5_LANES), F32),
                     jax.ShapeDtypeStruct((BATCH, 1, S5_LANES), F32)]
        scratch.append(pltpu.VMEM((2, SUBLANES, S5_LANES), F32))
        sem = ("parallel", "arbitrary")
    else:
        h0r, h0i = h0
        nseq = tt // DEC_SEQ
        base = N_PROMPT // tt
        grid = (N_SAMPLE // tt,)
        in_specs = [pl.BlockSpec((tt, 1024), lambda i: (base + i, COL_U_S5)),
                    pl.BlockSpec((tt, 1024), lambda i: (base + i, COL_Z_S5)),
                    pl.BlockSpec((nseq, S5_LANES), lambda i: (i, 0)),
                    pl.BlockSpec((nseq, S5_LANES), lambda i: (i, 0))] + w_specs
        args = [proj, proj, h0r, h0i] + w_args
        out_specs = [pl.BlockSpec((tt, 1024), lambda i: (i, 0)),
                     pl.BlockSpec((nseq, S5_LANES), lambda i: (i, 0)),
                     pl.BlockSpec((nseq, S5_LANES), lambda i: (i, 0))]
        out_shape = [jax.ShapeDtypeStruct((N_SAMPLE, 1024), BF16),
                     jax.ShapeDtypeStruct((DEC_BATCH, S5_LANES), F32),
                     jax.ShapeDtypeStruct((DEC_BATCH, S5_LANES), F32)]
        sem = ("parallel",)
    return pl.pallas_call(
        functools.partial(_s5_kernel, chained, tt),
        grid=grid, in_specs=in_specs, out_specs=out_specs, out_shape=out_shape,
        scratch_shapes=scratch, compiler_params=_params(sem),
        name="s5_prompt" if chained else "s5_sample",
    )(*args)


LRU_SCAN_W = 256


def _lru_kernel(chained, tt, *refs):
    if chained:
        (u_ref, z_ref, cw_ref, cb_ref, wg_ref, ba_ref, bx_ref, lam_ref,
         y_ref, hl_ref, ext, a_scr, b_scr, car) = refs
    else:
        (u_ref, z_ref, h0_ref, cbuf_ref, cw_ref, cb_ref, wg_ref, ba_ref, bx_ref, lam_ref,
         y_ref, hl_ref, ext, a_scr, b_scr) = refs
    t = pl.program_id(1) if chained else None
    nrg = tt // SUBLANES

    if chained:
        @pl.when(t == 0)
        def _():
            ext[0:SUBLANES, :] = jnp.zeros((SUBLANES, LRU_WIDTH), F32)
            car[...] = jnp.zeros_like(car)

    ext[SUBLANES:SUBLANES + tt, :] = u_ref[...].astype(F32)

    rowi = lax.broadcasted_iota(jnp.int32, (SUBLANES, LRU_WIDTH), 0)
    cws = [cw_ref[k:k + 1, :] for k in range(CONV_WIDTH)]
    cbias = cb_ref[...]

    def conv_body(r, _):
        row = pl.multiple_of(r * SUBLANES, SUBLANES)
        cur = ext[pl.ds(row + SUBLANES, SUBLANES), :]
        prev = ext[pl.ds(row, SUBLANES), :] if chained else cbuf_ref[pl.ds(row, SUBLANES), :]
        acc = cws[CONV_WIDTH - 1] * cur + cbias
        for s in range(1, CONV_WIDTH):
            merged = jnp.where(rowi >= SUBLANES - s, prev, cur)
            acc = acc + cws[CONV_WIDTH - 1 - s] * pltpu.roll(merged, s, 0)
        b_scr[pl.ds(row, SUBLANES), :] = acc
        return 0

    lax.fori_loop(0, nrg, conv_body, 0)
    if chained:
        ext[0:SUBLANES, :] = ext[tt:tt + SUBLANES, :]

    xc = b_scr[...]
    xcb = xc.astype(BF16)
    ra, rx = [], []
    for n in range(LRU_BLOCKS):
        res = jnp.dot(xcb[:, n * LRU_BLOCK:(n + 1) * LRU_BLOCK], wg_ref[n], preferred_element_type=F32)
        ra.append(res[:, :LRU_BLOCK])
        rx.append(res[:, LRU_BLOCK:])
    r_gate = _sigmoid(jnp.concatenate(ra, axis=1) + ba_ref[...])
    i_gate = _sigmoid(jnp.concatenate(rx, axis=1) + bx_ref[...])
    nl = -lam_ref[...]
    softplus = jnp.maximum(nl, 0.0) + jnp.log1p(jnp.exp(-jnp.abs(nl)))
    log_a = -LRU_C * r_gate * softplus
    a = jnp.exp(log_a)
    a_scr[...] = a
    b_scr[...] = jnp.sqrt(1.0 - a * a) * (i_gate * xc)

    w = LRU_SCAN_W
    rw = lax.broadcasted_iota(jnp.int32, (SUBLANES, w), 0)
    for c in range(LRU_WIDTH // w):
        sl = slice(c * w, (c + 1) * w)

        def body(r, carry, sl=sl):
            row = pl.multiple_of(r * SUBLANES, SUBLANES)
            av = a_scr[pl.ds(row, SUBLANES), sl]
            bv = b_scr[pl.ds(row, SUBLANES), sl]
            for k in (1, 2, 4):
                sa = jnp.where(rw >= k, pltpu.roll(av, k, 0), 1.0)
                sb = jnp.where(rw >= k, pltpu.roll(bv, k, 0), 0.0)
                bv = bv + av * sb
                av = av * sa
            if chained:
                cv = carry
            else:
                cv = jnp.broadcast_to(h0_ref[pl.ds(r, 1), sl], (SUBLANES, w))
            h = bv + av * cv
            b_scr[pl.ds(row, SUBLANES), sl] = h
            if not chained:
                return carry
            return jnp.broadcast_to(h[SUBLANES - 1:SUBLANES, :], (SUBLANES, w))

        if chained:
            car[:, sl] = lax.fori_loop(0, nrg, body, car[:, sl])
        else:
            lax.fori_loop(0, nrg, body, 0)

    y_ref[...] = (b_scr[...] * _silu(z_ref[...].astype(F32))).astype(y_ref.dtype)
    if chained:
        @pl.when(t == pl.num_programs(1) - 1)
        def _():
            hl_ref[0] = car[0:1, :]
    else:
        hl_ref[...] = _last_rows(b_scr[...], tt // DEC_SEQ)


def _lru_call(proj, conv_w, conv_b, wg, b_a, b_x, lam, state=None, tt=256):
    chained = state is None
    const = lambda shape: pl.BlockSpec(shape, lambda *_: (0,) * len(shape))
    w_specs = [const((CONV_WIDTH, LRU_WIDTH)), const((1, LRU_WIDTH)),
               const((LRU_BLOCKS, LRU_BLOCK, 2 * LRU_BLOCK)),
               const((1, LRU_WIDTH)), const((1, LRU_WIDTH)), const((1, LRU_WIDTH))]
    w_args = [conv_w, conv_b.reshape(1, -1), wg, b_a.reshape(1, -1), b_x.reshape(1, -1), lam.reshape(1, -1)]
    scratch = [pltpu.VMEM((tt + SUBLANES, LRU_WIDTH), F32), pltpu.VMEM((tt, LRU_WIDTH), F32),
               pltpu.VMEM((tt, LRU_WIDTH), F32)]
    if chained:
        nt = SEQ // tt
        grid = (BATCH, nt)
        row = lambda b, t: b * nt + t
        in_specs = [pl.BlockSpec((tt, 1024), lambda b, t: (row(b, t), COL_U_LRU)),
                    pl.BlockSpec((tt, 1024), lambda b, t: (row(b, t), COL_Z_LRU))] + w_specs
        args = [proj, proj] + w_args
        out_specs = [pl.BlockSpec((tt, 1024), lambda b, t: (row(b, t), 0)),
                     pl.BlockSpec((1, 1, LRU_WIDTH), lambda b, t: (b, 0, 0))]
        out_shape = [jax.ShapeDtypeStruct((N_PROMPT, 1024), BF16),
                     jax.ShapeDtypeStruct((BATCH, 1, LRU_WIDTH), F32)]
        scratch.append(pltpu.VMEM((SUBLANES, LRU_WIDTH), F32))
        sem = ("parallel", "arbitrary")
    else:
        h0, cbuf = state
        nseq = tt // DEC_SEQ
        base = N_PROMPT // tt
        grid = (N_SAMPLE // tt,)
        in_specs = [pl.BlockSpec((tt, 1024), lambda i: (base + i, COL_U_LRU)),
                    pl.BlockSpec((tt, 1024), lambda i: (base + i, COL_Z_LRU)),
                    pl.BlockSpec((nseq, LRU_WIDTH), lambda i: (i, 0)),
                    pl.BlockSpec((tt, LRU_WIDTH), lambda i: (i, 0))] + w_specs
        args = [proj, proj, h0, cbuf] + w_args
        out_specs = [pl.BlockSpec((tt, 1024), lambda i: (i, 0)),
                     pl.BlockSpec((nseq, LRU_WIDTH), lambda i: (i, 0))]
        out_shape = [jax.ShapeDtypeStruct((N_SAMPLE, 1024), BF16),
                     jax.ShapeDtypeStruct((DEC_BATCH, LRU_WIDTH), F32)]
        sem = ("parallel",)
    return pl.pallas_call(
        functools.partial(_lru_kernel, chained, tt),
        grid=grid, in_specs=in_specs, out_specs=out_specs, out_shape=out_shape,
        scratch_shapes=scratch, compiler_params=_params(sem),
        name="lru_prompt" if chained else "lru_sample",
    )(*args)


RET_ROWS = 128
RET_SEQS = RET_ROWS // DEC_SEQ


def _ret_tables(chunk, pos0):
    log_g = jnp.log1p(-jnp.exp2(-5.0 - jnp.arange(RET_HEADS, dtype=F32)))
    rows = jnp.arange(RET_ROWS)
    seq, idx = rows // chunk, (rows % chunk).astype(F32)
    diff = idx[:, None] - idx[None, :]
    same = seq[:, None] == seq[None, :]
    dmask = jnp.where((diff[None] >= 0) & same[None],
                      jnp.exp(jnp.maximum(diff, 0.0)[None] * log_g[:, None, None]), 0.0)
    xi = jnp.exp((idx[None, :] + 1.0) * log_g[:, None])
    zeta = jnp.exp((chunk - 1.0 - idx[None, :]) * log_g[:, None])
    gch = jnp.exp(chunk * log_g)
    full = lambda t: jnp.broadcast_to(t[:, :, None], (RET_HEADS, RET_ROWS, LANES))
    gc = jnp.broadcast_to(gch[:, None, None], (RET_HEADS, SUBLANES, LANES))
    return dmask, full(xi), full(zeta), gc


def _rope_tables(pos):
    half = RET_DK // 2
    freq = ROPE_BASE ** (-jnp.arange(half, dtype=F32) / half)
    ang = pos[:, None] * freq[None, :]
    cos, sin = jnp.cos(ang), jnp.sin(ang)
    return jnp.concatenate([cos, cos], axis=-1), jnp.concatenate([-sin, sin], axis=-1)


def _rope(x, cosf, sinf):
    return x * cosf + pltpu.roll(x, RET_DK // 2, 1) * sinf


def _ret_head(h, q_ref, k_ref, v_ref, z_ref, cos, sin, dmask_ref, xi_ref, zeta_ref, gng_ref):
    sl = slice(h * RET_DK, (h + 1) * RET_DK)
    qb = _rope(q_ref[:, sl].astype(F32), cos, sin).astype(BF16)
    kh = _rope(k_ref[:, sl].astype(F32), cos, sin) * (RET_DK ** -0.5)
    vb = v_ref[:, sl]
    sc = lax.dot_general(qb, kh.astype(BF16), (((1,), (1,)), ((), ())), preferred_element_type=F32)
    inner = jnp.dot((sc * dmask_ref[h]).astype(BF16), vb, preferred_element_type=F32)
    kz = (kh * zeta_ref[h]).astype(BF16)
    return sl, qb, kz, vb, inner


def _ret_finish(o, sl, z_ref, gng_ref, y_ref):
    mu = jnp.mean(o, axis=-1, keepdims=True)
    oc = o - mu
    var = jnp.mean(oc * oc, axis=-1, keepdims=True)
    on = oc * lax.rsqrt(var + GN_EPS) * gng_ref[:, sl]
    y_ref[:, sl] = (on * _silu(z_ref[:, sl].astype(F32))).astype(y_ref.dtype)


def _ret_prompt_kernel(q_ref, k_ref, v_ref, z_ref, cos_ref, sin_ref, dmask_ref, xi_ref, zeta_ref,
                       gc_ref, gng_ref, y_ref, r_ref):
    @pl.when(pl.program_id(1) == 0)
    def _():
        r_ref[...] = jnp.zeros_like(r_ref)

    cos, sin = cos_ref[...], sin_ref[...]
    for h in range(RET_HEADS):
        sl, qb, kz, vb, inner = _ret_head(h, q_ref, k_ref, v_ref, z_ref, cos, sin,
                                          dmask_ref, xi_ref, zeta_ref, gng_ref)
        r = r_ref[0, h]
        cross = jnp.dot(qb, r.astype(BF16), preferred_element_type=F32) * xi_ref[h]
        upd = lax.dot_general(kz, vb, (((0,), (0,)), ((), ())), preferred_element_type=F32)
        r_ref[0, h] = r * gc_ref[h, 0:1, :] + upd
        _ret_finish(inner + cross, sl, z_ref, gng_ref, y_ref)


def _ret_sample_kernel(q_ref, k_ref, v_ref, z_ref, r0_ref, cos_ref, sin_ref, dmask_ref, xi_ref,
                       zeta_ref, gc_ref, gng_ref, y_ref, r_ref):
    cos, sin = cos_ref[...], sin_ref[...]
    rowseq = lax.broadcasted_iota(jnp.int32, (RET_ROWS, RET_DV), 0) // DEC_SEQ
    for h in range(RET_HEADS):
        sl, qb, kz, vb, inner = _ret_head(h, q_ref, k_ref, v_ref, z_ref, cos, sin,
                                          dmask_ref, xi_ref, zeta_ref, gng_ref)
        rcat = jnp.concatenate([r0_ref[s, h] for s in range(RET_SEQS)], axis=1).astype(BF16)
        call = jnp.dot(qb, rcat, preferred_element_type=F32)
        cross = jnp.concatenate(
            [call[s * DEC_SEQ:(s + 1) * DEC_SEQ, s * RET_DV:(s + 1) * RET_DV] for s in range(RET_SEQS)],
            axis=0) * xi_ref[h]
        vf = vb.astype(F32)
        vexp = jnp.concatenate([jnp.where(rowseq == s, vf, 0.0) for s in range(RET_SEQS)],
                               axis=1).astype(BF16)
        upd = lax.dot_general(kz, vexp, (((0,), (0,)), ((), ())), preferred_element_type=F32)
        gc = gc_ref[h, 0:1, :]
        for s in range(RET_SEQS):
            r_ref[s, h] = r0_ref[s, h] * gc + upd[:, s * RET_DV:(s + 1) * RET_DV]
        _ret_finish(inner + cross, sl, z_ref, gng_ref, y_ref)


def _ret_call(proj, gn_g, r0=None, layer=None):
    prompt = r0 is None
    const = lambda shape: pl.BlockSpec(shape, lambda *_: (0,) * len(shape))
    if prompt:
        tabs = _ret_tables(RET_CHUNK, 0)
        cosf, sinf = _rope_tables(jnp.arange(SEQ, dtype=F32) + 0.0)
    else:
        tabs = _ret_tables(DEC_SEQ, PAST_LEN)
        pos = jnp.arange(DEC_SEQ, dtype=F32) + float(PAST_LEN)
        cosf, sinf = _rope_tables(jnp.tile(pos, RET_SEQS))
    t_specs = [const((RET_HEADS, RET_ROWS, RET_ROWS)), const((RET_HEADS, RET_ROWS, LANES)),
               const((RET_HEADS, RET_ROWS, LANES)), const((RET_HEADS, SUBLANES, LANES)),
               const((1, RET_WIDTH))]
    t_args = list(tabs) + [gn_g.reshape(1, -1)]
    if prompt:
        nc = SEQ // RET_CHUNK
        row = lambda b, c: b * nc + c
        col = lambda j: (lambda b, c: (row(b, c), j))
        in_specs = [pl.BlockSpec((RET_ROWS, 1024), col(COL_Q)), pl.BlockSpec((RET_ROWS, 1024), col(COL_K)),
                    pl.BlockSpec((RET_ROWS, 1024), col(COL_V)), pl.BlockSpec((RET_ROWS, 1024), col(COL_Z_RET)),
                    pl.BlockSpec((RET_ROWS, LANES), lambda b, c: (c, 0)),
                    pl.BlockSpec((RET_ROWS, LANES), lambda b, c: (c, 0))] + t_specs
        args = [proj, proj, proj, proj, cosf, sinf] + t_args
        return pl.pallas_call(
            _ret_prompt_kernel, grid=(BATCH, nc), in_specs=in_specs,
            out_specs=[pl.BlockSpec((RET_ROWS, 1024), lambda b, c: (row(b, c), 0)),
                       pl.BlockSpec((1, RET_HEADS, RET_DK, RET_DV), lambda b, c: (b, 0, 0, 0))],
            out_shape=[jax.ShapeDtypeStruct((N_PROMPT, 1024), BF16),
                       jax.ShapeDtypeStruct((BATCH, RET_HEADS, RET_DK, RET_DV), F32)],
            compiler_params=_params(("parallel", "arbitrary")), name="ret_prompt",
        )(*args)
    base = N_PROMPT // RET_ROWS
    col = lambda j: (lambda i: (base + i, j))
    st_block = (None, RET_SEQS, RET_HEADS, RET_DK, RET_DV)
    in_specs = [pl.BlockSpec((RET_ROWS, 1024), col(COL_Q)), pl.BlockSpec((RET_ROWS, 1024), col(COL_K)),
                pl.BlockSpec((RET_ROWS, 1024), col(COL_V)), pl.BlockSpec((RET_ROWS, 1024), col(COL_Z_RET)),
                pl.BlockSpec(st_block, lambda i: (layer, i, 0, 0, 0)),
                const((RET_ROWS, LANES)), const((RET_ROWS, LANES))] + t_specs
    args = [proj, proj, proj, proj, r0, cosf, sinf] + t_args
    return pl.pallas_call(
        _ret_sample_kernel, grid=(N_SAMPLE // RET_ROWS,), in_specs=in_specs,
        out_specs=[pl.BlockSpec((RET_ROWS, 1024), lambda i: (i, 0)),
                   pl.BlockSpec(st_block[1:], lambda i: (i, 0, 0, 0))],
        out_shape=[jax.ShapeDtypeStruct((N_SAMPLE, 1024), BF16),
                   jax.ShapeDtypeStruct((DEC_BATCH, RET_HEADS, RET_DK, RET_DV), F32)],
        compiler_params=_params(("parallel",)), name="ret_sample",
    )(*args)


def _merge_kernel(ys_ref, yl_ref, yr_ref, gs_ref, gl_ref, gr_ref, wb_ref, o_ref):
    acc = None
    for m, (y_ref, g_ref) in enumerate(((ys_ref, gs_ref), (yl_ref, gl_ref), (yr_ref, gr_ref))):
        b = jnp.dot(y_ref[...], wb_ref[m], preferred_element_type=F32)
        term = _sigmoid(g_ref[...].astype(F32)) * b
        acc = term if acc is None else acc + term
    o_ref[...] = acc.astype(o_ref.dtype)


def _merge_call(y_s5, y_lru, y_ret, proj, wb, tm=512):
    n = N_TOK
    ysp = pl.BlockSpec((tm, 1024), lambda i: (i, 0))
    gsp = lambda j: pl.BlockSpec((tm, D_MODEL), lambda i: (i, COL_GATES + j))
    return pl.pallas_call(
        _merge_kernel, grid=(n // tm,),
        in_specs=[ysp, ysp, ysp, gsp(0), gsp(1), gsp(2),
                  pl.BlockSpec((3, 1024, D_MODEL), lambda i: (0, 0, 0))],
        out_specs=pl.BlockSpec((tm, D_MODEL), lambda i: (i, 0)),
        out_shape=jax.ShapeDtypeStruct((n, D_MODEL), BF16),
        compiler_params=_params(("parallel",)), name="merge",
    )(y_s5, y_lru, y_ret, proj, proj, proj, wb)


def _outproj_kernel(m_ref, w_ref, x_ref, g_ref, xo_ref, xn_ref):
    x = x_ref[...] + jnp.dot(m_ref[...], w_ref[...], preferred_element_type=F32)
    xo_ref[...] = x
    ms = jnp.mean(x * x, axis=-1, keepdims=True)
    xn_ref[...] = (x * lax.rsqrt(ms + NORM_EPS) * g_ref[...]).astype(xn_ref.dtype)


def _outproj_call(merged, w_out, x, g_next, norm_dtype, tm=512):
    n = N_TOK
    tok = pl.BlockSpec((tm, D_MODEL), lambda i: (i, 0))
    return pl.pallas_call(
        _outproj_kernel, grid=(n // tm,),
        in_specs=[tok, pl.BlockSpec((D_MODEL, D_MODEL), lambda i: (0, 0)), tok,
                  pl.BlockSpec((1, D_MODEL), lambda i: (0, 0))],
        out_specs=[tok, tok],
        out_shape=[jax.ShapeDtypeStruct((n, D_MODEL), F32), jax.ShapeDtypeStruct((n, D_MODEL), norm_dtype)],
        compiler_params=_params(("parallel",)), name="outproj",
    )(merged, w_out, x, g_next.reshape(1, D_MODEL))


def kernel(x_prompt, x_sample, state_s5_re, state_s5_im, state_lru, state_conv, state_ret, norm_g, w_in, s5_lambda_re, s5_lambda_im, s5_log_dt, s5_b_re, s5_b_im, s5_c_re, s5_c_im, s5_d, s5_w_glu, s5_b_glu, lru_conv_w, lru_conv_b, lru_w_a, lru_b_a, lru_w_x, lru_b_x, lru_lambda, ret_gn_g, w_branch_s5, w_branch_lru, w_branch_ret, w_out, final_norm_g):
    x = jnp.concatenate([x_prompt.reshape(N_PROMPT, D_MODEL), x_sample.reshape(N_SAMPLE, D_MODEL)], axis=0)
    xn = _rmsnorm_call(x, norm_g[0], BF16)
    outs_p = [[] for _ in range(5)]
    outs_s = [[] for _ in range(5)]
    for l in range(DEPTH):
        proj = _inproj_call(xn, w_in, l)

        prep = _s5_prep(s5_lambda_re[l], s5_lambda_im[l], s5_log_dt[l], s5_b_re[l], s5_b_im[l],
                        s5_c_re[l], s5_c_im[l])
        wglu = s5_w_glu[l].astype(BF16)
        ys_p, hr_p, hi_p = _s5_call(proj, prep, s5_d[l], wglu, s5_b_glu[l])
        ys_s, hr_s, hi_s = _s5_call(proj, prep, s5_d[l], wglu, s5_b_glu[l],
                                    h0=(state_s5_re[l].reshape(DEC_BATCH, S5_LANES),
                                        state_s5_im[l].reshape(DEC_BATCH, S5_LANES)))

        wg = jnp.concatenate([lru_w_a[l], lru_w_x[l]], axis=-1).astype(BF16)
        lru_w = (lru_conv_w[l], lru_conv_b[l], wg, lru_b_a[l], lru_b_x[l], lru_lambda[l])
        yl_p, hl_p = _lru_call(proj, *lru_w)
        cbuf = jnp.pad(state_conv[l], ((0, 0), (DEC_SEQ - (CONV_WIDTH - 1), 0), (0, 0)))
        yl_s, hl_s = _lru_call(proj, *lru_w, state=(state_lru[l], cbuf.reshape(N_SAMPLE, LRU_WIDTH)))

        yr_p, r_p = _ret_call(proj, ret_gn_g[l])
        yr_s, r_s = _ret_call(proj, ret_gn_g[l], r0=state_ret, layer=l)

        y_s5 = jnp.concatenate([ys_p, ys_s], axis=0)
        y_lru = jnp.concatenate([yl_p, yl_s], axis=0)
        y_ret = jnp.concatenate([yr_p, yr_s], axis=0)
        wb = jnp.stack([w_branch_s5[l], w_branch_lru[l], w_branch_ret[l]]).astype(BF16)
        merged = _merge_call(y_s5, y_lru, y_ret, proj, wb)
        last = l == DEPTH - 1
        g_next = final_norm_g if last else norm_g[l + 1]
        x, xn = _outproj_call(merged, w_out[l].astype(BF16), x, g_next, F32 if last else BF16)

        u_lru = proj[:, COL_U_LRU * 1024:(COL_U_LRU + 1) * 1024]
        conv_p = u_lru[:N_PROMPT].reshape(BATCH, SEQ, LRU_WIDTH)[:, -(CONV_WIDTH - 1):].astype(F32)
        conv_s = u_lru[N_PROMPT:].reshape(DEC_BATCH, DEC_SEQ, LRU_WIDTH)[:, -(CONV_WIDTH - 1):].astype(F32)
        st = (S5_GROUPS, S5_STATE)
        for lst, vals in ((outs_p, (hr_p.reshape(BATCH, *st), hi_p.reshape(BATCH, *st),
                                    hl_p.reshape(BATCH, LRU_WIDTH), conv_p, r_p)),
                          (outs_s, (hr_s.reshape(DEC_BATCH, *st), hi_s.reshape(DEC_BATCH, *st),
                                    hl_s, conv_s, r_s))):
            for j in range(5):
                lst[j].append(vals[j])

    y_prompt = xn[:N_PROMPT].reshape(BATCH, SEQ, D_MODEL)
    y_sample = xn[N_PROMPT:].reshape(DEC_BATCH, DEC_SEQ, D_MODEL)
    sp = [jnp.stack(t, axis=0) for t in outs_p]
    ss = [jnp.stack(t, axis=0) for t in outs_s]
    return (y_prompt, y_sample, *sp, *ss)
```

```python
import functools

import jax
import jax.numpy as jnp
from jax import lax
from jax.experimental import pallas as pl
from jax.experimental.pallas import tpu as pltpu

F32 = jnp.float32
BF16 = jnp.bfloat16

D_MODEL = 2048
BATCH = 4
SEQ = 2048
DEPTH = 2
DEC_BATCH = 128
DEC_SEQ = 8
PAST_LEN = 16384
S5_WIDTH = 1024
S5_GROUP = 16
S5_GROUPS = 64
S5_STATE = 64
S5_LANES = S5_GROUPS * S5_STATE
LRU_WIDTH = 1024
LRU_BLOCKS = 8
LRU_BLOCK = 128
CONV_WIDTH = 4
LRU_C = 8.0
RET_HEADS = 8
RET_DK = 128
RET_DV = 128
RET_WIDTH = 1024
ROPE_BASE = 10000.0
NORM_EPS = 1e-6
GN_EPS = 1e-5
N_IN = 14336

N_PROMPT = BATCH * SEQ
N_SAMPLE = DEC_BATCH * DEC_SEQ
N_TOK = N_PROMPT + N_SAMPLE

SUBLANES = 8
LANES = 128
VMEM_LIMIT = 56 * 1024 * 1024

TILE = 256
SEG = TILE // SUBLANES
TILES = SEQ // TILE

COL_U_S5, COL_Z_S5, COL_U_LRU, COL_Z_LRU, COL_Q, COL_K, COL_V, COL_Z_RET = range(8)
COL_GATES = 4

S5_KB = 4
S5_KW = S5_WIDTH // S5_KB
S5_NW = S5_LANES // S5_KB
SCAN_W = 512

SAMPLE_ROW0 = N_PROMPT // DEC_BATCH


def _params(sem, vmem=VMEM_LIMIT):
    return pltpu.CompilerParams(dimension_semantics=sem, vmem_limit_bytes=vmem)


def _const_spec(shape):
    return pl.BlockSpec(shape, lambda *_: (0,) * len(shape))


def _sigmoid(x):
    return jax.nn.sigmoid(x)


def _silu(x):
    return x * jax.nn.sigmoid(x)


def _bcast_row(x, row):
    return jnp.broadcast_to(x[row:row + 1, :], x.shape)


def _cmul(ar, ai, br, bi):
    return ar * br - ai * bi, ar * bi + ai * br


def _to_rows(x_prompt, x_sample):
    xp = x_prompt.reshape(BATCH, TILES, SUBLANES, SEG, -1).transpose(0, 1, 3, 2, 4).reshape(N_PROMPT, -1)
    xs = x_sample.transpose(1, 0, 2).reshape(N_SAMPLE, -1)
    return xp, xs


def _from_rows(y):
    yp = y[:N_PROMPT].reshape(BATCH, TILES, SEG, SUBLANES, -1).transpose(0, 1, 3, 2, 4).reshape(BATCH, SEQ, -1)
    ys = y[N_PROMPT:].reshape(DEC_SEQ, DEC_BATCH, -1).transpose(1, 0, 2)
    return yp, ys


def _two_part_specs(tm, width):
    na = N_PROMPT // tm
    return [pl.BlockSpec((tm, width), lambda i: (jnp.minimum(i, na - 1), 0)),
            pl.BlockSpec((tm, width), lambda i: (jnp.maximum(i - na, 0), 0))], na


def _pick(na, a_ref, b_ref):
    return jnp.where(pl.program_id(0) < na, a_ref[...], b_ref[...])


def _norm_kernel(na, xa_ref, xb_ref, g_ref, x_ref, o_ref):
    x = _pick(na, xa_ref, xb_ref)
    x_ref[...] = x
    ms = jnp.mean(x * x, axis=-1, keepdims=True)
    o_ref[...] = (x * lax.rsqrt(ms + NORM_EPS) * g_ref[...]).astype(o_ref.dtype)


def _rmsnorm_call(xp, xs, g, tm=512):
    specs, na = _two_part_specs(tm, D_MODEL)
    tok = pl.BlockSpec((tm, D_MODEL), lambda i: (i, 0))
    return pl.pallas_call(
        functools.partial(_norm_kernel, na),
        grid=(N_TOK // tm,),
        in_specs=specs + [_const_spec((1, D_MODEL))],
        out_specs=[tok, tok],
        out_shape=[jax.ShapeDtypeStruct((N_TOK, D_MODEL), F32), jax.ShapeDtypeStruct((N_TOK, D_MODEL), BF16)],
        compiler_params=_params(("parallel",)),
        name="rmsnorm",
    )(xp, xs, g.reshape(1, D_MODEL))


def _inproj_kernel(xn_ref, w_ref, o_ref, wbf_ref):
    @pl.when(pl.program_id(1) == 0)
    def _():
        wbf_ref[...] = w_ref[...].astype(BF16)

    o_ref[...] = jnp.dot(xn_ref[...], wbf_ref[...], preferred_element_type=F32).astype(o_ref.dtype)


def _inproj_call(xn, w_in, layer, tm=1024, tn=1024):
    n = xn.shape[0]
    return pl.pallas_call(
        _inproj_kernel,
        grid=(N_IN // tn, n // tm),
        in_specs=[pl.BlockSpec((tm, D_MODEL), lambda j, i: (i, 0)),
                  pl.BlockSpec((None, D_MODEL, tn), lambda j, i: (layer, 0, j))],
        out_specs=pl.BlockSpec((tm, tn), lambda j, i: (i, j)),
        out_shape=jax.ShapeDtypeStruct((n, N_IN), BF16),
        scratch_shapes=[pltpu.VMEM((D_MODEL, tn), BF16)],
        compiler_params=_params(("parallel", "arbitrary")),
        name="inproj",
    )(xn, w_in)


def _s5_prep(lam_re, lam_im, log_dt, b_re, b_im, c_re, c_im):
    dt = jnp.exp(log_dt)[:, None]
    e = jnp.exp(lam_re * dt)
    lbr = e * jnp.cos(lam_im * dt)
    lbi = e * jnp.sin(lam_im * dt)
    nr, ni = lbr - 1.0, lbi
    den = lam_re * lam_re + lam_im * lam_im
    cr = (nr * lam_re + ni * lam_im) / den
    ci = (ni * lam_re - nr * lam_im) / den
    bbr = cr[..., None] * b_re - ci[..., None] * b_im
    bbi = cr[..., None] * b_im + ci[..., None] * b_re
    eye = jnp.eye(S5_GROUPS // S5_KB, dtype=F32)

    def bblk(bb):
        t = bb.reshape(S5_KB, 16, S5_STATE, S5_GROUP)
        t = jnp.einsum('kgpc,gh->kgchp', t, eye)
        return t.reshape(S5_KB, S5_KW, S5_NW)

    def cblk(cc):
        t = cc.reshape(S5_KB, 16, S5_GROUP, S5_STATE)
        t = jnp.einsum('kgcp,gh->kgphc', t, eye)
        return t.reshape(S5_KB, S5_NW, S5_KW)

    b_blk = jnp.concatenate([bblk(bbr), bblk(bbi)], axis=2).astype(BF16)
    c_blk = jnp.concatenate([cblk(c_re), -cblk(c_im)], axis=1).astype(BF16)

    lr, li = lbr.reshape(-1), lbi.reshape(-1)
    full = lambda v: jnp.broadcast_to(v, (SUBLANES, S5_LANES))
    lam = jnp.stack([full(lr), full(li)])
    sr, si = lr, li
    for _ in range(SEG.bit_length() - 1):
        sr, si = _cmul(sr, si, sr, si)
    pr, pi = [sr], [si]
    for _ in range(SUBLANES - 1):
        r_, i_ = _cmul(pr[-1], pi[-1], sr, si)
        pr.append(r_)
        pi.append(i_)
    row = jnp.arange(SUBLANES)[:, None]
    ak = jnp.stack([jnp.stack([jnp.where(row >= k, pr[k - 1][None, :], 0.0),
                               jnp.where(row >= k, pi[k - 1][None, :], 0.0)]) for k in (1, 2, 4)])
    pw = jnp.stack([jnp.stack(pr), jnp.stack(pi)])
    return b_blk, c_blk, lam, ak, pw


def _s5_drive(u, bblk_ref, scr):
    for kb in range(S5_KB):
        res = jnp.dot(u[:, kb * S5_KW:(kb + 1) * S5_KW], bblk_ref[kb], preferred_element_type=F32)
        scr[:, kb * S5_NW:(kb + 1) * S5_NW] = res[:, :S5_NW]
        scr[:, S5_LANES + kb * S5_NW:S5_LANES + (kb + 1) * S5_NW] = res[:, S5_NW:]


def _s5_readout(scr, u, z, cblk_ref, d_ref, wglu_ref, bglu_ref):
    parts = []
    for kb in range(S5_KB):
        hcat = jnp.concatenate(
            [scr[:, kb * S5_NW:(kb + 1) * S5_NW],
             scr[:, S5_LANES + kb * S5_NW:S5_LANES + (kb + 1) * S5_NW]], axis=1).astype(BF16)
        parts.append(jnp.dot(hcat, cblk_ref[kb], preferred_element_type=F32))
    y = jnp.concatenate(parts, axis=1) + d_ref[...] * u.astype(F32)
    y = jax.nn.gelu(y, approximate=True)
    glu = jnp.dot(y.astype(BF16), wglu_ref[...], preferred_element_type=F32) + bglu_ref[...]
    y = y * _sigmoid(glu)
    return (y * _silu(z.astype(F32))).astype(BF16)


def _s5_prompt_kernel(u_ref, z_ref, bblk_ref, cblk_ref, lam_ref, ak_ref, pw_ref, d_ref, wglu_ref, bglu_ref,
                      y_ref, hr_ref, hi_ref, scr, car):
    t = pl.program_id(1)

    @pl.when(t == 0)
    def _():
        car[...] = jnp.zeros_like(car)

    _s5_drive(u_ref[...], bblk_ref, scr)

    w = SCAN_W
    rowi = lax.broadcasted_iota(jnp.int32, (SUBLANES, w), 0)
    zero = jnp.zeros((SUBLANES, w), F32)
    for c in range(S5_LANES // w):
        sl_re = slice(c * w, (c + 1) * w)
        sl_im = slice(S5_LANES + c * w, S5_LANES + (c + 1) * w)
        lr, li = lam_ref[0, :, sl_re], lam_ref[1, :, sl_re]

        def local(r, h, sl_re=sl_re, sl_im=sl_im, lr=lr, li=li):
            row = pl.multiple_of(r * SUBLANES, SUBLANES)
            pr, pi = _cmul(lr, li, h[0], h[1])
            nr = scr[pl.ds(row, SUBLANES), sl_re] + pr
            ni = scr[pl.ds(row, SUBLANES), sl_im] + pi
            scr[pl.ds(row, SUBLANES), sl_re] = nr
            scr[pl.ds(row, SUBLANES), sl_im] = ni
            return nr, ni

        xr, xi = lax.fori_loop(0, SEG, local, (zero, zero))

        for k, idx in zip((1, 2, 4), range(3)):
            pr, pi = _cmul(ak_ref[idx, 0, :, sl_re], ak_ref[idx, 1, :, sl_re],
                           pltpu.roll(xr, k, 0), pltpu.roll(xi, k, 0))
            xr, xi = xr + pr, xi + pi
        cr, ci = car[0, :, sl_re], car[1, :, sl_re]
        pr, pi = _cmul(pw_ref[0, :, sl_re], pw_ref[1, :, sl_re], cr, ci)
        fr, fi = xr + pr, xi + pi
        inr = jnp.where(rowi == 0, cr, pltpu.roll(fr, 1, 0))
        ini = jnp.where(rowi == 0, ci, pltpu.roll(fi, 1, 0))
        car[0, :, sl_re] = _bcast_row(fr, SUBLANES - 1)
        car[1, :, sl_re] = _bcast_row(fi, SUBLANES - 1)

        def fix(r, dcy, sl_re=sl_re, sl_im=sl_im, lr=lr, li=li):
            row = pl.multiple_of(r * SUBLANES, SUBLANES)
            dr, di = _cmul(lr, li, dcy[0], dcy[1])
            scr[pl.ds(row, SUBLANES), sl_re] = scr[pl.ds(row, SUBLANES), sl_re] + dr
            scr[pl.ds(row, SUBLANES), sl_im] = scr[pl.ds(row, SUBLANES), sl_im] + di
            return dr, di

        lax.fori_loop(0, SEG, fix, (inr, ini))

    y_ref[...] = _s5_readout(scr, u_ref[...], z_ref[...], cblk_ref, d_ref, wglu_ref, bglu_ref)

    @pl.when(t == pl.num_programs(1) - 1)
    def _():
        hr_ref[0] = car[0, 0:1, :]
        hi_ref[0] = car[1, 0:1, :]


S5_SB = 32


def _s5_sample_kernel(u_ref, z_ref, h0r_ref, h0i_ref, bblk_ref, cblk_ref, lam_ref, d_ref, wglu_ref, bglu_ref,
                      y_ref, hr_ref, hi_ref, scr):
    rows = DEC_SEQ * S5_SB
    u = u_ref[...].reshape(rows, S5_WIDTH)
    _s5_drive(u, bblk_ref, scr)

    w = SCAN_W
    for c in range(S5_LANES // w):
        sl_re = slice(c * w, (c + 1) * w)
        sl_im = slice(S5_LANES + c * w, S5_LANES + (c + 1) * w)
        lr, li = lam_ref[0, :, sl_re], lam_ref[1, :, sl_re]

        def body(g, _, sl_re=sl_re, sl_im=sl_im, lr=lr, li=li):
            row = pl.multiple_of(g * SUBLANES, SUBLANES)
            hr = h0r_ref[pl.ds(row, SUBLANES), sl_re]
            hi = h0i_ref[pl.ds(row, SUBLANES), sl_re]
            for j in range(DEC_SEQ):
                pr, pi = _cmul(lr, li, hr, hi)
                hr = scr[pl.ds(j * S5_SB + row, SUBLANES), sl_re] + pr
                hi = scr[pl.ds(j * S5_SB + row, SUBLANES), sl_im] + pi
                scr[pl.ds(j * S5_SB + row, SUBLANES), sl_re] = hr
                scr[pl.ds(j * S5_SB + row, SUBLANES), sl_im] = hi
            hr_ref[pl.ds(row, SUBLANES), sl_re] = hr
            hi_ref[pl.ds(row, SUBLANES), sl_re] = hi
            return 0

        lax.fori_loop(0, S5_SB // SUBLANES, body, 0)

    y = _s5_readout(scr, u, z_ref[...].reshape(rows, S5_WIDTH), cblk_ref, d_ref, wglu_ref, bglu_ref)
    y_ref[...] = y.reshape(DEC_SEQ, S5_SB, S5_WIDTH)


def _s5_calls(proj, prep, d, wglu, bglu, h0r, h0i):
    b_blk, c_blk, lam, ak, pw = prep
    lam_spec = _const_spec((2, SUBLANES, S5_LANES))
    tail_specs = [_const_spec((1, S5_WIDTH)), _const_spec((S5_WIDTH, S5_WIDTH)), _const_spec((1, S5_WIDTH))]
    tail_args = [d.reshape(1, -1), wglu, bglu.reshape(1, -1)]
    mm_specs = [_const_spec((S5_KB, S5_KW, 2 * S5_NW)), _const_spec((S5_KB, 2 * S5_NW, S5_KW))]

    row = lambda b, t: b * TILES + t
    yp, hr_p, hi_p = pl.pallas_call(
        _s5_prompt_kernel, grid=(BATCH, TILES),
        in_specs=[pl.BlockSpec((TILE, 1024), lambda b, t: (row(b, t), COL_U_S5)),
                  pl.BlockSpec((TILE, 1024), lambda b, t: (row(b, t), COL_Z_S5))] + mm_specs
                 + [lam_spec, _const_spec((3, 2, SUBLANES, S5_LANES)), lam_spec] + tail_specs,
        out_specs=[pl.BlockSpec((TILE, 1024), lambda b, t: (row(b, t), 0)),
                   pl.BlockSpec((1, 1, S5_LANES), lambda b, t: (b, 0, 0)),
                   pl.BlockSpec((1, 1, S5_LANES), lambda b, t: (b, 0, 0))],
        out_shape=[jax.ShapeDtypeStruct((N_PROMPT, 1024), BF16),
                   jax.ShapeDtypeStruct((BATCH, 1, S5_LANES), F32),
                   jax.ShapeDtypeStruct((BATCH, 1, S5_LANES), F32)],
        scratch_shapes=[pltpu.VMEM((TILE, 2 * S5_LANES), F32), pltpu.VMEM((2, SUBLANES, S5_LANES), F32)],
        compiler_params=_params(("parallel", "arbitrary")), name="s5_prompt",
    )(proj, proj, b_blk, c_blk, lam, ak, pw, *tail_args)

    proj3 = proj.reshape(N_TOK // DEC_BATCH, DEC_BATCH, N_IN)
    blk = lambda col: pl.BlockSpec((DEC_SEQ, S5_SB, 1024), lambda i: (SAMPLE_ROW0 // DEC_SEQ, i, col))
    st = pl.BlockSpec((S5_SB, S5_LANES), lambda i: (i, 0))
    ys, hr_s, hi_s = pl.pallas_call(
        _s5_sample_kernel, grid=(DEC_BATCH // S5_SB,),
        in_specs=[blk(COL_U_S5), blk(COL_Z_S5), st, st] + mm_specs + [lam_spec] + tail_specs,
        out_specs=[pl.BlockSpec((DEC_SEQ, S5_SB, 1024), lambda i: (0, i, 0)), st, st],
        out_shape=[jax.ShapeDtypeStruct((DEC_SEQ, DEC_BATCH, 1024), BF16),
                   jax.ShapeDtypeStruct((DEC_BATCH, S5_LANES), F32),
                   jax.ShapeDtypeStruct((DEC_BATCH, S5_LANES), F32)],
        scratch_shapes=[pltpu.VMEM((DEC_SEQ * S5_SB, 2 * S5_LANES), F32)],
        compiler_params=_params(("parallel",)), name="s5_sample",
    )(proj3, proj3, h0r, h0i, b_blk, c_blk, lam, *tail_args)
    return (yp, hr_p, hi_p), (ys.reshape(N_SAMPLE, 1024), hr_s, hi_s)


def _lru_gates(xc, wg_ref, ba_ref, bx_ref, lam_ref):
    xcb = xc.astype(BF16)
    ra, rx = [], []
    for n in range(LRU_BLOCKS):
        res = jnp.dot(xcb[:, n * LRU_BLOCK:(n + 1) * LRU_BLOCK], wg_ref[n], preferred_element_type=F32)
        ra.append(res[:, :LRU_BLOCK])
        rx.append(res[:, LRU_BLOCK:])
    r_gate = _sigmoid(jnp.concatenate(ra, axis=1) + ba_ref[...])
    i_gate = _sigmoid(jnp.concatenate(rx, axis=1) + bx_ref[...])
    nl = -lam_ref[...]
    softplus = jnp.maximum(nl, 0.0) + jnp.log1p(jnp.exp(-jnp.abs(nl)))
    a = jnp.exp(-LRU_C * r_gate * softplus)
    return a, jnp.sqrt(1.0 - a * a) * (i_gate * xc)


def _lru_prompt_kernel(u_ref, z_ref, cw_ref, cb_ref, wg_ref, ba_ref, bx_ref, lam_ref,
                       y_ref, hl_ref, a_scr, b_scr, tail, car):
    t = pl.program_id(1)
    nwrap = CONV_WIDTH - 1

    @pl.when(t == 0)
    def _():
        tail[...] = jnp.zeros_like(tail)
        car[...] = jnp.zeros_like(car)

    x = u_ref[...].astype(F32)
    rowi = lax.broadcasted_iota(jnp.int32, (SUBLANES, LRU_WIDTH), 0)
    wrap = []
    for k in range(nwrap):
        cur = x[TILE - (nwrap - k) * SUBLANES:TILE - (nwrap - k - 1) * SUBLANES, :]
        prev = tail[k * SUBLANES:(k + 1) * SUBLANES, :]
        wrap.append(jnp.where(rowi == 0, pltpu.roll(prev, 1, 0), pltpu.roll(cur, 1, 0)))
    xc = cw_ref[nwrap:nwrap + 1, :] * x + cb_ref[...]
    for s in range(1, CONV_WIDTH):
        shifted = jnp.concatenate(wrap[nwrap - s:] + [x[:TILE - s * SUBLANES, :]], axis=0)
        xc = xc + cw_ref[nwrap - s:nwrap - s + 1, :] * shifted
    tail[...] = x[TILE - nwrap * SUBLANES:, :]

    a, b = _lru_gates(xc, wg_ref, ba_ref, bx_ref, lam_ref)
    a_scr[...] = a
    b_scr[...] = b

    w = SCAN_W
    rw = lax.broadcasted_iota(jnp.int32, (SUBLANES, w), 0)
    for c in range(LRU_WIDTH // w):
        sl = slice(c * w, (c + 1) * w)

        def local(r, hc, sl=sl):
            row = pl.multiple_of(r * SUBLANES, SUBLANES)
            av = a_scr[pl.ds(row, SUBLANES), sl]
            h = av * hc[0] + b_scr[pl.ds(row, SUBLANES), sl]
            p = av * hc[1]
            b_scr[pl.ds(row, SUBLANES), sl] = h
            a_scr[pl.ds(row, SUBLANES), sl] = p
            return h, p

        bv, av = lax.fori_loop(0, SEG, local, (jnp.zeros((SUBLANES, w), F32), jnp.ones((SUBLANES, w), F32)))

        for k in (1, 2, 4):
            sa = jnp.where(rw >= k, pltpu.roll(av, k, 0), 1.0)
            sb = jnp.where(rw >= k, pltpu.roll(bv, k, 0), 0.0)
            bv = bv + av * sb
            av = av * sa
        cv = car[:, sl]
        full = bv + av * cv
        enter = jnp.where(rw == 0, cv, pltpu.roll(full, 1, 0))
        car[:, sl] = _bcast_row(full, SUBLANES - 1)

        def fix(r, _, sl=sl, enter=enter):
            row = pl.multiple_of(r * SUBLANES, SUBLANES)
            b_scr[pl.ds(row, SUBLANES), sl] = (b_scr[pl.ds(row, SUBLANES), sl]
                                               + a_scr[pl.ds(row, SUBLANES), sl] * enter)
            return 0

        lax.fori_loop(0, SEG, fix, 0)

    y_ref[...] = (b_scr[...] * _silu(z_ref[...].astype(F32))).astype(y_ref.dtype)

    @pl.when(t == pl.num_programs(1) - 1)
    def _():
        hl_ref[0] = car[0:1, :]


def _lru_sample_kernel(u_ref, z_ref, h0_ref, cbuf_ref, cw_ref, cb_ref, wg_ref, ba_ref, bx_ref, lam_ref,
                       y_ref, hl_ref, ext):
    nb = DEC_BATCH
    nwrap = CONV_WIDTH - 1
    ext[0:nwrap * nb, :] = cbuf_ref[...]
    ext[nwrap * nb:, :] = u_ref[...].astype(F32)
    h = h0_ref[...]
    for j in range(DEC_SEQ):
        xc = cb_ref[...]
        for k in range(CONV_WIDTH):
            xc = xc + cw_ref[k:k + 1, :] * ext[(j + k) * nb:(j + k + 1) * nb, :]
        a, b = _lru_gates(xc, wg_ref, ba_ref, bx_ref, lam_ref)
        h = a * h + b
        y_ref[j * nb:(j + 1) * nb, :] = (h * _silu(z_ref[j * nb:(j + 1) * nb, :].astype(F32))).astype(y_ref.dtype)
    hl_ref[...] = h


def _lru_calls(proj, conv_w, conv_b, wg, b_a, b_x, lam, h0, cbuf):
    w_specs = [_const_spec((CONV_WIDTH, LRU_WIDTH)), _const_spec((1, LRU_WIDTH)),
               _const_spec((LRU_BLOCKS, LRU_BLOCK, 2 * LRU_BLOCK)),
               _const_spec((1, LRU_WIDTH)), _const_spec((1, LRU_WIDTH)), _const_spec((1, LRU_WIDTH))]
    w_args = [conv_w, conv_b.reshape(1, -1), wg, b_a.reshape(1, -1), b_x.reshape(1, -1), lam.reshape(1, -1)]
    row = lambda b, t: b * TILES + t
    yp, hl_p = pl.pallas_call(
        _lru_prompt_kernel, grid=(BATCH, TILES),
        in_specs=[pl.BlockSpec((TILE, 1024), lambda b, t: (row(b, t), COL_U_LRU)),
                  pl.BlockSpec((TILE, 1024), lambda b, t: (row(b, t), COL_Z_LRU))] + w_specs,
        out_specs=[pl.BlockSpec((TILE, 1024), lambda b, t: (row(b, t), 0)),
                   pl.BlockSpec((1, 1, LRU_WIDTH), lambda b, t: (b, 0, 0))],
        out_shape=[jax.ShapeDtypeStruct((N_PROMPT, 1024), BF16),
                   jax.ShapeDtypeStruct((BATCH, 1, LRU_WIDTH), F32)],
        scratch_shapes=[pltpu.VMEM((TILE, LRU_WIDTH), F32), pltpu.VMEM((TILE, LRU_WIDTH), F32),
                        pltpu.VMEM(((CONV_WIDTH - 1) * SUBLANES, LRU_WIDTH), F32),
                        pltpu.VMEM((SUBLANES, LRU_WIDTH), F32)],
        compiler_params=_params(("parallel", "arbitrary")), name="lru_prompt",
    )(proj, proj, *w_args)

    srow = N_PROMPT // N_SAMPLE
    ys, hl_s = pl.pallas_call(
        _lru_sample_kernel, grid=(1,),
        in_specs=[pl.BlockSpec((N_SAMPLE, 1024), lambda i: (srow, COL_U_LRU)),
                  pl.BlockSpec((N_SAMPLE, 1024), lambda i: (srow, COL_Z_LRU)),
                  _const_spec((DEC_BATCH, LRU_WIDTH)),
                  _const_spec(((CONV_WIDTH - 1) * DEC_BATCH, LRU_WIDTH))] + w_specs,
        out_specs=[_const_spec((N_SAMPLE, 1024)), _const_spec((DEC_BATCH, LRU_WIDTH))],
        out_shape=[jax.ShapeDtypeStruct((N_SAMPLE, 1024), BF16),
                   jax.ShapeDtypeStruct((DEC_BATCH, LRU_WIDTH), F32)],
        scratch_shapes=[pltpu.VMEM(((CONV_WIDTH - 1) * DEC_BATCH + N_SAMPLE, LRU_WIDTH), F32)],
        compiler_params=_params(("arbitrary",)), name="lru_sample",
    )(proj, proj, h0, cbuf, *w_args)
    return (yp, hl_p), (ys, hl_s)


RET_SB = 16
RET_SROWS = RET_SB * DEC_SEQ


def _ret_tables(seq, idx):
    n = idx.shape[0]
    chunk = jnp.max(idx) + 1.0
    log_g = jnp.log1p(-jnp.exp2(-5.0 - jnp.arange(RET_HEADS, dtype=F32)))
    diff = idx[:, None] - idx[None, :]
    same = seq[:, None] == seq[None, :]
    dmask = jnp.where((diff[None] >= 0) & same[None],
                      jnp.exp(jnp.maximum(diff, 0.0)[None] * log_g[:, None, None]), 0.0)
    xi = jnp.exp((idx[None, :] + 1.0) * log_g[:, None])
    zeta = jnp.exp((chunk - 1.0 - idx[None, :]) * log_g[:, None])
    gch = jnp.exp(chunk * log_g)
    full = lambda t: jnp.broadcast_to(t[:, :, None], (RET_HEADS, n, LANES))
    gc = jnp.broadcast_to(gch[:, None, None], (RET_HEADS, SUBLANES, LANES))
    return dmask, full(xi), full(zeta), gc


def _rope_tables(pos):
    half = RET_DK // 2
    freq = ROPE_BASE ** (-jnp.arange(half, dtype=F32) / half)
    ang = pos[:, None] * freq[None, :]
    cos, sin = jnp.cos(ang), jnp.sin(ang)
    return jnp.concatenate([cos, cos], axis=-1), jnp.concatenate([-sin, sin], axis=-1)


def _rope(x, cosf, sinf):
    return x * cosf + pltpu.roll(x, RET_DK // 2, 1) * sinf


def _ret_head(h, q, k, v, cos, sin, dmask_ref, zeta_ref):
    sl = slice(h * RET_DK, (h + 1) * RET_DK)
    qb = _rope(q[:, sl].astype(F32), cos, sin).astype(BF16)
    kh = _rope(k[:, sl].astype(F32), cos, sin) * (RET_DK ** -0.5)
    vb = v[:, sl]
    sc = lax.dot_general(qb, kh.astype(BF16), (((1,), (1,)), ((), ())), preferred_element_type=F32)
    inner = jnp.dot((sc * dmask_ref[h]).astype(BF16), vb, preferred_element_type=F32)
    kz = (kh * zeta_ref[h]).astype(BF16)
    return sl, qb, kz, vb, inner


def _ret_finish(o, z, g):
    mu = jnp.mean(o, axis=-1, keepdims=True)
    oc = o - mu
    var = jnp.mean(oc * oc, axis=-1, keepdims=True)
    on = oc * lax.rsqrt(var + GN_EPS) * g
    return (on * _silu(z.astype(F32))).astype(BF16)


def _ret_prompt_kernel(q_ref, k_ref, v_ref, z_ref, cos_ref, sin_ref, dmask_ref, xi_ref, zeta_ref,
                       gc_ref, gng_ref, y_ref, r_ref):
    @pl.when(pl.program_id(1) == 0)
    def _():
        r_ref[...] = jnp.zeros_like(r_ref)

    cos, sin = cos_ref[...], sin_ref[...]
    for h in range(RET_HEADS):
        sl, qb, kz, vb, inner = _ret_head(h, q_ref, k_ref, v_ref, cos, sin, dmask_ref, zeta_ref)
        r = r_ref[0, h]
        cross = jnp.dot(qb, r.astype(BF16), preferred_element_type=F32) * xi_ref[h]
        upd = lax.dot_general(kz, vb, (((0,), (0,)), ((), ())), preferred_element_type=F32)
        r_ref[0, h] = r * gc_ref[h, 0:1, :] + upd
        y_ref[:, sl] = _ret_finish(inner + cross, z_ref[:, sl], gng_ref[:, sl])


def _ret_sample_kernel(q_ref, k_ref, v_ref, z_ref, r0_ref, cos_ref, sin_ref, dmask_ref, xi_ref,
                       zeta_ref, gc_ref, gng_ref, y_ref, r_ref):
    cos, sin = cos_ref[...], sin_ref[...]
    q = q_ref[...].reshape(RET_SROWS, RET_WIDTH)
    k = k_ref[...].reshape(RET_SROWS, RET_WIDTH)
    v = v_ref[...].reshape(RET_SROWS, RET_WIDTH)
    z = z_ref[...].reshape(RET_SROWS, RET_WIDTH)
    rowseq = lax.broadcasted_iota(jnp.int32, (RET_SROWS, RET_DV), 0) % RET_SB
    outs = []
    for h in range(RET_HEADS):
        sl, qb, kz, vb, inner = _ret_head(h, q, k, v, cos, sin, dmask_ref, zeta_ref)
        rcat = jnp.concatenate([r0_ref[s, h] for s in range(RET_SB)], axis=1).astype(BF16)
        call = jnp.dot(qb, rcat, preferred_element_type=F32)
        cross = jnp.zeros((RET_SROWS, RET_DV), F32)
        for s in range(RET_SB):
            cross = jnp.where(rowseq == s, call[:, s * RET_DV:(s + 1) * RET_DV], cross)
        cross = cross * xi_ref[h]
        vf = vb.astype(F32)
        vexp = jnp.concatenate([jnp.where(rowseq == s, vf, 0.0) for s in range(RET_SB)],
                               axis=1).astype(BF16)
        upd = lax.dot_general(kz, vexp, (((0,), (0,)), ((), ())), preferred_element_type=F32)
        gc = gc_ref[h, 0:1, :]
        for s in range(RET_SB):
            r_ref[s, h] = r0_ref[s, h] * gc + upd[:, s * RET_DV:(s + 1) * RET_DV]
        outs.append(_ret_finish(inner + cross, z[:, sl], gng_ref[:, sl]))
    y_ref[...] = jnp.concatenate(outs, axis=1).reshape(DEC_SEQ, RET_SB, RET_WIDTH)


def _ret_calls(proj, gn_g, r0, layer):
    def t_specs(n):
        return [_const_spec((RET_HEADS, n, n)), _const_spec((RET_HEADS, n, LANES)),
                _const_spec((RET_HEADS, n, LANES)), _const_spec((RET_HEADS, SUBLANES, LANES)),
                _const_spec((1, RET_WIDTH))]

    rows = jnp.arange(TILE)
    tok = ((rows % SUBLANES) * SEG + rows // SUBLANES).astype(F32)
    tabs = _ret_tables(jnp.zeros((TILE,), jnp.int32), tok)
    pos = (jnp.arange(TILES, dtype=F32)[:, None] * TILE + tok[None, :]).reshape(SEQ) + 0.0
    cosf, sinf = _rope_tables(pos)
    row = lambda b, c: b * TILES + c
    col = lambda j: (lambda b, c: (row(b, c), j))
    yp, r_p = pl.pallas_call(
        _ret_prompt_kernel, grid=(BATCH, TILES),
        in_specs=[pl.BlockSpec((TILE, 1024), col(COL_Q)), pl.BlockSpec((TILE, 1024), col(COL_K)),
                  pl.BlockSpec((TILE, 1024), col(COL_V)), pl.BlockSpec((TILE, 1024), col(COL_Z_RET)),
                  pl.BlockSpec((TILE, LANES), lambda b, c: (c, 0)),
                  pl.BlockSpec((TILE, LANES), lambda b, c: (c, 0))] + t_specs(TILE),
        out_specs=[pl.BlockSpec((TILE, 1024), lambda b, c: (row(b, c), 0)),
                   pl.BlockSpec((1, RET_HEADS, RET_DK, RET_DV), lambda b, c: (b, 0, 0, 0))],
        out_shape=[jax.ShapeDtypeStruct((N_PROMPT, 1024), BF16),
                   jax.ShapeDtypeStruct((BATCH, RET_HEADS, RET_DK, RET_DV), F32)],
        compiler_params=_params(("parallel", "arbitrary")), name="ret_prompt",
    )(proj, proj, proj, proj, cosf, sinf, *tabs, gn_g.reshape(1, -1))

    rows = jnp.arange(RET_SROWS)
    tabs = _ret_tables(rows % RET_SB, (rows // RET_SB).astype(F32))
    cosf, sinf = _rope_tables((rows // RET_SB).astype(F32) + float(PAST_LEN))
    proj3 = proj.reshape(N_TOK // DEC_BATCH, DEC_BATCH, N_IN)
    blk = lambda c: pl.BlockSpec((DEC_SEQ, RET_SB, 1024), lambda i: (SAMPLE_ROW0 // DEC_SEQ, i, c))
    st_block = (None, RET_SB, RET_HEADS, RET_DK, RET_DV)
    ys, r_s = pl.pallas_call(
        _ret_sample_kernel, grid=(DEC_BATCH // RET_SB,),
        in_specs=[blk(COL_Q), blk(COL_K), blk(COL_V), blk(COL_Z_RET),
                  pl.BlockSpec(st_block, lambda i: (layer, i, 0, 0, 0)),
                  _const_spec((RET_SROWS, LANES)), _const_spec((RET_SROWS, LANES))] + t_specs(RET_SROWS),
        out_specs=[pl.BlockSpec((DEC_SEQ, RET_SB, 1024), lambda i: (0, i, 0)),
                   pl.BlockSpec(st_block[1:], lambda i: (i, 0, 0, 0))],
        out_shape=[jax.ShapeDtypeStruct((DEC_SEQ, DEC_BATCH, 1024), BF16),
                   jax.ShapeDtypeStruct((DEC_BATCH, RET_HEADS, RET_DK, RET_DV), F32)],
        compiler_params=_params(("parallel",)), name="ret_sample",
    )(proj3, proj3, proj3, proj3, r0, cosf, sinf, *tabs, gn_g.reshape(1, -1))
    return (yp, r_p), (ys.reshape(N_SAMPLE, 1024), r_s)


def _merge_kernel(na, ysp, yss, ylp, yls, yrp, yrs, gs_ref, gl_ref, gr_ref, wb_ref, o_ref):
    acc = None
    for m, (a_ref, b_ref, g_ref) in enumerate(((ysp, yss, gs_ref), (ylp, yls, gl_ref), (yrp, yrs, gr_ref))):
        b = jnp.dot(_pick(na, a_ref, b_ref), wb_ref[m], preferred_element_type=F32)
        term = _sigmoid(g_ref[...].astype(F32)) * b
        acc = term if acc is None else acc + term
    o_ref[...] = acc.astype(o_ref.dtype)


def _merge_call(ys, yl, yr, proj, wb, tm=512):
    specs, na = _two_part_specs(tm, 1024)
    gsp = lambda j: pl.BlockSpec((tm, D_MODEL), lambda i: (i, COL_GATES + j))
    return pl.pallas_call(
        functools.partial(_merge_kernel, na), grid=(N_TOK // tm,),
        in_specs=specs * 3 + [gsp(0), gsp(1), gsp(2), _const_spec((3, 1024, D_MODEL))],
        out_specs=pl.BlockSpec((tm, D_MODEL), lambda i: (i, 0)),
        out_shape=jax.ShapeDtypeStruct((N_TOK, D_MODEL), BF16),
        compiler_params=_params(("parallel",)), name="merge",
    )(*ys, *yl, *yr, proj, proj, proj, wb)


def _outproj_kernel(m_ref, w_ref, x_ref, g_ref, xo_ref, xn_ref):
    x = x_ref[...] + jnp.dot(m_ref[...], w_ref[...], preferred_element_type=F32)
    xo_ref[...] = x
    ms = jnp.mean(x * x, axis=-1, keepdims=True)
    xn_ref[...] = (x * lax.rsqrt(ms + NORM_EPS) * g_ref[...]).astype(xn_ref.dtype)


def _outproj_call(merged, w_out, x, g_next, norm_dtype, tm=512):
    tok = pl.BlockSpec((tm, D_MODEL), lambda i: (i, 0))
    return pl.pallas_call(
        _outproj_kernel, grid=(N_TOK // tm,),
        in_specs=[tok, _const_spec((D_MODEL, D_MODEL)), tok, _const_spec((1, D_MODEL))],
        out_specs=[tok, tok],
        out_shape=[jax.ShapeDtypeStruct((N_TOK, D_MODEL), F32), jax.ShapeDtypeStruct((N_TOK, D_MODEL), norm_dtype)],
        compiler_params=_params(("parallel",)), name="outproj",
    )(merged, w_out, x, g_next.reshape(1, D_MODEL))


def kernel(x_prompt, x_sample, state_s5_re, state_s5_im, state_lru, state_conv, state_ret, norm_g, w_in, s5_lambda_re, s5_lambda_im, s5_log_dt, s5_b_re, s5_b_im, s5_c_re, s5_c_im, s5_d, s5_w_glu, s5_b_glu, lru_conv_w, lru_conv_b, lru_w_a, lru_b_a, lru_w_x, lru_b_x, lru_lambda, ret_gn_g, w_branch_s5, w_branch_lru, w_branch_ret, w_out, final_norm_g):
    x, xn = _rmsnorm_call(*_to_rows(x_prompt, x_sample), norm_g[0])
    outs_p = [[] for _ in range(5)]
    outs_s = [[] for _ in range(5)]
    last = jnp.arange(SEQ - (CONV_WIDTH - 1), SEQ)
    last_rows = ((last // TILE) * TILE + (last % SEG) * SUBLANES + (last % TILE) // SEG)
    last_rows = (jnp.arange(BATCH)[:, None] * SEQ + last_rows[None, :]).reshape(-1)
    for l in range(DEPTH):
        proj = _inproj_call(xn, w_in, l)

        prep = _s5_prep(s5_lambda_re[l], s5_lambda_im[l], s5_log_dt[l], s5_b_re[l], s5_b_im[l],
                        s5_c_re[l], s5_c_im[l])
        (ys_p, hr_p, hi_p), (ys_s, hr_s, hi_s) = _s5_calls(
            proj, prep, s5_d[l], s5_w_glu[l].astype(BF16), s5_b_glu[l],
            state_s5_re[l].reshape(DEC_BATCH, S5_LANES), state_s5_im[l].reshape(DEC_BATCH, S5_LANES))

        wg = jnp.concatenate([lru_w_a[l], lru_w_x[l]], axis=-1).astype(BF16)
        cbuf = state_conv[l].transpose(1, 0, 2).reshape((CONV_WIDTH - 1) * DEC_BATCH, LRU_WIDTH)
        (yl_p, hl_p), (yl_s, hl_s) = _lru_calls(proj, lru_conv_w[l], lru_conv_b[l], wg, lru_b_a[l], lru_b_x[l],
                                                lru_lambda[l], state_lru[l], cbuf)

        (yr_p, r_p), (yr_s, r_s) = _ret_calls(proj, ret_gn_g[l], state_ret, l)

        wb = jnp.stack([w_branch_s5[l], w_branch_lru[l], w_branch_ret[l]]).astype(BF16)
        merged = _merge_call((ys_p, ys_s), (yl_p, yl_s), (yr_p, yr_s), proj, wb)
        final = l == DEPTH - 1
        g_next = final_norm_g if final else norm_g[l + 1]
        x, xn = _outproj_call(merged, w_out[l].astype(BF16), x, g_next, F32 if final else BF16)

        u_lru = proj[:, COL_U_LRU * 1024:(COL_U_LRU + 1) * 1024]
        conv_p = u_lru[last_rows].reshape(BATCH, CONV_WIDTH - 1, LRU_WIDTH).astype(F32)
        conv_s = u_lru[N_TOK - (CONV_WIDTH - 1) * DEC_BATCH:].reshape(CONV_WIDTH - 1, DEC_BATCH, LRU_WIDTH)
        conv_s = conv_s.transpose(1, 0, 2).astype(F32)
        st = (S5_GROUPS, S5_STATE)
        for lst, vals in ((outs_p, (hr_p.reshape(BATCH, *st), hi_p.reshape(BATCH, *st),
                                    hl_p.reshape(BATCH, LRU_WIDTH), conv_p, r_p)),
                          (outs_s, (hr_s.reshape(DEC_BATCH, *st), hi_s.reshape(DEC_BATCH, *st),
                                    hl_s, conv_s, r_s))):
            for j in range(5):
                lst[j].append(vals[j])

    y_prompt, y_sample = _from_rows(xn)
    sp = [jnp.stack(t, axis=0) for t in outs_p]
    ss = [jnp.stack(t, axis=0) for t in outs_s]
    return (y_prompt, y_sample, *sp, *ss)
```

```python
import functools

import jax
import jax.numpy as jnp
from jax import lax
from jax.experimental import pallas as pl
from jax.experimental.pallas import tpu as pltpu

F32 = jnp.float32
BF16 = jnp.bfloat16

D_MODEL = 2048
BATCH = 4
SEQ = 2048
DEPTH = 2
DEC_BATCH = 128
DEC_SEQ = 8
PAST_LEN = 16384
S5_WIDTH = 1024
S5_GROUP = 16
S5_GROUPS = 64
S5_STATE = 64
S5_LANES = S5_GROUPS * S5_STATE
LRU_WIDTH = 1024
LRU_BLOCKS = 8
LRU_BLOCK = 128
CONV_WIDTH = 4
LRU_C = 8.0
RET_HEADS = 8
RET_DK = 128
RET_DV = 128
RET_WIDTH = 1024
ROPE_BASE = 10000.0
NORM_EPS = 1e-6
GN_EPS = 1e-5
N_IN = 14336

N_PROMPT = BATCH * SEQ
N_SAMPLE = DEC_BATCH * DEC_SEQ
N_TOK = N_PROMPT + N_SAMPLE

SUBLANES = 8
LANES = 128
VMEM_LIMIT = 56 * 1024 * 1024

TILE = 256
SEG = TILE // SUBLANES
TILES = SEQ // TILE

COL_U_S5, COL_Z_S5, COL_U_LRU, COL_Z_LRU, COL_Q, COL_K, COL_V, COL_Z_RET = range(8)
COL_GATES = 4

S5_KB = 4
S5_KW = S5_WIDTH // S5_KB
S5_NW = S5_LANES // S5_KB
SCAN_W = 512

SAMPLE_ROW0 = N_PROMPT // DEC_BATCH


def _params(sem, vmem=VMEM_LIMIT):
    return pltpu.CompilerParams(dimension_semantics=sem, vmem_limit_bytes=vmem)


def _const_spec(shape):
    return pl.BlockSpec(shape, lambda *_: (0,) * len(shape))


def _sigmoid(x):
    return jax.nn.sigmoid(x)


def _silu(x):
    return x * jax.nn.sigmoid(x)


def _bcast_row(x, row):
    return jnp.broadcast_to(x[row:row + 1, :], x.shape)


def _cmul(ar, ai, br, bi):
    return ar * br - ai * bi, ar * bi + ai * br


def _to_rows(x_prompt, x_sample):
    xp = x_prompt.reshape(BATCH, TILES, SUBLANES, SEG, -1).transpose(0, 1, 3, 2, 4).reshape(N_PROMPT, -1)
    xs = x_sample.transpose(1, 0, 2).reshape(N_SAMPLE, -1)
    return xp, xs


def _from_rows(yp, ys):
    yp = yp.reshape(BATCH, TILES, SEG, SUBLANES, -1).transpose(0, 1, 3, 2, 4).reshape(BATCH, SEQ, -1)
    ys = ys.reshape(DEC_SEQ, DEC_BATCH, -1).transpose(1, 0, 2)
    return yp, ys


def _two_part_specs(tm, width):
    na = N_PROMPT // tm
    return [pl.BlockSpec((tm, width), lambda i: (jnp.minimum(i, na - 1), 0)),
            pl.BlockSpec((tm, width), lambda i: (jnp.maximum(i - na, 0), 0))], na


def _pick(na, a_ref, b_ref):
    return jnp.where(pl.program_id(0) < na, a_ref[...], b_ref[...])


def _norm_kernel(na, xa_ref, xb_ref, g_ref, x_ref, o_ref):
    x = _pick(na, xa_ref, xb_ref)
    x_ref[...] = x
    ms = jnp.mean(x * x, axis=-1, keepdims=True)
    o_ref[...] = (x * lax.rsqrt(ms + NORM_EPS) * g_ref[...]).astype(o_ref.dtype)


def _rmsnorm_call(xp, xs, g, tm=512):
    specs, na = _two_part_specs(tm, D_MODEL)
    tok = pl.BlockSpec((tm, D_MODEL), lambda i: (i, 0))
    return pl.pallas_call(
        functools.partial(_norm_kernel, na),
        grid=(N_TOK // tm,),
        in_specs=specs + [_const_spec((1, D_MODEL))],
        out_specs=[tok, tok],
        out_shape=[jax.ShapeDtypeStruct((N_TOK, D_MODEL), F32), jax.ShapeDtypeStruct((N_TOK, D_MODEL), BF16)],
        compiler_params=_params(("parallel",)),
        name="rmsnorm",
    )(xp, xs, g.reshape(1, D_MODEL))


def _inproj_kernel(xn_ref, w_ref, o_ref, wbf_ref):
    @pl.when(pl.program_id(1) == 0)
    def _():
        wbf_ref[...] = w_ref[...].astype(BF16)

    o_ref[...] = jnp.dot(xn_ref[...], wbf_ref[...], preferred_element_type=F32).astype(o_ref.dtype)


def _inproj_call(xn, w_in, layer, tm=1024, tn=1024):
    n = xn.shape[0]
    return pl.pallas_call(
        _inproj_kernel,
        grid=(N_IN // tn, n // tm),
        in_specs=[pl.BlockSpec((tm, D_MODEL), lambda j, i: (i, 0)),
                  pl.BlockSpec((None, D_MODEL, tn), lambda j, i: (layer, 0, j))],
        out_specs=pl.BlockSpec((tm, tn), lambda j, i: (i, j)),
        out_shape=jax.ShapeDtypeStruct((n, N_IN), BF16),
        scratch_shapes=[pltpu.VMEM((D_MODEL, tn), BF16)],
        compiler_params=_params(("parallel", "arbitrary")),
        name="inproj",
    )(xn, w_in)


def _s5_prep(lam_re, lam_im, log_dt, b_re, b_im, c_re, c_im):
    dt = jnp.exp(log_dt)[:, None]
    e = jnp.exp(lam_re * dt)
    lbr = e * jnp.cos(lam_im * dt)
    lbi = e * jnp.sin(lam_im * dt)
    nr, ni = lbr - 1.0, lbi
    den = lam_re * lam_re + lam_im * lam_im
    cr = (nr * lam_re + ni * lam_im) / den
    ci = (ni * lam_re - nr * lam_im) / den
    bbr = cr[..., None] * b_re - ci[..., None] * b_im
    bbi = cr[..., None] * b_im + ci[..., None] * b_re
    gpb = S5_GROUPS // S5_KB

    def bblk(bb):
        t = bb.astype(BF16).reshape(S5_KB, gpb, S5_STATE, S5_GROUP).transpose(0, 1, 3, 2)
        t = jnp.tile(t.reshape(S5_KB, S5_KW, S5_STATE), (1, 1, gpb))
        same = (jnp.arange(S5_KW)[:, None] // S5_GROUP) == (jnp.arange(S5_NW)[None, :] // S5_STATE)
        return jnp.where(same[None], t, 0)

    def cblk(cc):
        t = cc.astype(BF16).reshape(S5_KB, gpb, S5_GROUP, S5_STATE).transpose(0, 1, 3, 2)
        t = jnp.tile(t.reshape(S5_KB, S5_NW, S5_GROUP), (1, 1, gpb))
        same = (jnp.arange(S5_NW)[:, None] // S5_STATE) == (jnp.arange(S5_KW)[None, :] // S5_GROUP)
        return jnp.where(same[None], t, 0)

    b_blk = jnp.concatenate([bblk(bbr), bblk(bbi)], axis=2)
    c_blk = jnp.concatenate([cblk(c_re), cblk(-c_im)], axis=1)

    lr, li = lbr.reshape(-1), lbi.reshape(-1)
    full = lambda v: jnp.broadcast_to(v, (SUBLANES, S5_LANES))
    lam = jnp.stack([full(lr), full(li)])
    sr, si = lr, li
    for _ in range(SEG.bit_length() - 1):
        sr, si = _cmul(sr, si, sr, si)
    pr, pi = [sr], [si]
    for _ in range(SUBLANES - 1):
        r_, i_ = _cmul(pr[-1], pi[-1], sr, si)
        pr.append(r_)
        pi.append(i_)
    row = jnp.arange(SUBLANES)[:, None]
    ak = jnp.stack([jnp.stack([jnp.where(row >= k, pr[k - 1][None, :], 0.0),
                               jnp.where(row >= k, pi[k - 1][None, :], 0.0)]) for k in (1, 2, 4)])
    pw = jnp.stack([jnp.stack(pr), jnp.stack(pi)])
    return b_blk, c_blk, lam, ak, pw


def _s5_drive_block(kb, u, bblk_ref, scr):
    res = jnp.dot(u[:, kb * S5_KW:(kb + 1) * S5_KW], bblk_ref[kb], preferred_element_type=F32)
    scr[:, kb * S5_NW:(kb + 1) * S5_NW] = res[:, :S5_NW]
    scr[:, S5_LANES + kb * S5_NW:S5_LANES + (kb + 1) * S5_NW] = res[:, S5_NW:]


def _s5_readout_block(kb, scr, cblk_ref):
    hcat = jnp.concatenate(
        [scr[:, kb * S5_NW:(kb + 1) * S5_NW],
         scr[:, S5_LANES + kb * S5_NW:S5_LANES + (kb + 1) * S5_NW]], axis=1).astype(BF16)
    return jnp.dot(hcat, cblk_ref[kb], preferred_element_type=F32)


def _s5_finish(parts, u, z, d_ref, wglu_ref, bglu_ref):
    y = jnp.concatenate(parts, axis=1) + d_ref[...] * u.astype(F32)
    y = jax.nn.gelu(y, approximate=True)
    glu = jnp.dot(y.astype(BF16), wglu_ref[...], preferred_element_type=F32) + bglu_ref[...]
    y = y * _sigmoid(glu)
    return (y * _silu(z.astype(F32))).astype(BF16)


def _s5_prompt_kernel(u_ref, z_ref, bblk_ref, cblk_ref, lam_ref, ak_ref, pw_ref, d_ref, wglu_ref, bglu_ref,
                      y_ref, hr_ref, hi_ref, scr, car):
    t = pl.program_id(1)

    @pl.when(t == 0)
    def _():
        car[...] = jnp.zeros_like(car)

    w = SCAN_W
    rowi = lax.broadcasted_iota(jnp.int32, (SUBLANES, w), 0)
    u = u_ref[...]
    parts = []
    for kb in range(S5_KB):
        _s5_drive_block(kb, u, bblk_ref, scr)
        for c in range(kb * (S5_NW // w), (kb + 1) * (S5_NW // w)):
            sl_re = slice(c * w, (c + 1) * w)
            sl_im = slice(S5_LANES + c * w, S5_LANES + (c + 1) * w)
            lr, li = lam_ref[0, :, sl_re], lam_ref[1, :, sl_re]

            xr = xi = jnp.zeros((SUBLANES, w), F32)
            for r in range(SEG):
                rows = slice(r * SUBLANES, (r + 1) * SUBLANES)
                pr, pi = _cmul(lr, li, xr, xi)
                xr = scr[rows, sl_re] + pr
                xi = scr[rows, sl_im] + pi
                scr[rows, sl_re] = xr
                scr[rows, sl_im] = xi

            for k, idx in zip((1, 2, 4), range(3)):
                pr, pi = _cmul(ak_ref[idx, 0, :, sl_re], ak_ref[idx, 1, :, sl_re],
                               pltpu.roll(xr, k, 0), pltpu.roll(xi, k, 0))
                xr, xi = xr + pr, xi + pi
            cr, ci = car[0, :, sl_re], car[1, :, sl_re]
            pr, pi = _cmul(pw_ref[0, :, sl_re], pw_ref[1, :, sl_re], cr, ci)
            fr, fi = xr + pr, xi + pi
            dr = jnp.where(rowi == 0, cr, pltpu.roll(fr, 1, 0))
            di = jnp.where(rowi == 0, ci, pltpu.roll(fi, 1, 0))
            car[0, :, sl_re] = _bcast_row(fr, SUBLANES - 1)
            car[1, :, sl_re] = _bcast_row(fi, SUBLANES - 1)

            for r in range(SEG):
                rows = slice(r * SUBLANES, (r + 1) * SUBLANES)
                dr, di = _cmul(lr, li, dr, di)
                scr[rows, sl_re] = scr[rows, sl_re] + dr
                scr[rows, sl_im] = scr[rows, sl_im] + di
        parts.append(_s5_readout_block(kb, scr, cblk_ref))

    y_ref[...] = _s5_finish(parts, u, z_ref[...], d_ref, wglu_ref, bglu_ref)

    @pl.when(t == pl.num_programs(1) - 1)
    def _():
        hr_ref[0] = car[0, 0:1, :]
        hi_ref[0] = car[1, 0:1, :]


S5_SB = 32


def _s5_sample_kernel(u_ref, z_ref, h0r_ref, h0i_ref, bblk_ref, cblk_ref, lam_ref, d_ref, wglu_ref, bglu_ref,
                      y_ref, hr_ref, hi_ref, scr):
    rows = DEC_SEQ * S5_SB
    u = u_ref[...].reshape(rows, S5_WIDTH)
    for kb in range(S5_KB):
        _s5_drive_block(kb, u, bblk_ref, scr)

    w = SCAN_W
    for c in range(S5_LANES // w):
        sl_re = slice(c * w, (c + 1) * w)
        sl_im = slice(S5_LANES + c * w, S5_LANES + (c + 1) * w)
        lr, li = lam_ref[0, :, sl_re], lam_ref[1, :, sl_re]

        def body(g, _, sl_re=sl_re, sl_im=sl_im, lr=lr, li=li):
            row = pl.multiple_of(g * SUBLANES, SUBLANES)
            hr = h0r_ref[pl.ds(row, SUBLANES), sl_re]
            hi = h0i_ref[pl.ds(row, SUBLANES), sl_re]
            for j in range(DEC_SEQ):
                pr, pi = _cmul(lr, li, hr, hi)
                hr = scr[pl.ds(j * S5_SB + row, SUBLANES), sl_re] + pr
                hi = scr[pl.ds(j * S5_SB + row, SUBLANES), sl_im] + pi
                scr[pl.ds(j * S5_SB + row, SUBLANES), sl_re] = hr
                scr[pl.ds(j * S5_SB + row, SUBLANES), sl_im] = hi
            hr_ref[pl.ds(row, SUBLANES), sl_re] = hr
            hi_ref[pl.ds(row, SUBLANES), sl_re] = hi
            return 0

        lax.fori_loop(0, S5_SB // SUBLANES, body, 0)

    parts = [_s5_readout_block(kb, scr, cblk_ref) for kb in range(S5_KB)]
    y = _s5_finish(parts, u, z_ref[...].reshape(rows, S5_WIDTH), d_ref, wglu_ref, bglu_ref)
    y_ref[...] = y.reshape(DEC_SEQ, S5_SB, S5_WIDTH)


def _s5_calls(proj, prep, d, wglu, bglu, h0r, h0i):
    b_blk, c_blk, lam, ak, pw = prep
    lam_spec = _const_spec((2, SUBLANES, S5_LANES))
    tail_specs = [_const_spec((1, S5_WIDTH)), _const_spec((S5_WIDTH, S5_WIDTH)), _const_spec((1, S5_WIDTH))]
    tail_args = [d.reshape(1, -1), wglu, bglu.reshape(1, -1)]
    mm_specs = [_const_spec((S5_KB, S5_KW, 2 * S5_NW)), _const_spec((S5_KB, 2 * S5_NW, S5_KW))]

    row = lambda b, t: b * TILES + t
    yp, hr_p, hi_p = pl.pallas_call(
        _s5_prompt_kernel, grid=(BATCH, TILES),
        in_specs=[pl.BlockSpec((TILE, 1024), lambda b, t: (row(b, t), COL_U_S5)),
                  pl.BlockSpec((TILE, 1024), lambda b, t: (row(b, t), COL_Z_S5))] + mm_specs
                 + [lam_spec, _const_spec((3, 2, SUBLANES, S5_LANES)), lam_spec] + tail_specs,
        out_specs=[pl.BlockSpec((TILE, 1024), lambda b, t: (row(b, t), 0)),
                   pl.BlockSpec((1, 1, S5_LANES), lambda b, t: (b, 0, 0)),
                   pl.BlockSpec((1, 1, S5_LANES), lambda b, t: (b, 0, 0))],
        out_shape=[jax.ShapeDtypeStruct((N_PROMPT, 1024), BF16),
                   jax.ShapeDtypeStruct((BATCH, 1, S5_LANES), F32),
                   jax.ShapeDtypeStruct((BATCH, 1, S5_LANES), F32)],
        scratch_shapes=[pltpu.VMEM((TILE, 2 * S5_LANES), F32), pltpu.VMEM((2, SUBLANES, S5_LANES), F32)],
        compiler_params=_params(("parallel", "arbitrary")), name="s5_prompt",
    )(proj, proj, b_blk, c_blk, lam, ak, pw, *tail_args)

    proj3 = proj.reshape(N_TOK // DEC_BATCH, DEC_BATCH, N_IN)
    blk = lambda col: pl.BlockSpec((DEC_SEQ, S5_SB, 1024), lambda i: (SAMPLE_ROW0 // DEC_SEQ, i, col))
    st = pl.BlockSpec((S5_SB, S5_LANES), lambda i: (i, 0))
    ys, hr_s, hi_s = pl.pallas_call(
        _s5_sample_kernel, grid=(DEC_BATCH // S5_SB,),
        in_specs=[blk(COL_U_S5), blk(COL_Z_S5), st, st] + mm_specs + [lam_spec] + tail_specs,
        out_specs=[pl.BlockSpec((DEC_SEQ, S5_SB, 1024), lambda i: (0, i, 0)), st, st],
        out_shape=[jax.ShapeDtypeStruct((DEC_SEQ, DEC_BATCH, 1024), BF16),
                   jax.ShapeDtypeStruct((DEC_BATCH, S5_LANES), F32),
                   jax.ShapeDtypeStruct((DEC_BATCH, S5_LANES), F32)],
        scratch_shapes=[pltpu.VMEM((DEC_SEQ * S5_SB, 2 * S5_LANES), F32)],
        compiler_params=_params(("parallel",)), name="s5_sample",
    )(proj3, proj3, h0r, h0i, b_blk, c_blk, lam, *tail_args)
    return (yp, hr_p, hi_p), (ys.reshape(N_SAMPLE, 1024), hr_s, hi_s)


def _lru_gates(xc, wg_ref, ba_ref, bx_ref, lam_ref):
    xcb = xc.astype(BF16)
    ra, rx = [], []
    for n in range(LRU_BLOCKS):
        res = jnp.dot(xcb[:, n * LRU_BLOCK:(n + 1) * LRU_BLOCK], wg_ref[n], preferred_element_type=F32)
        ra.append(res[:, :LRU_BLOCK])
        rx.append(res[:, LRU_BLOCK:])
    r_gate = _sigmoid(jnp.concatenate(ra, axis=1) + ba_ref[...])
    i_gate = _sigmoid(jnp.concatenate(rx, axis=1) + bx_ref[...])
    nl = -lam_ref[...]
    softplus = jnp.maximum(nl, 0.0) + jnp.log1p(jnp.exp(-jnp.abs(nl)))
    a = jnp.exp(-LRU_C * r_gate * softplus)
    return a, jnp.sqrt(1.0 - a * a) * (i_gate * xc)


def _lru_prompt_kernel(u_ref, z_ref, cw_ref, cb_ref, wg_ref, ba_ref, bx_ref, lam_ref,
                       y_ref, hl_ref, cv_ref, a_scr, b_scr, tail, car):
    t = pl.program_id(1)
    nwrap = CONV_WIDTH - 1

    @pl.when(t == 0)
    def _():
        tail[...] = jnp.zeros_like(tail)
        car[...] = jnp.zeros_like(car)

    x = u_ref[...].astype(F32)
    rowi = lax.broadcasted_iota(jnp.int32, (SUBLANES, LRU_WIDTH), 0)
    wrap = []
    for k in range(nwrap):
        cur = x[TILE - (nwrap - k) * SUBLANES:TILE - (nwrap - k - 1) * SUBLANES, :]
        prev = tail[k * SUBLANES:(k + 1) * SUBLANES, :]
        wrap.append(jnp.where(rowi == 0, pltpu.roll(prev, 1, 0), pltpu.roll(cur, 1, 0)))
    xc = cw_ref[nwrap:nwrap + 1, :] * x + cb_ref[...]
    for s in range(1, CONV_WIDTH):
        shifted = jnp.concatenate(wrap[nwrap - s:] + [x[:TILE - s * SUBLANES, :]], axis=0)
        xc = xc + cw_ref[nwrap - s:nwrap - s + 1, :] * shifted
    tail[...] = x[TILE - nwrap * SUBLANES:, :]

    a, b = _lru_gates(xc, wg_ref, ba_ref, bx_ref, lam_ref)
    a_scr[...] = a
    b_scr[...] = b

    w = SCAN_W
    rw = lax.broadcasted_iota(jnp.int32, (SUBLANES, w), 0)
    for c in range(LRU_WIDTH // w):
        sl = slice(c * w, (c + 1) * w)

        def local(r, hc, sl=sl):
            row = pl.multiple_of(r * SUBLANES, SUBLANES)
            av = a_scr[pl.ds(row, SUBLANES), sl]
            h = av * hc[0] + b_scr[pl.ds(row, SUBLANES), sl]
            p = av * hc[1]
            b_scr[pl.ds(row, SUBLANES), sl] = h
            a_scr[pl.ds(row, SUBLANES), sl] = p
            return h, p

        bv, av = lax.fori_loop(0, SEG, local, (jnp.zeros((SUBLANES, w), F32), jnp.ones((SUBLANES, w), F32)))

        for k in (1, 2, 4):
            sa = jnp.where(rw >= k, pltpu.roll(av, k, 0), 1.0)
            sb = jnp.where(rw >= k, pltpu.roll(bv, k, 0), 0.0)
            bv = bv + av * sb
            av = av * sa
        cv = car[:, sl]
        full = bv + av * cv
        enter = jnp.where(rw == 0, cv, pltpu.roll(full, 1, 0))
        car[:, sl] = _bcast_row(full, SUBLANES - 1)

        def fix(r, _, sl=sl, enter=enter):
            row = pl.multiple_of(r * SUBLANES, SUBLANES)
            b_scr[pl.ds(row, SUBLANES), sl] = (b_scr[pl.ds(row, SUBLANES), sl]
                                               + a_scr[pl.ds(row, SUBLANES), sl] * enter)
            return 0

        lax.fori_loop(0, SEG, fix, 0)

    y_ref[...] = (b_scr[...] * _silu(z_ref[...].astype(F32))).astype(y_ref.dtype)

    @pl.when(t == pl.num_programs(1) - 1)
    def _():
        hl_ref[0] = car[0:1, :]
        cv_ref[0] = tail[...]


def _lru_sample_kernel(u_ref, z_ref, h0_ref, cbuf_ref, cw_ref, cb_ref, wg_ref, ba_ref, bx_ref, lam_ref,
                       y_ref, hl_ref, cv_ref, ext):
    nb = DEC_BATCH
    nwrap = CONV_WIDTH - 1
    ext[0:nwrap * nb, :] = cbuf_ref[...]
    ext[nwrap * nb:, :] = u_ref[...].astype(F32)
    h = h0_ref[...]
    for j in range(DEC_SEQ):
        xc = cb_ref[...]
        for k in range(CONV_WIDTH):
            xc = xc + cw_ref[k:k + 1, :] * ext[(j + k) * nb:(j + k + 1) * nb, :]
        a, b = _lru_gates(xc, wg_ref, ba_ref, bx_ref, lam_ref)
        h = a * h + b
        y_ref[j * nb:(j + 1) * nb, :] = (h * _silu(z_ref[j * nb:(j + 1) * nb, :].astype(F32))).astype(y_ref.dtype)
    hl_ref[...] = h
    cv_ref[...] = ext[DEC_SEQ * nb:, :]


def _lru_calls(proj, conv_w, conv_b, wg, b_a, b_x, lam, h0, cbuf):
    w_specs = [_const_spec((CONV_WIDTH, LRU_WIDTH)), _const_spec((1, LRU_WIDTH)),
               _const_spec((LRU_BLOCKS, LRU_BLOCK, 2 * LRU_BLOCK)),
               _const_spec((1, LRU_WIDTH)), _const_spec((1, LRU_WIDTH)), _const_spec((1, LRU_WIDTH))]
    w_args = [conv_w, conv_b.reshape(1, -1), wg, b_a.reshape(1, -1), b_x.reshape(1, -1), lam.reshape(1, -1)]
    row = lambda b, t: b * TILES + t
    yp, hl_p, cv_p = pl.pallas_call(
        _lru_prompt_kernel, grid=(BATCH, TILES),
        in_specs=[pl.BlockSpec((TILE, 1024), lambda b, t: (row(b, t), COL_U_LRU)),
                  pl.BlockSpec((TILE, 1024), lambda b, t: (row(b, t), COL_Z_LRU))] + w_specs,
        out_specs=[pl.BlockSpec((TILE, 1024), lambda b, t: (row(b, t), 0)),
                   pl.BlockSpec((1, 1, LRU_WIDTH), lambda b, t: (b, 0, 0)),
                   pl.BlockSpec((1, (CONV_WIDTH - 1) * SUBLANES, LRU_WIDTH), lambda b, t: (b, 0, 0))],
        out_shape=[jax.ShapeDtypeStruct((N_PROMPT, 1024), BF16),
                   jax.ShapeDtypeStruct((BATCH, 1, LRU_WIDTH), F32),
                   jax.ShapeDtypeStruct((BATCH, (CONV_WIDTH - 1) * SUBLANES, LRU_WIDTH), F32)],
        scratch_shapes=[pltpu.VMEM((TILE, LRU_WIDTH), F32), pltpu.VMEM((TILE, LRU_WIDTH), F32),
                        pltpu.VMEM(((CONV_WIDTH - 1) * SUBLANES, LRU_WIDTH), F32),
                        pltpu.VMEM((SUBLANES, LRU_WIDTH), F32)],
        compiler_params=_params(("parallel", "arbitrary")), name="lru_prompt",
    )(proj, proj, *w_args)

    srow = N_PROMPT // N_SAMPLE
    ys, hl_s, cv_s = pl.pallas_call(
        _lru_sample_kernel, grid=(1,),
        in_specs=[pl.BlockSpec((N_SAMPLE, 1024), lambda i: (srow, COL_U_LRU)),
                  pl.BlockSpec((N_SAMPLE, 1024), lambda i: (srow, COL_Z_LRU)),
                  _const_spec((DEC_BATCH, LRU_WIDTH)),
                  _const_spec(((CONV_WIDTH - 1) * DEC_BATCH, LRU_WIDTH))] + w_specs,
        out_specs=[_const_spec((N_SAMPLE, 1024)), _const_spec((DEC_BATCH, LRU_WIDTH)),
                   _const_spec(((CONV_WIDTH - 1) * DEC_BATCH, LRU_WIDTH))],
        out_shape=[jax.ShapeDtypeStruct((N_SAMPLE, 1024), BF16),
                   jax.ShapeDtypeStruct((DEC_BATCH, LRU_WIDTH), F32),
                   jax.ShapeDtypeStruct(((CONV_WIDTH - 1) * DEC_BATCH, LRU_WIDTH), F32)],
        scratch_shapes=[pltpu.VMEM(((CONV_WIDTH - 1) * DEC_BATCH + N_SAMPLE, LRU_WIDTH), F32)],
        compiler_params=_params(("arbitrary",)), name="lru_sample",
    )(proj, proj, h0, cbuf, *w_args)
    cv_p = cv_p.reshape(BATCH, CONV_WIDTH - 1, SUBLANES, LRU_WIDTH)[:, :, SUBLANES - 1, :]
    cv_s = cv_s.reshape(CONV_WIDTH - 1, DEC_BATCH, LRU_WIDTH).transpose(1, 0, 2)
    return (yp, hl_p, cv_p), (ys, hl_s, cv_s)


RET_SB = 16
RET_SROWS = RET_SB * DEC_SEQ


def _ret_tables(seq, idx):
    n = idx.shape[0]
    chunk = jnp.max(idx) + 1.0
    log_g = jnp.log1p(-jnp.exp2(-5.0 - jnp.arange(RET_HEADS, dtype=F32)))
    diff = idx[:, None] - idx[None, :]
    same = seq[:, None] == seq[None, :]
    dmask = jnp.where((diff[None] >= 0) & same[None],
                      jnp.exp(jnp.maximum(diff, 0.0)[None] * log_g[:, None, None]), 0.0)
    xi = jnp.exp((idx[None, :] + 1.0) * log_g[:, None])
    zeta = jnp.exp((chunk - 1.0 - idx[None, :]) * log_g[:, None])
    gch = jnp.exp(chunk * log_g)
    full = lambda t: jnp.broadcast_to(t[:, :, None], (RET_HEADS, n, LANES))
    gc = jnp.broadcast_to(gch[:, None, None], (RET_HEADS, SUBLANES, LANES))
    return dmask, full(xi), full(zeta), gc


def _rope_tables(pos):
    half = RET_DK // 2
    freq = ROPE_BASE ** (-jnp.arange(half, dtype=F32) / half)
    ang = pos[:, None] * freq[None, :]
    cos, sin = jnp.cos(ang), jnp.sin(ang)
    return jnp.concatenate([cos, cos], axis=-1), jnp.concatenate([-sin, sin], axis=-1)


def _rope(x, cosf, sinf):
    return x * cosf + pltpu.roll(x, RET_DK // 2, 1) * sinf


def _ret_head(h, q, k, v, cos, sin, dmask_ref, zeta_ref):
    sl = slice(h * RET_DK, (h + 1) * RET_DK)
    qb = _rope(q[:, sl].astype(F32), cos, sin).astype(BF16)
    kh = _rope(k[:, sl].astype(F32), cos, sin) * (RET_DK ** -0.5)
    vb = v[:, sl]
    sc = lax.dot_general(qb, kh.astype(BF16), (((1,), (1,)), ((), ())), preferred_element_type=F32)
    inner = jnp.dot((sc * dmask_ref[h]).astype(BF16), vb, preferred_element_type=F32)
    kz = (kh * zeta_ref[h]).astype(BF16)
    return sl, qb, kz, vb, inner


def _ret_finish(o, z, g):
    mu = jnp.mean(o, axis=-1, keepdims=True)
    oc = o - mu
    var = jnp.mean(oc * oc, axis=-1, keepdims=True)
    on = oc * lax.rsqrt(var + GN_EPS) * g
    return (on * _silu(z.astype(F32))).astype(BF16)


def _ret_prompt_kernel(q_ref, k_ref, v_ref, z_ref, cos_ref, sin_ref, dmask_ref, xi_ref, zeta_ref,
                       gc_ref, gng_ref, y_ref, r_ref):
    @pl.when(pl.program_id(1) == 0)
    def _():
        r_ref[...] = jnp.zeros_like(r_ref)

    cos, sin = cos_ref[...], sin_ref[...]
    for h in range(RET_HEADS):
        sl, qb, kz, vb, inner = _ret_head(h, q_ref, k_ref, v_ref, cos, sin, dmask_ref, zeta_ref)
        r = r_ref[0, h]
        cross = jnp.dot(qb, r.astype(BF16), preferred_element_type=F32) * xi_ref[h]
        upd = lax.dot_general(kz, vb, (((0,), (0,)), ((), ())), preferred_element_type=F32)
        r_ref[0, h] = r * gc_ref[h, 0:1, :] + upd
        y_ref[:, sl] = _ret_finish(inner + cross, z_ref[:, sl], gng_ref[:, sl])


def _ret_sample_kernel(q_ref, k_ref, v_ref, z_ref, r0_ref, cos_ref, sin_ref, dmask_ref, xi_ref,
                       zeta_ref, gc_ref, gng_ref, *rest):
    y_ref, r_ref = rest[-2:]
    cos, sin = cos_ref[...], sin_ref[...]
    q = q_ref[...].reshape(RET_SROWS, RET_WIDTH)
    k = k_ref[...].reshape(RET_SROWS, RET_WIDTH)
    v = v_ref[...].reshape(RET_SROWS, RET_WIDTH)
    z = z_ref[...].reshape(RET_SROWS, RET_WIDTH)
    rowseq = lax.broadcasted_iota(jnp.int32, (RET_SROWS, RET_DV), 0) % RET_SB
    outs = []
    for h in range(RET_HEADS):
        sl, qb, kz, vb, inner = _ret_head(h, q, k, v, cos, sin, dmask_ref, zeta_ref)
        rcat = jnp.concatenate([r0_ref[s, h] for s in range(RET_SB)], axis=1).astype(BF16)
        call = jnp.dot(qb, rcat, preferred_element_type=F32)
        cross = jnp.zeros((RET_SROWS, RET_DV), F32)
        for s in range(RET_SB):
            cross = jnp.where(rowseq == s, call[:, s * RET_DV:(s + 1) * RET_DV], cross)
        cross = cross * xi_ref[h]
        vf = vb.astype(F32)
        vexp = jnp.concatenate([jnp.where(rowseq == s, vf, 0.0) for s in range(RET_SB)],
                               axis=1).astype(BF16)
        upd = lax.dot_general(kz, vexp, (((0,), (0,)), ((), ())), preferred_element_type=F32)
        gc = gc_ref[h, 0:1, :]
        for s in range(RET_SB):
            r_ref[s, h] = r0_ref[s, h] * gc + upd[:, s * RET_DV:(s + 1) * RET_DV]
        outs.append(_ret_finish(inner + cross, z[:, sl], gng_ref[:, sl]))
    y_ref[...] = jnp.concatenate(outs, axis=1).reshape(DEC_SEQ, RET_SB, RET_WIDTH)


def _ret_calls(proj, gn_g, r0, layer, r_all):
    def t_specs(n):
        return [_const_spec((RET_HEADS, n, n)), _const_spec((RET_HEADS, n, LANES)),
                _const_spec((RET_HEADS, n, LANES)), _const_spec((RET_HEADS, SUBLANES, LANES)),
                _const_spec((1, RET_WIDTH))]

    rows = jnp.arange(TILE)
    tok = ((rows % SUBLANES) * SEG + rows // SUBLANES).astype(F32)
    tabs = _ret_tables(jnp.zeros((TILE,), jnp.int32), tok)
    pos = (jnp.arange(TILES, dtype=F32)[:, None] * TILE + tok[None, :]).reshape(SEQ) + 0.0
    cosf, sinf = _rope_tables(pos)
    row = lambda b, c: b * TILES + c
    col = lambda j: (lambda b, c: (row(b, c), j))
    yp, r_p = pl.pallas_call(
        _ret_prompt_kernel, grid=(BATCH, TILES),
        in_specs=[pl.BlockSpec((TILE, 1024), col(COL_Q)), pl.BlockSpec((TILE, 1024), col(COL_K)),
                  pl.BlockSpec((TILE, 1024), col(COL_V)), pl.BlockSpec((TILE, 1024), col(COL_Z_RET)),
                  pl.BlockSpec((TILE, LANES), lambda b, c: (c, 0)),
                  pl.BlockSpec((TILE, LANES), lambda b, c: (c, 0))] + t_specs(TILE),
        out_specs=[pl.BlockSpec((TILE, 1024), lambda b, c: (row(b, c), 0)),
                   pl.BlockSpec((1, RET_HEADS, RET_DK, RET_DV), lambda b, c: (b, 0, 0, 0))],
        out_shape=[jax.ShapeDtypeStruct((N_PROMPT, 1024), BF16),
                   jax.ShapeDtypeStruct((BATCH, RET_HEADS, RET_DK, RET_DV), F32)],
        compiler_params=_params(("parallel", "arbitrary")), name="ret_prompt",
    )(proj, proj, proj, proj, cosf, sinf, *tabs, gn_g.reshape(1, -1))

    rows = jnp.arange(RET_SROWS)
    tabs = _ret_tables(rows % RET_SB, (rows // RET_SB).astype(F32))
    cosf, sinf = _rope_tables((rows // RET_SB).astype(F32) + float(PAST_LEN))
    proj3 = proj.reshape(N_TOK // DEC_BATCH, DEC_BATCH, N_IN)
    blk = lambda c: pl.BlockSpec((DEC_SEQ, RET_SB, 1024), lambda i: (SAMPLE_ROW0 // DEC_SEQ, i, c))
    st_block = (None, RET_SB, RET_HEADS, RET_DK, RET_DV)
    st_spec = pl.BlockSpec(st_block, lambda i: (layer, i, 0, 0, 0))
    in_specs = [blk(COL_Q), blk(COL_K), blk(COL_V), blk(COL_Z_RET), st_spec,
                _const_spec((RET_SROWS, LANES)), _const_spec((RET_SROWS, LANES))] + t_specs(RET_SROWS)
    args = [proj3, proj3, proj3, proj3, r0, cosf, sinf, *tabs, gn_g.reshape(1, -1)]
    aliases = {}
    if r_all is not None:
        in_specs.append(pl.BlockSpec(memory_space=pl.ANY))
        args.append(r_all)
        aliases = {len(args) - 1: 1}
    ys, r_all = pl.pallas_call(
        _ret_sample_kernel, grid=(DEC_BATCH // RET_SB,),
        in_specs=in_specs,
        out_specs=[pl.BlockSpec((DEC_SEQ, RET_SB, 1024), lambda i: (0, i, 0)), st_spec],
        out_shape=[jax.ShapeDtypeStruct((DEC_SEQ, DEC_BATCH, 1024), BF16),
                   jax.ShapeDtypeStruct((DEPTH, DEC_BATCH, RET_HEADS, RET_DK, RET_DV), F32)],
        input_output_aliases=aliases,
        compiler_params=_params(("parallel",)), name="ret_sample",
    )(*args)
    return (yp, r_p), (ys.reshape(N_SAMPLE, 1024), r_all)


def _merge_kernel(na, ysp, yss, ylp, yls, yrp, yrs, gs_ref, gl_ref, gr_ref, wb_ref, o_ref):
    acc = None
    for m, (a_ref, b_ref, g_ref) in enumerate(((ysp, yss, gs_ref), (ylp, yls, gl_ref), (yrp, yrs, gr_ref))):
        b = jnp.dot(_pick(na, a_ref, b_ref), wb_ref[m], preferred_element_type=F32)
        term = _sigmoid(g_ref[...].astype(F32)) * b
        acc = term if acc is None else acc + term
    o_ref[...] = acc.astype(o_ref.dtype)


def _merge_call(ys, yl, yr, proj, wb, tm=512):
    specs, na = _two_part_specs(tm, 1024)
    gsp = lambda j: pl.BlockSpec((tm, D_MODEL), lambda i: (i, COL_GATES + j))
    return pl.pallas_call(
        functools.partial(_merge_kernel, na), grid=(N_TOK // tm,),
        in_specs=specs * 3 + [gsp(0), gsp(1), gsp(2), _const_spec((3, 1024, D_MODEL))],
        out_specs=pl.BlockSpec((tm, D_MODEL), lambda i: (i, 0)),
        out_shape=jax.ShapeDtypeStruct((N_TOK, D_MODEL), BF16),
        compiler_params=_params(("parallel",)), name="merge",
    )(*ys, *yl, *yr, proj, proj, proj, wb)


def _outproj_kernel(na, m_ref, w_ref, x_ref, g_ref, oa_ref, ob_ref):
    x = x_ref[...] + jnp.dot(m_ref[...], w_ref[...], preferred_element_type=F32)
    ms = jnp.mean(x * x, axis=-1, keepdims=True)
    xn = x * lax.rsqrt(ms + NORM_EPS) * g_ref[...]
    if na is None:
        oa_ref[...] = x
        ob_ref[...] = xn.astype(ob_ref.dtype)
    else:
        @pl.when(pl.program_id(0) < na)
        def _():
            oa_ref[...] = xn

        @pl.when(pl.program_id(0) >= na)
        def _():
            ob_ref[...] = xn


def _outproj_call(merged, w_out, x, g_next, final, tm=512):
    tok = pl.BlockSpec((tm, D_MODEL), lambda i: (i, 0))
    if final:
        out_specs, na = _two_part_specs(tm, D_MODEL)
        out_shape = [jax.ShapeDtypeStruct((N_PROMPT, D_MODEL), F32), jax.ShapeDtypeStruct((N_SAMPLE, D_MODEL), F32)]
    else:
        out_specs, na = [tok, tok], None
        out_shape = [jax.ShapeDtypeStruct((N_TOK, D_MODEL), F32), jax.ShapeDtypeStruct((N_TOK, D_MODEL), BF16)]
    return pl.pallas_call(
        functools.partial(_outproj_kernel, na), grid=(N_TOK // tm,),
        in_specs=[tok, _const_spec((D_MODEL, D_MODEL)), tok, _const_spec((1, D_MODEL))],
        out_specs=out_specs, out_shape=out_shape,
        compiler_params=_params(("arbitrary",)), name="outproj",
    )(merged, w_out, x, g_next.reshape(1, D_MODEL))


def kernel(x_prompt, x_sample, state_s5_re, state_s5_im, state_lru, state_conv, state_ret, norm_g, w_in, s5_lambda_re, s5_lambda_im, s5_log_dt, s5_b_re, s5_b_im, s5_c_re, s5_c_im, s5_d, s5_w_glu, s5_b_glu, lru_conv_w, lru_conv_b, lru_w_a, lru_b_a, lru_w_x, lru_b_x, lru_lambda, ret_gn_g, w_branch_s5, w_branch_lru, w_branch_ret, w_out, final_norm_g):
    x, xn = _rmsnorm_call(*_to_rows(x_prompt, x_sample), norm_g[0])
    outs_p = [[] for _ in range(5)]
    outs_s = [[] for _ in range(4)]
    r_s = None
    for l in range(DEPTH):
        proj = _inproj_call(xn, w_in, l)

        prep = _s5_prep(s5_lambda_re[l], s5_lambda_im[l], s5_log_dt[l], s5_b_re[l], s5_b_im[l],
                        s5_c_re[l], s5_c_im[l])
        (ys_p, hr_p, hi_p), (ys_s, hr_s, hi_s) = _s5_calls(
            proj, prep, s5_d[l], s5_w_glu[l].astype(BF16), s5_b_glu[l],
            state_s5_re[l].reshape(DEC_BATCH, S5_LANES), state_s5_im[l].reshape(DEC_BATCH, S5_LANES))

        wg = jnp.concatenate([lru_w_a[l], lru_w_x[l]], axis=-1).astype(BF16)
        cbuf = state_conv[l].transpose(1, 0, 2).reshape((CONV_WIDTH - 1) * DEC_BATCH, LRU_WIDTH)
        (yl_p, hl_p, conv_p), (yl_s, hl_s, conv_s) = _lru_calls(
            proj, lru_conv_w[l], lru_conv_b[l], wg, lru_b_a[l], lru_b_x[l], lru_lambda[l], state_lru[l], cbuf)

        (yr_p, r_p), (yr_s, r_s) = _ret_calls(proj, ret_gn_g[l], state_ret, l, r_s)

        wb = jnp.stack([w_branch_s5[l], w_branch_lru[l], w_branch_ret[l]]).astype(BF16)
        merged = _merge_call((ys_p, ys_s), (yl_p, yl_s), (yr_p, yr_s), proj, wb)
        final = l == DEPTH - 1
        g_next = final_norm_g if final else norm_g[l + 1]
        x, xn = _outproj_call(merged, w_out[l].astype(BF16), x, g_next, final)

        st = (S5_GROUPS, S5_STATE)
        for lst, vals in ((outs_p, (hr_p.reshape(BATCH, *st), hi_p.reshape(BATCH, *st),
                                    hl_p.reshape(BATCH, LRU_WIDTH), conv_p, r_p)),
                          (outs_s, (hr_s.reshape(DEC_BATCH, *st), hi_s.reshape(DEC_BATCH, *st), hl_s, conv_s))):
            for j, v in enumerate(vals):
                lst[j].append(v)

    y_prompt, y_sample = _from_rows(x, xn)
    sp = [jnp.stack(t, axis=0) for t in outs_p]
    ss = [jnp.stack(t, axis=0) for t in outs_s]
    return (y_prompt, y_sample, *sp, *ss, r_s)
```

```python
import functools

import jax
import jax.numpy as jnp
from jax import lax
from jax.experimental import pallas as pl
from jax.experimental.pallas import tpu as pltpu

F32 = jnp.float32
BF16 = jnp.bfloat16

D_MODEL = 2048
BATCH = 4
SEQ = 2048
DEPTH = 2
DEC_BATCH = 128
DEC_SEQ = 8
PAST_LEN = 16384
S5_WIDTH = 1024
S5_GROUP = 16
S5_GROUPS = 64
S5_STATE = 64
S5_LANES = S5_GROUPS * S5_STATE
LRU_WIDTH = 1024
LRU_BLOCKS = 8
LRU_BLOCK = 128
CONV_WIDTH = 4
LRU_C = 8.0
RET_HEADS = 8
RET_DK = 128
RET_DV = 128
RET_WIDTH = 1024
ROPE_BASE = 10000.0
NORM_EPS = 1e-6
GN_EPS = 1e-5
N_IN = 14336

N_PROMPT = BATCH * SEQ
N_SAMPLE = DEC_BATCH * DEC_SEQ
N_TOK = N_PROMPT + N_SAMPLE

SUBLANES = 8
LANES = 128
VMEM_LIMIT = 56 * 1024 * 1024

TILE = 256
SEG = TILE // SUBLANES
TILES = SEQ // TILE

PROJ_GROUPS = ((0, 2), (2, 4), (6, 4), (10, 4))
MM_COLS = 1024
MM_UNITS = TILES
MM_ROWS = N_TOK // MM_UNITS
MM_COL_PIECES = 2
MM_PIECE_COLS = MM_COLS // MM_COL_PIECES
MM_ROW_PIECES = 4
MM_PIECE_ROWS = MM_ROWS // MM_ROW_PIECES
MM_PIECES = MM_COL_PIECES * MM_ROW_PIECES

S5_KB = 4
S5_KW = S5_WIDTH // S5_KB
S5_NW = S5_LANES // S5_KB
SCAN_W = 512

SAMPLE_ROW0 = N_PROMPT // DEC_BATCH


def _params(sem, vmem=VMEM_LIMIT):
    return pltpu.CompilerParams(dimension_semantics=sem, vmem_limit_bytes=vmem)


def _const_spec(shape, single=False):
    return pl.BlockSpec(shape, lambda *_: (0,) * len(shape), pipeline_mode=pl.Buffered(1) if single else None)


def _sigmoid(x):
    return jax.nn.sigmoid(x)


def _silu(x):
    return x * jax.nn.sigmoid(x)


def _bcast_row(x, row):
    return jnp.broadcast_to(x[row:row + 1, :], x.shape)


def _cmul(ar, ai, br, bi):
    return ar * br - ai * bi, ar * bi + ai * br


def _to_rows(x_prompt, x_sample):
    xp = x_prompt.reshape(BATCH, TILES, SUBLANES, SEG, -1).transpose(0, 1, 3, 2, 4).reshape(N_PROMPT, -1)
    xs = x_sample.transpose(1, 0, 2).reshape(N_SAMPLE, -1)
    return xp, xs


def _from_rows(yp, ys):
    yp = yp.reshape(BATCH, TILES, SEG, SUBLANES, -1).transpose(0, 1, 3, 2, 4).reshape(BATCH, SEQ, -1)
    ys = ys.reshape(DEC_SEQ, DEC_BATCH, -1).transpose(1, 0, 2)
    return yp, ys


def _two_part_specs(tm, width):
    na = N_PROMPT // tm
    return [pl.BlockSpec((tm, width), lambda i: (jnp.minimum(i, na - 1), 0)),
            pl.BlockSpec((tm, width), lambda i: (jnp.maximum(i - na, 0), 0))], na


def _pick(na, a_ref, b_ref):
    return jnp.where(pl.program_id(0) < na, a_ref[...], b_ref[...])


def _norm_kernel(na, xa_ref, xb_ref, g_ref, x_ref, o_ref):
    x = _pick(na, xa_ref, xb_ref)
    x_ref[...] = x
    ms = jnp.mean(x * x, axis=-1, keepdims=True)
    o_ref[...] = (x * lax.rsqrt(ms + NORM_EPS) * g_ref[...]).astype(o_ref.dtype)


def _rmsnorm_call(xp, xs, g, tm=512):
    specs, na = _two_part_specs(tm, D_MODEL)
    tok = pl.BlockSpec((tm, D_MODEL), lambda i: (i, 0))
    return pl.pallas_call(
        functools.partial(_norm_kernel, na),
        grid=(N_TOK // tm,),
        in_specs=specs + [_const_spec((1, D_MODEL))],
        out_specs=[tok, tok],
        out_shape=[jax.ShapeDtypeStruct((N_TOK, D_MODEL), F32), jax.ShapeDtypeStruct((N_TOK, D_MODEL), BF16)],
        compiler_params=_params(("parallel",)),
        name="rmsnorm",
    )(xp, xs, g.reshape(1, D_MODEL))


def _inproj_kernel(mixer, n_in, n_out, *refs):
    xn_ref, w_ref = refs[n_in:n_in + 2]
    o_ref = refs[n_in + 2 + n_out]
    wbf_ref = refs[n_in + 3 + n_out]

    @pl.when(pl.program_id(0) % MM_UNITS == 0)
    def _():
        wbf_ref[...] = w_ref[...].astype(BF16)

    def piece(k):
        rows = slice((k // MM_COL_PIECES) * MM_PIECE_ROWS, (k // MM_COL_PIECES + 1) * MM_PIECE_ROWS)
        cols = slice((k % MM_COL_PIECES) * MM_PIECE_COLS, (k % MM_COL_PIECES + 1) * MM_PIECE_COLS)
        o_ref[rows, cols] = jnp.dot(xn_ref[rows, :], wbf_ref[:, cols],
                                    preferred_element_type=F32).astype(o_ref.dtype)

    if mixer is None:
        for k in range(MM_PIECES):
            piece(k)
    else:
        mixer(piece, *refs[:n_in], *refs[n_in + 2:n_in + 2 + n_out], *refs[n_in + 4 + n_out:])


def _inproj_call(name, xn, w_in, layer, group, mixer=None, in_specs=(), args=(), out_specs=(), out_shape=(),
                 scratch=()):
    first, tiles = PROJ_GROUPS[group]
    mm_in = [pl.BlockSpec((MM_ROWS, D_MODEL), lambda s: (s % MM_UNITS, 0)),
             pl.BlockSpec((None, D_MODEL, MM_COLS), lambda s: (layer, 0, first + s // MM_UNITS))]
    mm_out = pl.BlockSpec((MM_ROWS, MM_COLS), lambda s: (s % MM_UNITS, s // MM_UNITS))
    res = pl.pallas_call(
        functools.partial(_inproj_kernel, mixer, len(in_specs), len(out_specs)),
        grid=(tiles * MM_UNITS,),
        in_specs=list(in_specs) + mm_in,
        out_specs=list(out_specs) + [mm_out],
        out_shape=list(out_shape) + [jax.ShapeDtypeStruct((N_TOK, tiles * MM_COLS), BF16)],
        scratch_shapes=[pltpu.VMEM((D_MODEL, MM_COLS), BF16)] + list(scratch),
        compiler_params=_params(("arbitrary",)),
        name=name,
    )(*args, xn, w_in)
    return res[:-1], res[-1]


def _s5_prep(lam_re, lam_im, log_dt, b_re, b_im, c_re, c_im):
    dt = jnp.exp(log_dt)[:, None]
    e = jnp.exp(lam_re * dt)
    lbr = e * jnp.cos(lam_im * dt)
    lbi = e * jnp.sin(lam_im * dt)
    nr, ni = lbr - 1.0, lbi
    den = lam_re * lam_re + lam_im * lam_im
    cr = (nr * lam_re + ni * lam_im) / den
    ci = (ni * lam_re - nr * lam_im) / den
    bbr = cr[..., None] * b_re - ci[..., None] * b_im
    bbi = cr[..., None] * b_im + ci[..., None] * b_re
    gpb = S5_GROUPS // S5_KB

    def bblk(bb):
        t = bb.astype(BF16).reshape(S5_KB, gpb, S5_STATE, S5_GROUP).transpose(0, 1, 3, 2)
        t = jnp.tile(t.reshape(S5_KB, S5_KW, S5_STATE), (1, 1, gpb))
        same = (jnp.arange(S5_KW)[:, None] // S5_GROUP) == (jnp.arange(S5_NW)[None, :] // S5_STATE)
        return jnp.where(same[None], t, 0)

    def cblk(cc):
        t = cc.astype(BF16).reshape(S5_KB, gpb, S5_GROUP, S5_STATE).transpose(0, 1, 3, 2)
        t = jnp.tile(t.reshape(S5_KB, S5_NW, S5_GROUP), (1, 1, gpb))
        same = (jnp.arange(S5_NW)[:, None] // S5_STATE) == (jnp.arange(S5_KW)[None, :] // S5_GROUP)
        return jnp.where(same[None], t, 0)

    b_blk = jnp.concatenate([bblk(bbr), bblk(bbi)], axis=2)
    c_blk = jnp.concatenate([cblk(c_re), cblk(-c_im)], axis=1)

    lr, li = lbr.reshape(-1), lbi.reshape(-1)
    full = lambda v: jnp.broadcast_to(v, (SUBLANES, S5_LANES))
    lam = jnp.stack([full(lr), full(li)])
    sr, si = lr, li
    for _ in range(SEG.bit_length() - 1):
        sr, si = _cmul(sr, si, sr, si)
    pr, pi = [sr], [si]
    for _ in range(SUBLANES - 1):
        r_, i_ = _cmul(pr[-1], pi[-1], sr, si)
        pr.append(r_)
        pi.append(i_)
    row = jnp.arange(SUBLANES)[:, None]
    ak = jnp.stack([jnp.stack([jnp.where(row >= k, pr[k - 1][None, :], 0.0),
                               jnp.where(row >= k, pi[k - 1][None, :], 0.0)]) for k in (1, 2, 4)])
    pw = jnp.stack([jnp.stack(pr), jnp.stack(pi)])
    return b_blk, c_blk, lam, ak, pw


def _s5_drive_block(kb, u, bblk_ref, scr):
    res = jnp.dot(u[:, kb * S5_KW:(kb + 1) * S5_KW], bblk_ref[kb], preferred_element_type=F32)
    scr[:, kb * S5_NW:(kb + 1) * S5_NW] = res[:, :S5_NW]
    scr[:, S5_LANES + kb * S5_NW:S5_LANES + (kb + 1) * S5_NW] = res[:, S5_NW:]


def _s5_readout_block(kb, scr, cblk_ref):
    hcat = jnp.concatenate(
        [scr[:, kb * S5_NW:(kb + 1) * S5_NW],
         scr[:, S5_LANES + kb * S5_NW:S5_LANES + (kb + 1) * S5_NW]], axis=1).astype(BF16)
    return jnp.dot(hcat, cblk_ref[kb], preferred_element_type=F32)


def _s5_finish(parts, u, z, d_ref, wglu_ref, bglu_ref):
    y = jnp.concatenate(parts, axis=1) + d_ref[...] * u.astype(F32)
    y = jax.nn.gelu(y, approximate=True)
    glu = jnp.dot(y.astype(BF16), wglu_ref[...], preferred_element_type=F32) + bglu_ref[...]
    y = y * _sigmoid(glu)
    return (y * _silu(z.astype(F32))).astype(BF16)


def _s5_prompt_kernel(mm_piece, u_ref, z_ref, bblk_ref, cblk_ref, lam_ref, ak_ref, pw_ref, d_ref, wglu_ref,
                      bglu_ref, y_ref, hr_ref, hi_ref, scr, car):
    t = pl.program_id(0) % TILES

    @pl.when(t == 0)
    def _():
        car[...] = jnp.zeros_like(car)

    w = SCAN_W
    rowi = lax.broadcasted_iota(jnp.int32, (SUBLANES, w), 0)
    u = u_ref[...]
    parts = []
    for kb in range(S5_KB):
        _s5_drive_block(kb, u, bblk_ref, scr)
        for c in range(kb * (S5_NW // w), (kb + 1) * (S5_NW // w)):
            mm_piece(c)
            sl_re = slice(c * w, (c + 1) * w)
            sl_im = slice(S5_LANES + c * w, S5_LANES + (c + 1) * w)
            lr, li = lam_ref[0, :, sl_re], lam_ref[1, :, sl_re]

            xr = xi = jnp.zeros((SUBLANES, w), F32)
            for r in range(SEG):
                rows = slice(r * SUBLANES, (r + 1) * SUBLANES)
                pr, pi = _cmul(lr, li, xr, xi)
                xr = scr[rows, sl_re] + pr
                xi = scr[rows, sl_im] + pi
                scr[rows, sl_re] = xr
                scr[rows, sl_im] = xi

            for k, idx in zip((1, 2, 4), range(3)):
                pr, pi = _cmul(ak_ref[idx, 0, :, sl_re], ak_ref[idx, 1, :, sl_re],
                               pltpu.roll(xr, k, 0), pltpu.roll(xi, k, 0))
                xr, xi = xr + pr, xi + pi
            cr, ci = car[0, :, sl_re], car[1, :, sl_re]
            pr, pi = _cmul(pw_ref[0, :, sl_re], pw_ref[1, :, sl_re], cr, ci)
            fr, fi = xr + pr, xi + pi
            dr = jnp.where(rowi == 0, cr, pltpu.roll(fr, 1, 0))
            di = jnp.where(rowi == 0, ci, pltpu.roll(fi, 1, 0))
            car[0, :, sl_re] = _bcast_row(fr, SUBLANES - 1)
            car[1, :, sl_re] = _bcast_row(fi, SUBLANES - 1)

            for r in range(SEG):
                rows = slice(r * SUBLANES, (r + 1) * SUBLANES)
                dr, di = _cmul(lr, li, dr, di)
                scr[rows, sl_re] = scr[rows, sl_re] + dr
                scr[rows, sl_im] = scr[rows, sl_im] + di
        parts.append(_s5_readout_block(kb, scr, cblk_ref))

    y_ref[...] = _s5_finish(parts, u, z_ref[...], d_ref, wglu_ref, bglu_ref)

    @pl.when(t == TILES - 1)
    def _():
        hr_ref[0] = car[0, 0:1, :]
        hi_ref[0] = car[1, 0:1, :]


S5_SB = 32


def _s5_sample_kernel(u_ref, z_ref, h0r_ref, h0i_ref, bblk_ref, cblk_ref, lam_ref, d_ref, wglu_ref, bglu_ref,
                      y_ref, hr_ref, hi_ref, scr):
    rows = DEC_SEQ * S5_SB
    u = u_ref[...].reshape(rows, S5_WIDTH)
    for kb in range(S5_KB):
        _s5_drive_block(kb, u, bblk_ref, scr)

    w = SCAN_W
    for c in range(S5_LANES // w):
        sl_re = slice(c * w, (c + 1) * w)
        sl_im = slice(S5_LANES + c * w, S5_LANES + (c + 1) * w)
        lr, li = lam_ref[0, :, sl_re], lam_ref[1, :, sl_re]

        def body(g, _, sl_re=sl_re, sl_im=sl_im, lr=lr, li=li):
            row = pl.multiple_of(g * SUBLANES, SUBLANES)
            hr = h0r_ref[pl.ds(row, SUBLANES), sl_re]
            hi = h0i_ref[pl.ds(row, SUBLANES), sl_re]
            for j in range(DEC_SEQ):
                pr, pi = _cmul(lr, li, hr, hi)
                hr = scr[pl.ds(j * S5_SB + row, SUBLANES), sl_re] + pr
                hi = scr[pl.ds(j * S5_SB + row, SUBLANES), sl_im] + pi
                scr[pl.ds(j * S5_SB + row, SUBLANES), sl_re] = hr
                scr[pl.ds(j * S5_SB + row, SUBLANES), sl_im] = hi
            hr_ref[pl.ds(row, SUBLANES), sl_re] = hr
            hi_ref[pl.ds(row, SUBLANES), sl_re] = hi
            return 0

        lax.fori_loop(0, S5_SB // SUBLANES, body, 0)

    parts = [_s5_readout_block(kb, scr, cblk_ref) for kb in range(S5_KB)]
    y = _s5_finish(parts, u, z_ref[...].reshape(rows, S5_WIDTH), d_ref, wglu_ref, bglu_ref)
    y_ref[...] = y.reshape(DEC_SEQ, S5_SB, S5_WIDTH)


def _tile_spec(col):
    return pl.BlockSpec((TILE, 1024), lambda s: (s, col))


def _state_spec(*shape):
    return pl.BlockSpec((1,) + shape, lambda s: (s // TILES,) + (0,) * len(shape))


def _sample3(proj):
    return proj.reshape(N_TOK // DEC_BATCH, DEC_BATCH, proj.shape[-1])


def _sample_spec(nseq, col):
    return pl.BlockSpec((DEC_SEQ, nseq, 1024), lambda i: (SAMPLE_ROW0 // DEC_SEQ, i, col))


def _s5_specs(prep, d, wglu, bglu, single):
    b_blk, c_blk, lam, ak, pw = prep
    spec = functools.partial(_const_spec, single=single)
    lam_spec = spec((2, SUBLANES, S5_LANES))
    mm = ([spec((S5_KB, S5_KW, 2 * S5_NW)), spec((S5_KB, 2 * S5_NW, S5_KW))], [b_blk, c_blk])
    chain = ([spec((3, 2, SUBLANES, S5_LANES)), lam_spec], [ak, pw])
    tail = ([spec((1, S5_WIDTH)), spec((S5_WIDTH, S5_WIDTH)), spec((1, S5_WIDTH))],
            [d.reshape(1, -1), wglu, bglu.reshape(1, -1)])
    return mm, (lam_spec, lam), chain, tail


def _s5_prompt_mixer(proj_a, prep, d, wglu, bglu):
    mm, (lam_spec, lam), chain, tail = _s5_specs(prep, d, wglu, bglu, single=True)
    return dict(
        mixer=_s5_prompt_kernel,
        in_specs=[_tile_spec(0), _tile_spec(1)] + mm[0] + [lam_spec] + chain[0] + tail[0],
        args=[proj_a, proj_a] + mm[1] + [lam] + chain[1] + tail[1],
        out_specs=[_tile_spec(0), _state_spec(1, S5_LANES), _state_spec(1, S5_LANES)],
        out_shape=[jax.ShapeDtypeStruct((N_PROMPT, 1024), BF16),
                   jax.ShapeDtypeStruct((BATCH, 1, S5_LANES), F32),
                   jax.ShapeDtypeStruct((BATCH, 1, S5_LANES), F32)],
        scratch=[pltpu.VMEM((TILE, 2 * S5_LANES), F32), pltpu.VMEM((2, SUBLANES, S5_LANES), F32)])


def _s5_sample_call(proj_a, prep, d, wglu, bglu, h0r, h0i):
    mm, (lam_spec, lam), _, tail = _s5_specs(prep, d, wglu, bglu, single=False)
    proj3 = _sample3(proj_a)
    st = pl.BlockSpec((S5_SB, S5_LANES), lambda i: (i, 0))
    ys, hr_s, hi_s = pl.pallas_call(
        _s5_sample_kernel, grid=(DEC_BATCH // S5_SB,),
        in_specs=[_sample_spec(S5_SB, 0), _sample_spec(S5_SB, 1), st, st] + mm[0] + [lam_spec] + tail[0],
        out_specs=[pl.BlockSpec((DEC_SEQ, S5_SB, 1024), lambda i: (0, i, 0)), st, st],
        out_shape=[jax.ShapeDtypeStruct((DEC_SEQ, DEC_BATCH, 1024), BF16),
                   jax.ShapeDtypeStruct((DEC_BATCH, S5_LANES), F32),
                   jax.ShapeDtypeStruct((DEC_BATCH, S5_LANES), F32)],
        scratch_shapes=[pltpu.VMEM((DEC_SEQ * S5_SB, 2 * S5_LANES), F32)],
        compiler_params=_params(("parallel",)), name="s5_sample",
    )(proj3, proj3, h0r, h0i, *mm[1], lam, *tail[1])
    return ys.reshape(N_SAMPLE, 1024), hr_s, hi_s


def _lru_gates_block(n, xc, wg_ref, ba_ref, bx_ref, lam_ref):
    sl = slice(n * LRU_BLOCK, (n + 1) * LRU_BLOCK)
    xb = xc[:, sl]
    res = jnp.dot(xb.astype(BF16), wg_ref[n], preferred_element_type=F32)
    r_gate = _sigmoid(res[:, :LRU_BLOCK] + ba_ref[:, sl])
    i_gate = _sigmoid(res[:, LRU_BLOCK:] + bx_ref[:, sl])
    nl = -lam_ref[:, sl]
    softplus = jnp.maximum(nl, 0.0) + jnp.log1p(jnp.exp(-jnp.abs(nl)))
    a = jnp.exp(-LRU_C * r_gate * softplus)
    return a, jnp.sqrt(1.0 - a * a) * (i_gate * xb)


def _lru_gates(xc, wg_ref, ba_ref, bx_ref, lam_ref):
    ab = [_lru_gates_block(n, xc, wg_ref, ba_ref, bx_ref, lam_ref) for n in range(LRU_BLOCKS)]
    return jnp.concatenate([a for a, _ in ab], axis=1), jnp.concatenate([b for _, b in ab], axis=1)


def _lru_prompt_kernel(mm_piece, u_ref, z_ref, cw_ref, cb_ref, wg_ref, ba_ref, bx_ref, lam_ref,
                       y_ref, hl_ref, cv_ref, a_scr, b_scr, tail, car):
    t = pl.program_id(0) % TILES
    nwrap = CONV_WIDTH - 1

    @pl.when(t == 0)
    def _():
        tail[...] = jnp.zeros_like(tail)
        car[...] = jnp.zeros_like(car)

    x = u_ref[...].astype(F32)
    rowi = lax.broadcasted_iota(jnp.int32, (SUBLANES, LRU_WIDTH), 0)
    wrap = []
    for k in range(nwrap):
        cur = x[TILE - (nwrap - k) * SUBLANES:TILE - (nwrap - k - 1) * SUBLANES, :]
        prev = tail[k * SUBLANES:(k + 1) * SUBLANES, :]
        wrap.append(jnp.where(rowi == 0, pltpu.roll(prev, 1, 0), pltpu.roll(cur, 1, 0)))
    xc = cw_ref[nwrap:nwrap + 1, :] * x + cb_ref[...]
    for s in range(1, CONV_WIDTH):
        shifted = jnp.concatenate(wrap[nwrap - s:] + [x[:TILE - s * SUBLANES, :]], axis=0)
        xc = xc + cw_ref[nwrap - s:nwrap - s + 1, :] * shifted
    tail[...] = x[TILE - nwrap * SUBLANES:, :]

    for n in range(LRU_BLOCKS):
        mm_piece(n)
        sl = slice(n * LRU_BLOCK, (n + 1) * LRU_BLOCK)
        a_scr[:, sl], b_scr[:, sl] = _lru_gates_block(n, xc, wg_ref, ba_ref, bx_ref, lam_ref)

    w = SCAN_W
    rw = lax.broadcasted_iota(jnp.int32, (SUBLANES, w), 0)
    for c in range(LRU_WIDTH // w):
        sl = slice(c * w, (c + 1) * w)

        bv = jnp.zeros((SUBLANES, w), F32)
        av = jnp.ones((SUBLANES, w), F32)
        for r in range(SEG):
            rows = slice(r * SUBLANES, (r + 1) * SUBLANES)
            ar = a_scr[rows, sl]
            bv = ar * bv + b_scr[rows, sl]
            av = ar * av
            b_scr[rows, sl] = bv
            a_scr[rows, sl] = av

        for k in (1, 2, 4):
            sa = jnp.where(rw >= k, pltpu.roll(av, k, 0), 1.0)
            sb = jnp.where(rw >= k, pltpu.roll(bv, k, 0), 0.0)
            bv = bv + av * sb
            av = av * sa
        cv = car[:, sl]
        full = bv + av * cv
        enter = jnp.where(rw == 0, cv, pltpu.roll(full, 1, 0))
        car[:, sl] = _bcast_row(full, SUBLANES - 1)

        for r in range(SEG):
            rows = slice(r * SUBLANES, (r + 1) * SUBLANES)
            b_scr[rows, sl] = b_scr[rows, sl] + a_scr[rows, sl] * enter

    y_ref[...] = (b_scr[...] * _silu(z_ref[...].astype(F32))).astype(y_ref.dtype)

    @pl.when(t == TILES - 1)
    def _():
        hl_ref[0] = car[0:1, :]
        cv_ref[0] = tail[...]


def _lru_sample_kernel(u_ref, z_ref, h0_ref, cbuf_ref, cw_ref, cb_ref, wg_ref, ba_ref, bx_ref, lam_ref,
                       y_ref, hl_ref, cv_ref, ext):
    nb = DEC_BATCH
    nwrap = CONV_WIDTH - 1
    ext[0:nwrap * nb, :] = cbuf_ref[...]
    ext[nwrap * nb:, :] = u_ref[...].astype(F32)
    h = h0_ref[...]
    for j in range(DEC_SEQ):
        xc = cb_ref[...]
        for k in range(CONV_WIDTH):
            xc = xc + cw_ref[k:k + 1, :] * ext[(j + k) * nb:(j + k + 1) * nb, :]
        a, b = _lru_gates(xc, wg_ref, ba_ref, bx_ref, lam_ref)
        h = a * h + b
        y_ref[j * nb:(j + 1) * nb, :] = (h * _silu(z_ref[j * nb:(j + 1) * nb, :].astype(F32))).astype(y_ref.dtype)
    hl_ref[...] = h
    cv_ref[...] = ext[DEC_SEQ * nb:, :]


def _lru_weights(conv_w, conv_b, wg, b_a, b_x, lam):
    specs = [_const_spec((CONV_WIDTH, LRU_WIDTH)), _const_spec((1, LRU_WIDTH)),
             _const_spec((LRU_BLOCKS, LRU_BLOCK, 2 * LRU_BLOCK)),
             _const_spec((1, LRU_WIDTH)), _const_spec((1, LRU_WIDTH)), _const_spec((1, LRU_WIDTH))]
    args = [conv_w, conv_b.reshape(1, -1), wg, b_a.reshape(1, -1), b_x.reshape(1, -1), lam.reshape(1, -1)]
    return specs, args


def _lru_prompt_mixer(proj_b, weights):
    w_specs, w_args = weights
    nrows = (CONV_WIDTH - 1) * SUBLANES
    return dict(
        mixer=_lru_prompt_kernel,
        in_specs=[_tile_spec(0), _tile_spec(1)] + w_specs,
        args=[proj_b, proj_b] + w_args,
        out_specs=[_tile_spec(0), _state_spec(1, LRU_WIDTH), _state_spec(nrows, LRU_WIDTH)],
        out_shape=[jax.ShapeDtypeStruct((N_PROMPT, 1024), BF16),
                   jax.ShapeDtypeStruct((BATCH, 1, LRU_WIDTH), F32),
                   jax.ShapeDtypeStruct((BATCH, nrows, LRU_WIDTH), F32)],
        scratch=[pltpu.VMEM((TILE, LRU_WIDTH), F32), pltpu.VMEM((TILE, LRU_WIDTH), F32),
                 pltpu.VMEM((nrows, LRU_WIDTH), F32), pltpu.VMEM((SUBLANES, LRU_WIDTH), F32)])


def _lru_prompt_conv_state(cv_p):
    return cv_p.reshape(BATCH, CONV_WIDTH - 1, SUBLANES, LRU_WIDTH)[:, :, SUBLANES - 1, :]


def _lru_sample_call(proj_b, weights, h0, cbuf):
    w_specs, w_args = weights
    srow = N_PROMPT // N_SAMPLE
    ys, hl_s, cv_s = pl.pallas_call(
        _lru_sample_kernel, grid=(1,),
        in_specs=[pl.BlockSpec((N_SAMPLE, 1024), lambda i: (srow, 0)),
                  pl.BlockSpec((N_SAMPLE, 1024), lambda i: (srow, 1)),
                  _const_spec((DEC_BATCH, LRU_WIDTH)),
                  _const_spec(((CONV_WIDTH - 1) * DEC_BATCH, LRU_WIDTH))] + w_specs,
        out_specs=[_const_spec((N_SAMPLE, 1024)), _const_spec((DEC_BATCH, LRU_WIDTH)),
                   _const_spec(((CONV_WIDTH - 1) * DEC_BATCH, LRU_WIDTH))],
        out_shape=[jax.ShapeDtypeStruct((N_SAMPLE, 1024), BF16),
                   jax.ShapeDtypeStruct((DEC_BATCH, LRU_WIDTH), F32),
                   jax.ShapeDtypeStruct(((CONV_WIDTH - 1) * DEC_BATCH, LRU_WIDTH), F32)],
        scratch_shapes=[pltpu.VMEM(((CONV_WIDTH - 1) * DEC_BATCH + N_SAMPLE, LRU_WIDTH), F32)],
        compiler_params=_params(("arbitrary",)), name="lru_sample",
    )(proj_b, proj_b, h0, cbuf, *w_args)
    return ys, hl_s, cv_s.reshape(CONV_WIDTH - 1, DEC_BATCH, LRU_WIDTH).transpose(1, 0, 2)


RET_SB = 16
RET_SROWS = RET_SB * DEC_SEQ


def _ret_tables(seq, idx):
    n = idx.shape[0]
    chunk = jnp.max(idx) + 1.0
    log_g = jnp.log1p(-jnp.exp2(-5.0 - jnp.arange(RET_HEADS, dtype=F32)))
    diff = idx[:, None] - idx[None, :]
    same = seq[:, None] == seq[None, :]
    dmask = jnp.where((diff[None] >= 0) & same[None],
                      jnp.exp(jnp.maximum(diff, 0.0)[None] * log_g[:, None, None]), 0.0)
    xi = jnp.exp((idx[None, :] + 1.0) * log_g[:, None])
    zeta = jnp.exp((chunk - 1.0 - idx[None, :]) * log_g[:, None])
    gch = jnp.exp(chunk * log_g)
    full = lambda t: jnp.broadcast_to(t[:, :, None], (RET_HEADS, n, LANES))
    gc = jnp.broadcast_to(gch[:, None, None], (RET_HEADS, SUBLANES, LANES))
    return dmask, full(xi), full(zeta), gc


def _rope_tables(pos):
    half = RET_DK // 2
    freq = ROPE_BASE ** (-jnp.arange(half, dtype=F32) / half)
    ang = pos[:, None] * freq[None, :]
    cos, sin = jnp.cos(ang), jnp.sin(ang)
    return jnp.concatenate([cos, cos], axis=-1), jnp.concatenate([-sin, sin], axis=-1)


def _rope(x, cosf, sinf):
    return x * cosf + pltpu.roll(x, RET_DK // 2, 1) * sinf


def _ret_head(h, q, k, v, cos, sin, dmask_ref, zeta_ref):
    sl = slice(h * RET_DK, (h + 1) * RET_DK)
    qb = _rope(q[:, sl].astype(F32), cos, sin).astype(BF16)
    kh = _rope(k[:, sl].astype(F32), cos, sin) * (RET_DK ** -0.5)
    vb = v[:, sl]
    sc = lax.dot_general(qb, kh.astype(BF16), (((1,), (1,)), ((), ())), preferred_element_type=F32)
    inner = jnp.dot((sc * dmask_ref[h]).astype(BF16), vb, preferred_element_type=F32)
    kz = (kh * zeta_ref[h]).astype(BF16)
    return sl, qb, kz, vb, inner


def _ret_finish(o, z, g):
    mu = jnp.mean(o, axis=-1, keepdims=True)
    oc = o - mu
    var = jnp.mean(oc * oc, axis=-1, keepdims=True)
    on = oc * lax.rsqrt(var + GN_EPS) * g
    return (on * _silu(z.astype(F32))).astype(BF16)


def _ret_prompt_kernel(mm_piece, q_ref, k_ref, v_ref, z_ref, cos_ref, sin_ref, dmask_ref, xi_ref, zeta_ref,
                       gc_ref, gng_ref, y_ref, r_ref):
    @pl.when(pl.program_id(0) % TILES == 0)
    def _():
        r_ref[...] = jnp.zeros_like(r_ref)

    cos, sin = cos_ref[...], sin_ref[...]
    for h in range(RET_HEADS):
        mm_piece(h)
        sl, qb, kz, vb, inner = _ret_head(h, q_ref, k_ref, v_ref, cos, sin, dmask_ref, zeta_ref)
        r = r_ref[0, h]
        cross = jnp.dot(qb, r.astype(BF16), preferred_element_type=F32) * xi_ref[h]
        upd = lax.dot_general(kz, vb, (((0,), (0,)), ((), ())), preferred_element_type=F32)
        r_ref[0, h] = r * gc_ref[h, 0:1, :] + upd
        y_ref[:, sl] = _ret_finish(inner + cross, z_ref[:, sl], gng_ref[:, sl])


def _ret_sample_kernel(q_ref, k_ref, v_ref, z_ref, r0_ref, cos_ref, sin_ref, dmask_ref, xi_ref,
                       zeta_ref, gc_ref, gng_ref, *rest):
    y_ref, r_ref = rest[-2:]
    cos, sin = cos_ref[...], sin_ref[...]
    q = q_ref[...].reshape(RET_SROWS, RET_WIDTH)
    k = k_ref[...].reshape(RET_SROWS, RET_WIDTH)
    v = v_ref[...].reshape(RET_SROWS, RET_WIDTH)
    z = z_ref[...].reshape(RET_SROWS, RET_WIDTH)
    rowseq = lax.broadcasted_iota(jnp.int32, (RET_SROWS, RET_DV), 0) % RET_SB
    outs = []
    for h in range(RET_HEADS):
        sl, qb, kz, vb, inner = _ret_head(h, q, k, v, cos, sin, dmask_ref, zeta_ref)
        rcat = jnp.concatenate([r0_ref[s, h] for s in range(RET_SB)], axis=1).astype(BF16)
        call = jnp.dot(qb, rcat, preferred_element_type=F32)
        cross = jnp.zeros((RET_SROWS, RET_DV), F32)
        for s in range(RET_SB):
            cross = jnp.where(rowseq == s, call[:, s * RET_DV:(s + 1) * RET_DV], cross)
        cross = cross * xi_ref[h]
        vf = vb.astype(F32)
        vexp = jnp.concatenate([jnp.where(rowseq == s, vf, 0.0) for s in range(RET_SB)],
                               axis=1).astype(BF16)
        upd = lax.dot_general(kz, vexp, (((0,), (0,)), ((), ())), preferred_element_type=F32)
        gc = gc_ref[h, 0:1, :]
        for s in range(RET_SB):
            r_ref[s, h] = r0_ref[s, h] * gc + upd[:, s * RET_DV:(s + 1) * RET_DV]
        outs.append(_ret_finish(inner + cross, z[:, sl], gng_ref[:, sl]))
    y_ref[...] = jnp.concatenate(outs, axis=1).reshape(DEC_SEQ, RET_SB, RET_WIDTH)


def _ret_table_specs(n):
    return [_const_spec((RET_HEADS, n, n)), _const_spec((RET_HEADS, n, LANES)),
            _const_spec((RET_HEADS, n, LANES)), _const_spec((RET_HEADS, SUBLANES, LANES)),
            _const_spec((1, RET_WIDTH))]


def _ret_prompt_mixer(proj_b, proj_c, gn_g):
    rows = jnp.arange(TILE)
    tok = ((rows % SUBLANES) * SEG + rows // SUBLANES).astype(F32)
    tabs = _ret_tables(jnp.zeros((TILE,), jnp.int32), tok)
    pos = (jnp.arange(TILES, dtype=F32)[:, None] * TILE + tok[None, :]).reshape(SEQ) + 0.0
    cosf, sinf = _rope_tables(pos)
    rope_spec = pl.BlockSpec((TILE, LANES), lambda s: (s % TILES, 0))
    return dict(
        mixer=_ret_prompt_kernel,
        in_specs=[_tile_spec(2), _tile_spec(3), _tile_spec(0), _tile_spec(1), rope_spec, rope_spec]
                 + _ret_table_specs(TILE),
        args=[proj_b, proj_b, proj_c, proj_c, cosf, sinf, *tabs, gn_g.reshape(1, -1)],
        out_specs=[_tile_spec(0), _state_spec(RET_HEADS, RET_DK, RET_DV)],
        out_shape=[jax.ShapeDtypeStruct((N_PROMPT, 1024), BF16),
                   jax.ShapeDtypeStruct((BATCH, RET_HEADS, RET_DK, RET_DV), F32)])


def _ret_sample_call(proj_b, proj_c, gn_g, r0, layer, r_all):
    rows = jnp.arange(RET_SROWS)
    tabs = _ret_tables(rows % RET_SB, (rows // RET_SB).astype(F32))
    cosf, sinf = _rope_tables((rows // RET_SB).astype(F32) + float(PAST_LEN))
    pb3, pc3 = _sample3(proj_b), _sample3(proj_c)
    st_block = (None, RET_SB, RET_HEADS, RET_DK, RET_DV)
    st_spec = pl.BlockSpec(st_block, lambda i: (layer, i, 0, 0, 0))
    in_specs = [_sample_spec(RET_SB, 2), _sample_spec(RET_SB, 3), _sample_spec(RET_SB, 0), _sample_spec(RET_SB, 1),
                st_spec, _const_spec((RET_SROWS, LANES)), _const_spec((RET_SROWS, LANES))] \
        + _ret_table_specs(RET_SROWS)
    args = [pb3, pb3, pc3, pc3, r0, cosf, sinf, *tabs, gn_g.reshape(1, -1)]
    aliases = {}
    if r_all is not None:
        in_specs.append(pl.BlockSpec(memory_space=pl.ANY))
        args.append(r_all)
        aliases = {len(args) - 1: 1}
    ys, r_all = pl.pallas_call(
        _ret_sample_kernel, grid=(DEC_BATCH // RET_SB,),
        in_specs=in_specs,
        out_specs=[pl.BlockSpec((DEC_SEQ, RET_SB, 1024), lambda i: (0, i, 0)), st_spec],
        out_shape=[jax.ShapeDtypeStruct((DEC_SEQ, DEC_BATCH, 1024), BF16),
                   jax.ShapeDtypeStruct((DEPTH, DEC_BATCH, RET_HEADS, RET_DK, RET_DV), F32)],
        input_output_aliases=aliases,
        compiler_params=_params(("parallel",)), name="ret_sample",
    )(*args)
    return ys.reshape(N_SAMPLE, 1024), r_all


def _merge_kernel(na, ysp, yss, ylp, yls, yrp, yrs, gs_ref, gl_ref, gr_ref, wb_ref, o_ref):
    acc = None
    for m, (a_ref, b_ref, g_ref) in enumerate(((ysp, yss, gs_ref), (ylp, yls, gl_ref), (yrp, yrs, gr_ref))):
        b = jnp.dot(_pick(na, a_ref, b_ref), wb_ref[m], preferred_element_type=F32)
        term = _sigmoid(g_ref[...].astype(F32)) * b
        acc = term if acc is None else acc + term
    o_ref[...] = acc.astype(o_ref.dtype)


def _merge_call(ys, yl, yr, proj_c, proj_d, wb, tm=512):
    specs, na = _two_part_specs(tm, 1024)
    gsp = lambda j: pl.BlockSpec((tm, D_MODEL), lambda i: (i, j))
    return pl.pallas_call(
        functools.partial(_merge_kernel, na), grid=(N_TOK // tm,),
        in_specs=specs * 3 + [gsp(1), gsp(0), gsp(1), _const_spec((3, 1024, D_MODEL))],
        out_specs=pl.BlockSpec((tm, D_MODEL), lambda i: (i, 0)),
        out_shape=jax.ShapeDtypeStruct((N_TOK, D_MODEL), BF16),
        compiler_params=_params(("parallel",)), name="merge",
    )(*ys, *yl, *yr, proj_c, proj_d, proj_d, wb)


def _outproj_kernel(na, m_ref, w_ref, x_ref, g_ref, oa_ref, ob_ref):
    x = x_ref[...] + jnp.dot(m_ref[...], w_ref[...], preferred_element_type=F32)
    ms = jnp.mean(x * x, axis=-1, keepdims=True)
    xn = x * lax.rsqrt(ms + NORM_EPS) * g_ref[...]
    if na is None:
        oa_ref[...] = x
        ob_ref[...] = xn.astype(ob_ref.dtype)
    else:
        @pl.when(pl.program_id(0) < na)
        def _():
            oa_ref[...] = xn

        @pl.when(pl.program_id(0) >= na)
        def _():
            ob_ref[...] = xn


def _outproj_call(merged, w_out, x, g_next, final, tm=512):
    tok = pl.BlockSpec((tm, D_MODEL), lambda i: (i, 0))
    if final:
        out_specs, na = _two_part_specs(tm, D_MODEL)
        out_shape = [jax.ShapeDtypeStruct((N_PROMPT, D_MODEL), F32), jax.ShapeDtypeStruct((N_SAMPLE, D_MODEL), F32)]
    else:
        out_specs, na = [tok, tok], None
        out_shape = [jax.ShapeDtypeStruct((N_TOK, D_MODEL), F32), jax.ShapeDtypeStruct((N_TOK, D_MODEL), BF16)]
    return pl.pallas_call(
        functools.partial(_outproj_kernel, na), grid=(N_TOK // tm,),
        in_specs=[tok, _const_spec((D_MODEL, D_MODEL)), tok, _const_spec((1, D_MODEL))],
        out_specs=out_specs, out_shape=out_shape,
        compiler_params=_params(("arbitrary",)), name="outproj",
    )(merged, w_out, x, g_next.reshape(1, D_MODEL))


def kernel(x_prompt, x_sample, state_s5_re, state_s5_im, state_lru, state_conv, state_ret, norm_g, w_in, s5_lambda_re, s5_lambda_im, s5_log_dt, s5_b_re, s5_b_im, s5_c_re, s5_c_im, s5_d, s5_w_glu, s5_b_glu, lru_conv_w, lru_conv_b, lru_w_a, lru_b_a, lru_w_x, lru_b_x, lru_lambda, ret_gn_g, w_branch_s5, w_branch_lru, w_branch_ret, w_out, final_norm_g):
    x, xn = _rmsnorm_call(*_to_rows(x_prompt, x_sample), norm_g[0])
    outs_p = [[] for _ in range(5)]
    outs_s = [[] for _ in range(4)]
    r_s = None
    for l in range(DEPTH):
        s5_w = (_s5_prep(s5_lambda_re[l], s5_lambda_im[l], s5_log_dt[l], s5_b_re[l], s5_b_im[l],
                         s5_c_re[l], s5_c_im[l]), s5_d[l], s5_w_glu[l].astype(BF16), s5_b_glu[l])
        wg = jnp.concatenate([lru_w_a[l], lru_w_x[l]], axis=-1).astype(BF16)
        lru_w = _lru_weights(lru_conv_w[l], lru_conv_b[l], wg, lru_b_a[l], lru_b_x[l], lru_lambda[l])
        cbuf = state_conv[l].transpose(1, 0, 2).reshape((CONV_WIDTH - 1) * DEC_BATCH, LRU_WIDTH)

        _, proj_a = _inproj_call("inproj_a", xn, w_in, l, 0)
        (ys_p, hr_p, hi_p), proj_b = _inproj_call("inproj_b_s5", xn, w_in, l, 1, **_s5_prompt_mixer(proj_a, *s5_w))
        ys_s, hr_s, hi_s = _s5_sample_call(proj_a, *s5_w, state_s5_re[l].reshape(DEC_BATCH, S5_LANES),
                                           state_s5_im[l].reshape(DEC_BATCH, S5_LANES))
        (yl_p, hl_p, conv_p), proj_c = _inproj_call("inproj_c_lru", xn, w_in, l, 2,
                                                    **_lru_prompt_mixer(proj_b, lru_w))
        conv_p = _lru_prompt_conv_state(conv_p)
        yl_s, hl_s, conv_s = _lru_sample_call(proj_b, lru_w, state_lru[l], cbuf)
        (yr_p, r_p), proj_d = _inproj_call("inproj_d_ret", xn, w_in, l, 3,
                                           **_ret_prompt_mixer(proj_b, proj_c, ret_gn_g[l]))
        yr_s, r_s = _ret_sample_call(proj_b, proj_c, ret_gn_g[l], state_ret, l, r_s)

        wb = jnp.stack([w_branch_s5[l], w_branch_lru[l], w_branch_ret[l]]).astype(BF16)
        merged = _merge_call((ys_p, ys_s), (yl_p, yl_s), (yr_p, yr_s), proj_c, proj_d, wb)
        final = l == DEPTH - 1
        g_next = final_norm_g if final else norm_g[l + 1]
        x, xn = _outproj_call(merged, w_out[l].astype(BF16), x, g_next, final)

        st = (S5_GROUPS, S5_STATE)
        for lst, vals in ((outs_p, (hr_p.reshape(BATCH, *st), hi_p.reshape(BATCH, *st),
                                    hl_p.reshape(BATCH, LRU_WIDTH), conv_p, r_p)),
                          (outs_s, (hr_s.reshape(DEC_BATCH, *st), hi_s.reshape(DEC_BATCH, *st), hl_s, conv_s))):
            for j, v in enumerate(vals):
                lst[j].append(v)

    y_prompt, y_sample = _from_rows(x, xn)
    sp = [jnp.stack(t, axis=0) for t in outs_p]
    ss = [jnp.stack(t, axis=0) for t in outs_s]
    return (y_prompt, y_sample, *sp, *ss, r_s)
```

```python
import functools

import jax
import jax.numpy as jnp
from jax import lax
from jax.experimental import pallas as pl
from jax.experimental.pallas import tpu as pltpu

F32 = jnp.float32
BF16 = jnp.bfloat16

D_MODEL = 2048
BATCH = 4
SEQ = 2048
DEPTH = 2
DEC_BATCH = 128
DEC_SEQ = 8
PAST_LEN = 16384
S5_WIDTH = 1024
S5_GROUP = 16
S5_GROUPS = 64
S5_STATE = 64
S5_LANES = S5_GROUPS * S5_STATE
LRU_WIDTH = 1024
LRU_BLOCKS = 8
LRU_BLOCK = 128
CONV_WIDTH = 4
LRU_C = 8.0
RET_HEADS = 8
RET_DK = 128
RET_DV = 128
RET_WIDTH = 1024
ROPE_BASE = 10000.0
NORM_EPS = 1e-6
GN_EPS = 1e-5
N_IN = 14336

N_PROMPT = BATCH * SEQ
N_SAMPLE = DEC_BATCH * DEC_SEQ
N_TOK = N_PROMPT + N_SAMPLE

SUBLANES = 8
LANES = 128
VMEM_LIMIT = 56 * 1024 * 1024

TILE = 256
SEG = TILE // SUBLANES
TILES = SEQ // TILE

PROJ_GROUPS = ((0, 2), (2, 4), (6, 4), (10, 4))
MM_COLS = 1024
MM_UNITS = TILES
MM_ROWS = N_TOK // MM_UNITS
MM_COL_PIECES = 2
MM_PIECE_COLS = MM_COLS // MM_COL_PIECES
MM_ROW_PIECES = 4
MM_PIECE_ROWS = MM_ROWS // MM_ROW_PIECES
MM_PIECES = MM_COL_PIECES * MM_ROW_PIECES

S5_KB = 4
S5_KW = S5_WIDTH // S5_KB
S5_NW = S5_LANES // S5_KB
SCAN_W = 512

SAMPLE_ROW0 = N_PROMPT // DEC_BATCH


def _params(sem, vmem=VMEM_LIMIT):
    return pltpu.CompilerParams(dimension_semantics=sem, vmem_limit_bytes=vmem)


def _const_spec(shape, single=False):
    return pl.BlockSpec(shape, lambda *_: (0,) * len(shape), pipeline_mode=pl.Buffered(1) if single else None)


def _sigmoid(x):
    return jax.nn.sigmoid(x)


def _silu(x):
    return x * jax.nn.sigmoid(x)


def _bcast_row(x, row):
    return jnp.broadcast_to(x[row:row + 1, :], x.shape)


def _cmul(ar, ai, br, bi):
    return ar * br - ai * bi, ar * bi + ai * br


def _to_rows(x_prompt, x_sample):
    xp = x_prompt.reshape(BATCH, TILES, SUBLANES, SEG, -1).transpose(0, 1, 3, 2, 4).reshape(N_PROMPT, -1)
    xs = x_sample.transpose(1, 0, 2).reshape(N_SAMPLE, -1)
    return xp, xs


def _from_rows(yp, ys):
    yp = yp.reshape(BATCH, TILES, SEG, SUBLANES, -1).transpose(0, 1, 3, 2, 4).reshape(BATCH, SEQ, -1)
    ys = ys.reshape(DEC_SEQ, DEC_BATCH, -1).transpose(1, 0, 2)
    return yp, ys


def _two_part_specs(tm, width):
    na = N_PROMPT // tm
    return [pl.BlockSpec((tm, width), lambda i: (jnp.minimum(i, na - 1), 0)),
            pl.BlockSpec((tm, width), lambda i: (jnp.maximum(i - na, 0), 0))], na


def _pick(na, a_ref, b_ref):
    return jnp.where(pl.program_id(0) < na, a_ref[...], b_ref[...])


def _norm_kernel(na, xa_ref, xb_ref, g_ref, o_ref):
    x = _pick(na, xa_ref, xb_ref)
    ms = jnp.mean(x * x, axis=-1, keepdims=True)
    o_ref[...] = (x * lax.rsqrt(ms + NORM_EPS) * g_ref[...]).astype(o_ref.dtype)


def _rmsnorm_call(xp, xs, g, tm=512):
    specs, na = _two_part_specs(tm, D_MODEL)
    return pl.pallas_call(
        functools.partial(_norm_kernel, na),
        grid=(N_TOK // tm,),
        in_specs=specs + [_const_spec((1, D_MODEL))],
        out_specs=pl.BlockSpec((tm, D_MODEL), lambda i: (i, 0)),
        out_shape=jax.ShapeDtypeStruct((N_TOK, D_MODEL), BF16),
        compiler_params=_params(("parallel",)),
        name="rmsnorm",
    )(xp, xs, g.reshape(1, D_MODEL))


def _cast_kernel(w_ref, o_ref):
    o_ref[...] = w_ref[...].astype(o_ref.dtype)


def _cast_bf16(w, layer, tr=512):
    _, rows, cols = w.shape
    return pl.pallas_call(
        _cast_kernel, grid=(rows // tr,),
        in_specs=[pl.BlockSpec((None, tr, cols), lambda i: (layer, i, 0))],
        out_specs=pl.BlockSpec((tr, cols), lambda i: (i, 0)),
        out_shape=jax.ShapeDtypeStruct((rows, cols), BF16),
        compiler_params=_params(("parallel",)), name="cast_bf16",
    )(w)


def _inproj_kernel(mixer, n_in, n_out, *refs):
    xn_ref, w_ref = refs[n_in:n_in + 2]
    o_ref = refs[n_in + 2 + n_out]
    wbf_ref = refs[n_in + 3 + n_out]

    @pl.when(pl.program_id(0) % MM_UNITS == 0)
    def _():
        wbf_ref[...] = w_ref[...].astype(BF16)

    def piece(k):
        rows = slice((k // MM_COL_PIECES) * MM_PIECE_ROWS, (k // MM_COL_PIECES + 1) * MM_PIECE_ROWS)
        cols = slice((k % MM_COL_PIECES) * MM_PIECE_COLS, (k % MM_COL_PIECES + 1) * MM_PIECE_COLS)
        o_ref[rows, cols] = jnp.dot(xn_ref[rows, :], wbf_ref[:, cols],
                                    preferred_element_type=F32).astype(o_ref.dtype)

    if mixer is None:
        for k in range(MM_PIECES):
            piece(k)
    else:
        mixer(piece, *refs[:n_in], *refs[n_in + 2:n_in + 2 + n_out], *refs[n_in + 4 + n_out:])


def _inproj_call(name, xn, w_in, layer, group, mixer=None, in_specs=(), args=(), out_specs=(), out_shape=(),
                 scratch=()):
    first, tiles = PROJ_GROUPS[group]
    mm_in = [pl.BlockSpec((MM_ROWS, D_MODEL), lambda s: (s % MM_UNITS, 0)),
             pl.BlockSpec((None, D_MODEL, MM_COLS), lambda s: (layer, 0, first + s // MM_UNITS))]
    mm_out = pl.BlockSpec((MM_ROWS, MM_COLS), lambda s: (s % MM_UNITS, s // MM_UNITS))
    res = pl.pallas_call(
        functools.partial(_inproj_kernel, mixer, len(in_specs), len(out_specs)),
        grid=(tiles * MM_UNITS,),
        in_specs=list(in_specs) + mm_in,
        out_specs=list(out_specs) + [mm_out],
        out_shape=list(out_shape) + [jax.ShapeDtypeStruct((N_TOK, tiles * MM_COLS), BF16)],
        scratch_shapes=[pltpu.VMEM((D_MODEL, MM_COLS), BF16)] + list(scratch),
        compiler_params=_params(("arbitrary",)),
        name=name,
    )(*args, xn, w_in)
    return res[:-1], res[-1]


def _s5_prep(lam_re, lam_im, log_dt, b_re, b_im, c_re, c_im):
    dt = jnp.exp(log_dt)[:, None]
    e = jnp.exp(lam_re * dt)
    lbr = e * jnp.cos(lam_im * dt)
    lbi = e * jnp.sin(lam_im * dt)
    nr, ni = lbr - 1.0, lbi
    den = lam_re * lam_re + lam_im * lam_im
    cr = (nr * lam_re + ni * lam_im) / den
    ci = (ni * lam_re - nr * lam_im) / den
    bbr = cr[..., None] * b_re - ci[..., None] * b_im
    bbi = cr[..., None] * b_im + ci[..., None] * b_re
    gpb = S5_GROUPS // S5_KB

    hsel = jnp.arange(gpb)
    t = jnp.stack([bbr, bbi]).astype(BF16).reshape(2, S5_KB, gpb, S5_STATE, S5_GROUP)
    t = t.transpose(1, 2, 4, 0, 3).reshape(S5_KB, S5_KW, 2, 1, S5_STATE)
    same = (jnp.arange(S5_KW)[:, None] // S5_GROUP) == hsel[None, :]
    b_blk = jnp.where(same[None, :, None, :, None], t, 0).reshape(S5_KB, S5_KW, 2 * S5_NW)
    t = jnp.stack([c_re, -c_im]).astype(BF16).reshape(2, S5_KB, gpb, S5_GROUP, S5_STATE)
    t = t.transpose(1, 0, 2, 4, 3).reshape(S5_KB, 2 * S5_NW, 1, S5_GROUP)
    same = ((jnp.arange(2 * S5_NW)[:, None] % S5_NW) // S5_STATE) == hsel[None, :]
    c_blk = jnp.where(same[None, :, :, None], t, 0).reshape(S5_KB, 2 * S5_NW, S5_KW)

    lr, li = lbr.reshape(-1), lbi.reshape(-1)
    full = lambda v: jnp.broadcast_to(v, (SUBLANES, S5_LANES))
    lam = jnp.stack([full(lr), full(li)])
    sr, si = lr, li
    for _ in range(SEG.bit_length() - 1):
        sr, si = _cmul(sr, si, sr, si)
    pr, pi = [sr], [si]
    for _ in range(SUBLANES - 1):
        r_, i_ = _cmul(pr[-1], pi[-1], sr, si)
        pr.append(r_)
        pi.append(i_)
    row = jnp.arange(SUBLANES)[:, None]
    ak = jnp.stack([jnp.stack([jnp.where(row >= k, pr[k - 1][None, :], 0.0),
                               jnp.where(row >= k, pi[k - 1][None, :], 0.0)]) for k in (1, 2, 4)])
    pw = jnp.stack([jnp.stack(pr), jnp.stack(pi)])
    return b_blk, c_blk, lam, ak, pw


def _s5_drive_block(kb, u, bblk_ref, scr):
    res = jnp.dot(u[:, kb * S5_KW:(kb + 1) * S5_KW], bblk_ref[kb], preferred_element_type=F32)
    scr[:, kb * S5_NW:(kb + 1) * S5_NW] = res[:, :S5_NW]
    scr[:, S5_LANES + kb * S5_NW:S5_LANES + (kb + 1) * S5_NW] = res[:, S5_NW:]


def _s5_readout_block(kb, scr, cblk_ref):
    hcat = jnp.concatenate(
        [scr[:, kb * S5_NW:(kb + 1) * S5_NW],
         scr[:, S5_LANES + kb * S5_NW:S5_LANES + (kb + 1) * S5_NW]], axis=1).astype(BF16)
    return jnp.dot(hcat, cblk_ref[kb], preferred_element_type=F32)


def _s5_finish(parts, u, z, d_ref, wglu_ref, bglu_ref):
    y = jnp.concatenate(parts, axis=1) + d_ref[...] * u.astype(F32)
    y = jax.nn.gelu(y, approximate=True)
    glu = jnp.dot(y.astype(BF16), wglu_ref[...], preferred_element_type=F32) + bglu_ref[...]
    y = y * _sigmoid(glu)
    return (y * _silu(z.astype(F32))).astype(BF16)


def _s5_prompt_kernel(mm_piece, u_ref, z_ref, bblk_ref, cblk_ref, lam_ref, ak_ref, pw_ref, d_ref, wglu_ref,
                      bglu_ref, y_ref, hr_ref, hi_ref, scr, car):
    t = pl.program_id(0) % TILES

    @pl.when(t == 0)
    def _():
        car[...] = jnp.zeros_like(car)

    w = SCAN_W
    rowi = lax.broadcasted_iota(jnp.int32, (SUBLANES, w), 0)
    u = u_ref[...]
    parts = []
    for kb in range(S5_KB):
        _s5_drive_block(kb, u, bblk_ref, scr)
        for c in range(kb * (S5_NW // w), (kb + 1) * (S5_NW // w)):
            mm_piece(c)
            sl_re = slice(c * w, (c + 1) * w)
            sl_im = slice(S5_LANES + c * w, S5_LANES + (c + 1) * w)
            lr, li = lam_ref[0, :, sl_re], lam_ref[1, :, sl_re]

            xr = xi = jnp.zeros((SUBLANES, w), F32)
            for r in range(SEG):
                rows = slice(r * SUBLANES, (r + 1) * SUBLANES)
                pr, pi = _cmul(lr, li, xr, xi)
                xr = scr[rows, sl_re] + pr
                xi = scr[rows, sl_im] + pi
                scr[rows, sl_re] = xr
                scr[rows, sl_im] = xi

            for k, idx in zip((1, 2, 4), range(3)):
                pr, pi = _cmul(ak_ref[idx, 0, :, sl_re], ak_ref[idx, 1, :, sl_re],
                               pltpu.roll(xr, k, 0), pltpu.roll(xi, k, 0))
                xr, xi = xr + pr, xi + pi
            cr, ci = car[0, :, sl_re], car[1, :, sl_re]
            pr, pi = _cmul(pw_ref[0, :, sl_re], pw_ref[1, :, sl_re], cr, ci)
            fr, fi = xr + pr, xi + pi
            dr = jnp.where(rowi == 0, cr, pltpu.roll(fr, 1, 0))
            di = jnp.where(rowi == 0, ci, pltpu.roll(fi, 1, 0))
            car[0, :, sl_re] = _bcast_row(fr, SUBLANES - 1)
            car[1, :, sl_re] = _bcast_row(fi, SUBLANES - 1)

            for r in range(SEG):
                rows = slice(r * SUBLANES, (r + 1) * SUBLANES)
                dr, di = _cmul(lr, li, dr, di)
                scr[rows, sl_re] = scr[rows, sl_re] + dr
                scr[rows, sl_im] = scr[rows, sl_im] + di
        parts.append(_s5_readout_block(kb, scr, cblk_ref))

    y_ref[...] = _s5_finish(parts, u, z_ref[...], d_ref, wglu_ref, bglu_ref)

    @pl.when(t == TILES - 1)
    def _():
        hr_ref[0] = car[0, 0:1, :]
        hi_ref[0] = car[1, 0:1, :]


S5_SB = 32


def _s5_sample_kernel(u_ref, z_ref, h0r_ref, h0i_ref, bblk_ref, cblk_ref, lam_ref, d_ref, wglu_ref, bglu_ref,
                      y_ref, hr_ref, hi_ref, scr):
    rows = DEC_SEQ * S5_SB
    u = u_ref[...].reshape(rows, S5_WIDTH)
    w = SCAN_W
    parts = []
    for kb in range(S5_KB):
        _s5_drive_block(kb, u, bblk_ref, scr)
        for c in range(kb * (S5_NW // w), (kb + 1) * (S5_NW // w)):
            sl_re = slice(c * w, (c + 1) * w)
            sl_im = slice(S5_LANES + c * w, S5_LANES + (c + 1) * w)
            lr, li = lam_ref[0, :, sl_re], lam_ref[1, :, sl_re]
            for g in range(S5_SB // SUBLANES):
                seqs = slice(g * SUBLANES, (g + 1) * SUBLANES)
                hr, hi = h0r_ref[seqs, sl_re], h0i_ref[seqs, sl_re]
                for j in range(DEC_SEQ):
                    rows_j = slice(j * S5_SB + g * SUBLANES, j * S5_SB + (g + 1) * SUBLANES)
                    pr, pi = _cmul(lr, li, hr, hi)
                    hr = scr[rows_j, sl_re] + pr
                    hi = scr[rows_j, sl_im] + pi
                    scr[rows_j, sl_re] = hr
                    scr[rows_j, sl_im] = hi
                hr_ref[seqs, sl_re] = hr
                hi_ref[seqs, sl_re] = hi
        parts.append(_s5_readout_block(kb, scr, cblk_ref))

    y = _s5_finish(parts, u, z_ref[...].reshape(rows, S5_WIDTH), d_ref, wglu_ref, bglu_ref)
    y_ref[...] = y.reshape(DEC_SEQ, S5_SB, S5_WIDTH)


def _tile_spec(col):
    return pl.BlockSpec((TILE, 1024), lambda s: (s, col))


def _state_spec(*shape):
    return pl.BlockSpec((1,) + shape, lambda s: (s // TILES,) + (0,) * len(shape))


def _sample3(proj):
    return proj.reshape(N_TOK // DEC_BATCH, DEC_BATCH, proj.shape[-1])


def _sample_spec(nseq, col):
    return pl.BlockSpec((DEC_SEQ, nseq, 1024), lambda i: (SAMPLE_ROW0 // DEC_SEQ, i, col))


def _s5_specs(prep, d, wglu, bglu, single):
    b_blk, c_blk, lam, ak, pw = prep
    spec = functools.partial(_const_spec, single=single)
    lam_spec = spec((2, SUBLANES, S5_LANES))
    mm = ([spec((S5_KB, S5_KW, 2 * S5_NW)), spec((S5_KB, 2 * S5_NW, S5_KW))], [b_blk, c_blk])
    chain = ([spec((3, 2, SUBLANES, S5_LANES)), lam_spec], [ak, pw])
    tail = ([spec((1, S5_WIDTH)), spec((S5_WIDTH, S5_WIDTH)), spec((1, S5_WIDTH))],
            [d.reshape(1, -1), wglu, bglu.reshape(1, -1)])
    return mm, (lam_spec, lam), chain, tail


def _s5_prompt_mixer(proj_a, prep, d, wglu, bglu):
    mm, (lam_spec, lam), chain, tail = _s5_specs(prep, d, wglu, bglu, single=True)
    return dict(
        mixer=_s5_prompt_kernel,
        in_specs=[_tile_spec(0), _tile_spec(1)] + mm[0] + [lam_spec] + chain[0] + tail[0],
        args=[proj_a, proj_a] + mm[1] + [lam] + chain[1] + tail[1],
        out_specs=[_tile_spec(0), _state_spec(1, S5_LANES), _state_spec(1, S5_LANES)],
        out_shape=[jax.ShapeDtypeStruct((N_PROMPT, 1024), BF16),
                   jax.ShapeDtypeStruct((BATCH, 1, S5_LANES), F32),
                   jax.ShapeDtypeStruct((BATCH, 1, S5_LANES), F32)],
        scratch=[pltpu.VMEM((TILE, 2 * S5_LANES), F32), pltpu.VMEM((2, SUBLANES, S5_LANES), F32)])


def _s5_sample_call(proj_a, prep, d, wglu, bglu, h0r, h0i):
    mm, (lam_spec, lam), _, tail = _s5_specs(prep, d, wglu, bglu, single=False)
    proj3 = _sample3(proj_a)
    st = pl.BlockSpec((S5_SB, S5_LANES), lambda i: (i, 0))
    ys, hr_s, hi_s = pl.pallas_call(
        _s5_sample_kernel, grid=(DEC_BATCH // S5_SB,),
        in_specs=[_sample_spec(S5_SB, 0), _sample_spec(S5_SB, 1), st, st] + mm[0] + [lam_spec] + tail[0],
        out_specs=[pl.BlockSpec((DEC_SEQ, S5_SB, 1024), lambda i: (0, i, 0)), st, st],
        out_shape=[jax.ShapeDtypeStruct((DEC_SEQ, DEC_BATCH, 1024), BF16),
                   jax.ShapeDtypeStruct((DEC_BATCH, S5_LANES), F32),
                   jax.ShapeDtypeStruct((DEC_BATCH, S5_LANES), F32)],
        scratch_shapes=[pltpu.VMEM((DEC_SEQ * S5_SB, 2 * S5_LANES), F32)],
        compiler_params=_params(("parallel",)), name="s5_sample",
    )(proj3, proj3, h0r, h0i, *mm[1], lam, *tail[1])
    return ys.reshape(N_SAMPLE, 1024), hr_s, hi_s


def _lru_gates_block(n, xc, wg_ref, ba_ref, bx_ref, lam_ref):
    sl = slice(n * LRU_BLOCK, (n + 1) * LRU_BLOCK)
    xb = xc[:, sl]
    res = jnp.dot(xb.astype(BF16), wg_ref[n], preferred_element_type=F32)
    r_gate = _sigmoid(res[:, :LRU_BLOCK] + ba_ref[:, sl])
    i_gate = _sigmoid(res[:, LRU_BLOCK:] + bx_ref[:, sl])
    nl = -lam_ref[:, sl]
    softplus = jnp.maximum(nl, 0.0) + jnp.log1p(jnp.exp(-jnp.abs(nl)))
    a = jnp.exp(-LRU_C * r_gate * softplus)
    return a, jnp.sqrt(1.0 - a * a) * (i_gate * xb)


def _lru_gates(xc, wg_ref, ba_ref, bx_ref, lam_ref):
    ab = [_lru_gates_block(n, xc, wg_ref, ba_ref, bx_ref, lam_ref) for n in range(LRU_BLOCKS)]
    return jnp.concatenate([a for a, _ in ab], axis=1), jnp.concatenate([b for _, b in ab], axis=1)


def _lru_prompt_kernel(mm_piece, u_ref, z_ref, cw_ref, cb_ref, wg_ref, ba_ref, bx_ref, lam_ref,
                       y_ref, hl_ref, cv_ref, a_scr, b_scr, tail, car):
    t = pl.program_id(0) % TILES
    nwrap = CONV_WIDTH - 1

    @pl.when(t == 0)
    def _():
        tail[...] = jnp.zeros_like(tail)
        car[...] = jnp.zeros_like(car)

    x = u_ref[...].astype(F32)
    rowi = lax.broadcasted_iota(jnp.int32, (SUBLANES, LRU_WIDTH), 0)
    wrap = []
    for k in range(nwrap):
        cur = x[TILE - (nwrap - k) * SUBLANES:TILE - (nwrap - k - 1) * SUBLANES, :]
        prev = tail[k * SUBLANES:(k + 1) * SUBLANES, :]
        wrap.append(jnp.where(rowi == 0, pltpu.roll(prev, 1, 0), pltpu.roll(cur, 1, 0)))
    xc = cw_ref[nwrap:nwrap + 1, :] * x + cb_ref[...]
    for s in range(1, CONV_WIDTH):
        shifted = jnp.concatenate(wrap[nwrap - s:] + [x[:TILE - s * SUBLANES, :]], axis=0)
        xc = xc + cw_ref[nwrap - s:nwrap - s + 1, :] * shifted
    tail[...] = x[TILE - nwrap * SUBLANES:, :]

    for n in range(LRU_BLOCKS):
        mm_piece(n)
        sl = slice(n * LRU_BLOCK, (n + 1) * LRU_BLOCK)
        a_scr[:, sl], b_scr[:, sl] = _lru_gates_block(n, xc, wg_ref, ba_ref, bx_ref, lam_ref)

    w = SCAN_W
    rw = lax.broadcasted_iota(jnp.int32, (SUBLANES, w), 0)
    for c in range(LRU_WIDTH // w):
        sl = slice(c * w, (c + 1) * w)

        bv = jnp.zeros((SUBLANES, w), F32)
        av = jnp.ones((SUBLANES, w), F32)
        for r in range(SEG):
            rows = slice(r * SUBLANES, (r + 1) * SUBLANES)
            ar = a_scr[rows, sl]
            bv = ar * bv + b_scr[rows, sl]
            av = ar * av
            b_scr[rows, sl] = bv
            a_scr[rows, sl] = av

        for k in (1, 2, 4):
            sa = jnp.where(rw >= k, pltpu.roll(av, k, 0), 1.0)
            sb = jnp.where(rw >= k, pltpu.roll(bv, k, 0), 0.0)
            bv = bv + av * sb
            av = av * sa
        cv = car[:, sl]
        full = bv + av * cv
        enter = jnp.where(rw == 0, cv, pltpu.roll(full, 1, 0))
        car[:, sl] = _bcast_row(full, SUBLANES - 1)

        for r in range(SEG):
            rows = slice(r * SUBLANES, (r + 1) * SUBLANES)
            b_scr[rows, sl] = b_scr[rows, sl] + a_scr[rows, sl] * enter

    y_ref[...] = (b_scr[...] * _silu(z_ref[...].astype(F32))).astype(y_ref.dtype)

    @pl.when(t == TILES - 1)
    def _():
        hl_ref[0] = car[0:1, :]
        cv_ref[0] = tail[...]


def _lru_sample_kernel(u_ref, z_ref, h0_ref, cbuf_ref, cw_ref, cb_ref, wg_ref, ba_ref, bx_ref, lam_ref,
                       y_ref, hl_ref, cv_ref, ext):
    nb = DEC_BATCH
    nwrap = CONV_WIDTH - 1
    ext[0:nwrap * nb, :] = cbuf_ref[...]
    ext[nwrap * nb:, :] = u_ref[...].astype(F32)
    h = h0_ref[...]
    for j in range(DEC_SEQ):
        xc = cb_ref[...]
        for k in range(CONV_WIDTH):
            xc = xc + cw_ref[k:k + 1, :] * ext[(j + k) * nb:(j + k + 1) * nb, :]
        a, b = _lru_gates(xc, wg_ref, ba_ref, bx_ref, lam_ref)
        h = a * h + b
        y_ref[j * nb:(j + 1) * nb, :] = (h * _silu(z_ref[j * nb:(j + 1) * nb, :].astype(F32))).astype(y_ref.dtype)
    hl_ref[...] = h
    cv_ref[...] = ext[DEC_SEQ * nb:, :]


def _lru_weights(conv_w, conv_b, wg, b_a, b_x, lam):
    specs = [_const_spec((CONV_WIDTH, LRU_WIDTH)), _const_spec((1, LRU_WIDTH)),
             _const_spec((LRU_BLOCKS, LRU_BLOCK, 2 * LRU_BLOCK)),
             _const_spec((1, LRU_WIDTH)), _const_spec((1, LRU_WIDTH)), _const_spec((1, LRU_WIDTH))]
    args = [conv_w, conv_b.reshape(1, -1), wg, b_a.reshape(1, -1), b_x.reshape(1, -1), lam.reshape(1, -1)]
    return specs, args


def _lru_prompt_mixer(proj_b, weights):
    w_specs, w_args = weights
    nrows = (CONV_WIDTH - 1) * SUBLANES
    return dict(
        mixer=_lru_prompt_kernel,
        in_specs=[_tile_spec(0), _tile_spec(1)] + w_specs,
        args=[proj_b, proj_b] + w_args,
        out_specs=[_tile_spec(0), _state_spec(1, LRU_WIDTH), _state_spec(nrows, LRU_WIDTH)],
        out_shape=[jax.ShapeDtypeStruct((N_PROMPT, 1024), BF16),
                   jax.ShapeDtypeStruct((BATCH, 1, LRU_WIDTH), F32),
                   jax.ShapeDtypeStruct((BATCH, nrows, LRU_WIDTH), F32)],
        scratch=[pltpu.VMEM((TILE, LRU_WIDTH), F32), pltpu.VMEM((TILE, LRU_WIDTH), F32),
                 pltpu.VMEM((nrows, LRU_WIDTH), F32), pltpu.VMEM((SUBLANES, LRU_WIDTH), F32)])


def _lru_prompt_conv_state(cv_p):
    return cv_p.reshape(BATCH, CONV_WIDTH - 1, SUBLANES, LRU_WIDTH)[:, :, SUBLANES - 1, :]


def _lru_sample_call(proj_b, weights, h0, cbuf):
    w_specs, w_args = weights
    srow = N_PROMPT // N_SAMPLE
    ys, hl_s, cv_s = pl.pallas_call(
        _lru_sample_kernel, grid=(1,),
        in_specs=[pl.BlockSpec((N_SAMPLE, 1024), lambda i: (srow, 0)),
                  pl.BlockSpec((N_SAMPLE, 1024), lambda i: (srow, 1)),
                  _const_spec((DEC_BATCH, LRU_WIDTH)),
                  _const_spec(((CONV_WIDTH - 1) * DEC_BATCH, LRU_WIDTH))] + w_specs,
        out_specs=[_const_spec((N_SAMPLE, 1024)), _const_spec((DEC_BATCH, LRU_WIDTH)),
                   _const_spec(((CONV_WIDTH - 1) * DEC_BATCH, LRU_WIDTH))],
        out_shape=[jax.ShapeDtypeStruct((N_SAMPLE, 1024), BF16),
                   jax.ShapeDtypeStruct((DEC_BATCH, LRU_WIDTH), F32),
                   jax.ShapeDtypeStruct(((CONV_WIDTH - 1) * DEC_BATCH, LRU_WIDTH), F32)],
        scratch_shapes=[pltpu.VMEM(((CONV_WIDTH - 1) * DEC_BATCH + N_SAMPLE, LRU_WIDTH), F32)],
        compiler_params=_params(("arbitrary",)), name="lru_sample",
    )(proj_b, proj_b, h0, cbuf, *w_args)
    return ys, hl_s, cv_s.reshape(CONV_WIDTH - 1, DEC_BATCH, LRU_WIDTH).transpose(1, 0, 2)


RET_SB = 16
RET_SROWS = RET_SB * DEC_SEQ


def _ret_tables(seq, idx):
    n = idx.shape[0]
    chunk = jnp.max(idx) + 1.0
    log_g = jnp.log1p(-jnp.exp2(-5.0 - jnp.arange(RET_HEADS, dtype=F32)))
    diff = idx[:, None] - idx[None, :]
    same = seq[:, None] == seq[None, :]
    dmask = jnp.where((diff[None] >= 0) & same[None],
                      jnp.exp(jnp.maximum(diff, 0.0)[None] * log_g[:, None, None]), 0.0)
    xi = jnp.exp((idx[None, :] + 1.0) * log_g[:, None])
    zeta = jnp.exp((chunk - 1.0 - idx[None, :]) * log_g[:, None])
    gch = jnp.exp(chunk * log_g)
    full = lambda t: jnp.broadcast_to(t[:, :, None], (RET_HEADS, n, LANES))
    gc = jnp.broadcast_to(gch[:, None, None], (RET_HEADS, SUBLANES, LANES))
    return dmask, full(xi), full(zeta), gc


def _rope_tables(pos):
    half = RET_DK // 2
    freq = ROPE_BASE ** (-jnp.arange(half, dtype=F32) / half)
    ang = pos[:, None] * freq[None, :]
    cos, sin = jnp.cos(ang), jnp.sin(ang)
    return jnp.concatenate([cos, cos], axis=-1), jnp.concatenate([-sin, sin], axis=-1)


def _rope(x, cosf, sinf):
    return x * cosf + pltpu.roll(x, RET_DK // 2, 1) * sinf


def _ret_head(h, q, k, v, cos, sin, dmask_ref, zeta_ref):
    sl = slice(h * RET_DK, (h + 1) * RET_DK)
    qb = _rope(q[:, sl].astype(F32), cos, sin).astype(BF16)
    kh = _rope(k[:, sl].astype(F32), cos, sin) * (RET_DK ** -0.5)
    vb = v[:, sl]
    sc = lax.dot_general(qb, kh.astype(BF16), (((1,), (1,)), ((), ())), preferred_element_type=F32)
    inner = jnp.dot((sc * dmask_ref[h]).astype(BF16), vb, preferred_element_type=F32)
    kz = (kh * zeta_ref[h]).astype(BF16)
    return sl, qb, kz, vb, inner


def _ret_finish(o, z, g):
    mu = jnp.mean(o, axis=-1, keepdims=True)
    oc = o - mu
    var = jnp.mean(oc * oc, axis=-1, keepdims=True)
    on = oc * lax.rsqrt(var + GN_EPS) * g
    return (on * _silu(z.astype(F32))).astype(BF16)


def _ret_prompt_kernel(mm_piece, q_ref, k_ref, v_ref, z_ref, cos_ref, sin_ref, dmask_ref, xi_ref, zeta_ref,
                       gc_ref, gng_ref, y_ref, r_ref):
    @pl.when(pl.program_id(0) % TILES == 0)
    def _():
        r_ref[...] = jnp.zeros_like(r_ref)

    cos, sin = cos_ref[...], sin_ref[...]
    for h in range(RET_HEADS):
        mm_piece(h)
        sl, qb, kz, vb, inner = _ret_head(h, q_ref, k_ref, v_ref, cos, sin, dmask_ref, zeta_ref)
        r = r_ref[0, h]
        cross = jnp.dot(qb, r.astype(BF16), preferred_element_type=F32) * xi_ref[h]
        upd = lax.dot_general(kz, vb, (((0,), (0,)), ((), ())), preferred_element_type=F32)
        r_ref[0, h] = r * gc_ref[h, 0:1, :] + upd
        y_ref[:, sl] = _ret_finish(inner + cross, z_ref[:, sl], gng_ref[:, sl])


def _ret_sample_kernel(q_ref, k_ref, v_ref, z_ref, r0_ref, cos_ref, sin_ref, dmask_ref, xi_ref,
                       zeta_ref, gc_ref, gng_ref, *rest):
    y_ref, r_ref = rest[-2:]
    cos, sin = cos_ref[...], sin_ref[...]
    q = q_ref[...].reshape(RET_SROWS, RET_WIDTH)
    k = k_ref[...].reshape(RET_SROWS, RET_WIDTH)
    v = v_ref[...].reshape(RET_SROWS, RET_WIDTH)
    z = z_ref[...].reshape(RET_SROWS, RET_WIDTH)
    rowseq = lax.broadcasted_iota(jnp.int32, (RET_SROWS, RET_DV), 0) % RET_SB
    outs = []
    for h in range(RET_HEADS):
        sl, qb, kz, vb, inner = _ret_head(h, q, k, v, cos, sin, dmask_ref, zeta_ref)
        rcat = jnp.concatenate([r0_ref[s, h] for s in range(RET_SB)], axis=1).astype(BF16)
        call = jnp.dot(qb, rcat, preferred_element_type=F32)
        cross = jnp.zeros((RET_SROWS, RET_DV), F32)
        for s in range(RET_SB):
            cross = jnp.where(rowseq == s, call[:, s * RET_DV:(s + 1) * RET_DV], cross)
        cross = cross * xi_ref[h]
        vf = vb.astype(F32)
        vexp = jnp.concatenate([jnp.where(rowseq == s, vf, 0.0) for s in range(RET_SB)],
                               axis=1).astype(BF16)
        upd = lax.dot_general(kz, vexp, (((0,), (0,)), ((), ())), preferred_element_type=F32)
        gc = gc_ref[h, 0:1, :]
        for s in range(RET_SB):
            r_ref[s, h] = r0_ref[s, h] * gc + upd[:, s * RET_DV:(s + 1) * RET_DV]
        outs.append(_ret_finish(inner + cross, z[:, sl], gng_ref[:, sl]))
    y_ref[...] = jnp.concatenate(outs, axis=1).reshape(DEC_SEQ, RET_SB, RET_WIDTH)


def _ret_table_specs(n):
    return [_const_spec((RET_HEADS, n, n)), _const_spec((RET_HEADS, n, LANES)),
            _const_spec((RET_HEADS, n, LANES)), _const_spec((RET_HEADS, SUBLANES, LANES)),
            _const_spec((1, RET_WIDTH))]


def _ret_prompt_mixer(proj_b, proj_c, gn_g):
    rows = jnp.arange(TILE)
    tok = ((rows % SUBLANES) * SEG + rows // SUBLANES).astype(F32)
    tabs = _ret_tables(jnp.zeros((TILE,), jnp.int32), tok)
    pos = (jnp.arange(TILES, dtype=F32)[:, None] * TILE + tok[None, :]).reshape(SEQ) + 0.0
    cosf, sinf = _rope_tables(pos)
    rope_spec = pl.BlockSpec((TILE, LANES), lambda s: (s % TILES, 0))
    return dict(
        mixer=_ret_prompt_kernel,
        in_specs=[_tile_spec(2), _tile_spec(3), _tile_spec(0), _tile_spec(1), rope_spec, rope_spec]
                 + _ret_table_specs(TILE),
        args=[proj_b, proj_b, proj_c, proj_c, cosf, sinf, *tabs, gn_g.reshape(1, -1)],
        out_specs=[_tile_spec(0), _state_spec(RET_HEADS, RET_DK, RET_DV)],
        out_shape=[jax.ShapeDtypeStruct((N_PROMPT, 1024), BF16),
                   jax.ShapeDtypeStruct((BATCH, RET_HEADS, RET_DK, RET_DV), F32)])


def _ret_sample_call(proj_b, proj_c, gn_g, r0, layer, r_all):
    rows = jnp.arange(RET_SROWS)
    tabs = _ret_tables(rows % RET_SB, (rows // RET_SB).astype(F32))
    cosf, sinf = _rope_tables((rows // RET_SB).astype(F32) + float(PAST_LEN))
    pb3, pc3 = _sample3(proj_b), _sample3(proj_c)
    st_block = (None, RET_SB, RET_HEADS, RET_DK, RET_DV)
    st_spec = pl.BlockSpec(st_block, lambda i: (layer, i, 0, 0, 0))
    in_specs = [_sample_spec(RET_SB, 2), _sample_spec(RET_SB, 3), _sample_spec(RET_SB, 0), _sample_spec(RET_SB, 1),
                st_spec, _const_spec((RET_SROWS, LANES)), _const_spec((RET_SROWS, LANES))] \
        + _ret_table_specs(RET_SROWS)
    args = [pb3, pb3, pc3, pc3, r0, cosf, sinf, *tabs, gn_g.reshape(1, -1)]
    aliases = {}
    if r_all is not None:
        in_specs.append(pl.BlockSpec(memory_space=pl.ANY))
        args.append(r_all)
        aliases = {len(args) - 1: 1}
    ys, r_all = pl.pallas_call(
        _ret_sample_kernel, grid=(DEC_BATCH // RET_SB,),
        in_specs=in_specs,
        out_specs=[pl.BlockSpec((DEC_SEQ, RET_SB, 1024), lambda i: (0, i, 0)), st_spec],
        out_shape=[jax.ShapeDtypeStruct((DEC_SEQ, DEC_BATCH, 1024), BF16),
                   jax.ShapeDtypeStruct((DEPTH, DEC_BATCH, RET_HEADS, RET_DK, RET_DV), F32)],
        input_output_aliases=aliases,
        compiler_params=_params(("parallel",)), name="ret_sample",
    )(*args)
    return ys.reshape(N_SAMPLE, 1024), r_all


def _merge_kernel(na, ysp, yss, ylp, yls, yrp, yrs, gs_ref, gl_ref, gr_ref, ws_ref, wl_ref, wr_ref, o_ref):
    acc = None
    for a_ref, b_ref, g_ref, w_ref in ((ysp, yss, gs_ref, ws_ref), (ylp, yls, gl_ref, wl_ref),
                                       (yrp, yrs, gr_ref, wr_ref)):
        b = jnp.dot(_pick(na, a_ref, b_ref), w_ref[...], preferred_element_type=F32)
        term = _sigmoid(g_ref[...].astype(F32)) * b
        acc = term if acc is None else acc + term
    o_ref[...] = acc.astype(o_ref.dtype)


def _merge_call(ys, yl, yr, proj_c, proj_d, wb, tm=512):
    specs, na = _two_part_specs(tm, 1024)
    gsp = lambda j: pl.BlockSpec((tm, D_MODEL), lambda i: (i, j))
    return pl.pallas_call(
        functools.partial(_merge_kernel, na), grid=(N_TOK // tm,),
        in_specs=specs * 3 + [gsp(1), gsp(0), gsp(1)] + [_const_spec((1024, D_MODEL))] * 3,
        out_specs=pl.BlockSpec((tm, D_MODEL), lambda i: (i, 0)),
        out_shape=jax.ShapeDtypeStruct((N_TOK, D_MODEL), BF16),
        compiler_params=_params(("parallel",)), name="merge",
    )(*ys, *yl, *yr, proj_c, proj_d, proj_d, *wb)


def _outproj_kernel(nx, na, m_ref, w_ref, *refs):
    g_ref, oa_ref, ob_ref = refs[nx:]
    x = refs[0][...] if nx == 1 else _pick(N_PROMPT // m_ref.shape[0], refs[0], refs[1])
    x = x + jnp.dot(m_ref[...], w_ref[...], preferred_element_type=F32)
    ms = jnp.mean(x * x, axis=-1, keepdims=True)
    xn = x * lax.rsqrt(ms + NORM_EPS) * g_ref[...]
    if na is None:
        oa_ref[...] = x
        ob_ref[...] = xn.astype(ob_ref.dtype)
    else:
        @pl.when(pl.program_id(0) < na)
        def _():
            oa_ref[...] = xn

        @pl.when(pl.program_id(0) >= na)
        def _():
            ob_ref[...] = xn


def _outproj_call(merged, w_out, x, g_next, final, tm=512):
    tok = pl.BlockSpec((tm, D_MODEL), lambda i: (i, 0))
    x = x if isinstance(x, tuple) else (x,)
    x_specs = [tok] if len(x) == 1 else _two_part_specs(tm, D_MODEL)[0]
    if final:
        out_specs, na = _two_part_specs(tm, D_MODEL)
        out_shape = [jax.ShapeDtypeStruct((N_PROMPT, D_MODEL), F32), jax.ShapeDtypeStruct((N_SAMPLE, D_MODEL), F32)]
    else:
        out_specs, na = [tok, tok], None
        out_shape = [jax.ShapeDtypeStruct((N_TOK, D_MODEL), F32), jax.ShapeDtypeStruct((N_TOK, D_MODEL), BF16)]
    return pl.pallas_call(
        functools.partial(_outproj_kernel, len(x), na), grid=(N_TOK // tm,),
        in_specs=[tok, _const_spec((D_MODEL, D_MODEL))] + x_specs + [_const_spec((1, D_MODEL))],
        out_specs=out_specs, out_shape=out_shape,
        compiler_params=_params(("arbitrary",)), name="outproj",
    )(merged, w_out, *x, g_next.reshape(1, D_MODEL))


def kernel(x_prompt, x_sample, state_s5_re, state_s5_im, state_lru, state_conv, state_ret, norm_g, w_in, s5_lambda_re, s5_lambda_im, s5_log_dt, s5_b_re, s5_b_im, s5_c_re, s5_c_im, s5_d, s5_w_glu, s5_b_glu, lru_conv_w, lru_conv_b, lru_w_a, lru_b_a, lru_w_x, lru_b_x, lru_lambda, ret_gn_g, w_branch_s5, w_branch_lru, w_branch_ret, w_out, final_norm_g):
    x = _to_rows(x_prompt, x_sample)
    xn = _rmsnorm_call(*x, norm_g[0])
    outs_p = [[] for _ in range(5)]
    outs_s = [[] for _ in range(4)]
    r_s = None
    for l in range(DEPTH):
        s5_w = (_s5_prep(s5_lambda_re[l], s5_lambda_im[l], s5_log_dt[l], s5_b_re[l], s5_b_im[l],
                         s5_c_re[l], s5_c_im[l]), s5_d[l], s5_w_glu[l].astype(BF16), s5_b_glu[l])
        wg = jnp.concatenate([lru_w_a[l], lru_w_x[l]], axis=-1).astype(BF16)
        lru_w = _lru_weights(lru_conv_w[l], lru_conv_b[l], wg, lru_b_a[l], lru_b_x[l], lru_lambda[l])
        cbuf = state_conv[l].transpose(1, 0, 2).reshape((CONV_WIDTH - 1) * DEC_BATCH, LRU_WIDTH)

        _, proj_a = _inproj_call("inproj_a", xn, w_in, l, 0)
        (ys_p, hr_p, hi_p), proj_b = _inproj_call("inproj_b_s5", xn, w_in, l, 1, **_s5_prompt_mixer(proj_a, *s5_w))
        ys_s, hr_s, hi_s = _s5_sample_call(proj_a, *s5_w, state_s5_re[l].reshape(DEC_BATCH, S5_LANES),
                                           state_s5_im[l].reshape(DEC_BATCH, S5_LANES))
        (yl_p, hl_p, conv_p), proj_c = _inproj_call("inproj_c_lru", xn, w_in, l, 2,
                                                    **_lru_prompt_mixer(proj_b, lru_w))
        conv_p = _lru_prompt_conv_state(conv_p)
        yl_s, hl_s, conv_s = _lru_sample_call(proj_b, lru_w, state_lru[l], cbuf)
        (yr_p, r_p), proj_d = _inproj_call("inproj_d_ret", xn, w_in, l, 3,
                                           **_ret_prompt_mixer(proj_b, proj_c, ret_gn_g[l]))
        yr_s, r_s = _ret_sample_call(proj_b, proj_c, ret_gn_g[l], state_ret, l, r_s)

        wb = [_cast_bf16(w, l) for w in (w_branch_s5, w_branch_lru, w_branch_ret)]
        merged = _merge_call((ys_p, ys_s), (yl_p, yl_s), (yr_p, yr_s), proj_c, proj_d, wb)
        final = l == DEPTH - 1
        g_next = final_norm_g if final else norm_g[l + 1]
        x, xn = _outproj_call(merged, _cast_bf16(w_out, l), x, g_next, final)

        st = (S5_GROUPS, S5_STATE)
        for lst, vals in ((outs_p, (hr_p.reshape(BATCH, *st), hi_p.reshape(BATCH, *st),
                                    hl_p.reshape(BATCH, LRU_WIDTH), conv_p, r_p)),
                          (outs_s, (hr_s.reshape(DEC_BATCH, *st), hi_s.reshape(DEC_BATCH, *st), hl_s, conv_s))):
            for j, v in enumerate(vals):
                lst[j].append(v)

    y_prompt, y_sample = _from_rows(x, xn)
    sp = [jnp.stack(t, axis=0) for t in outs_p]
    ss = [jnp.stack(t, axis=0) for t in outs_s]
    return (y_prompt, y_sample, *sp, *ss, r_s)
```

```python
import functools

import jax
import jax.numpy as jnp
from jax import lax
from jax.experimental import pallas as pl
from jax.experimental.pallas import tpu as pltpu

F32 = jnp.float32
BF16 = jnp.bfloat16

D_MODEL = 2048
BATCH = 4
SEQ = 2048
DEPTH = 2
DEC_BATCH = 128
DEC_SEQ = 8
PAST_LEN = 16384
S5_WIDTH = 1024
S5_GROUP = 16
S5_GROUPS = 64
S5_STATE = 64
S5_LANES = S5_GROUPS * S5_STATE
LRU_WIDTH = 1024
LRU_BLOCKS = 8
LRU_BLOCK = 128
CONV_WIDTH = 4
LRU_C = 8.0
RET_HEADS = 8
RET_DK = 128
RET_DV = 128
RET_WIDTH = 1024
ROPE_BASE = 10000.0
NORM_EPS = 1e-6
GN_EPS = 1e-5
N_IN = 14336

N_PROMPT = BATCH * SEQ
N_SAMPLE = DEC_BATCH * DEC_SEQ
N_TOK = N_PROMPT + N_SAMPLE

SUBLANES = 8
LANES = 128
VMEM_LIMIT = 56 * 1024 * 1024

TILE = 256
SEG = TILE // SUBLANES
TILES = SEQ // TILE

PROJ_GROUPS = ((0, 2), (2, 4), (6, 4), (10, 4))
MM_COLS = 1024
MM_UNITS = TILES
MM_ROWS = N_TOK // MM_UNITS
MM_COL_PIECES = 2
MM_PIECE_COLS = MM_COLS // MM_COL_PIECES
MM_ROW_PIECES = 4
MM_PIECE_ROWS = MM_ROWS // MM_ROW_PIECES
MM_PIECES = MM_COL_PIECES * MM_ROW_PIECES

S5_KB = 4
S5_KW = S5_WIDTH // S5_KB
S5_NW = S5_LANES // S5_KB
SCAN_W = 512

SAMPLE_ROW0 = N_PROMPT // DEC_BATCH


def _params(sem, vmem=VMEM_LIMIT):
    return pltpu.CompilerParams(dimension_semantics=sem, vmem_limit_bytes=vmem)


def _const_spec(shape, single=False):
    return pl.BlockSpec(shape, lambda *_: (0,) * len(shape), pipeline_mode=pl.Buffered(1) if single else None)


def _sigmoid(x):
    return jax.nn.sigmoid(x)


def _silu(x):
    return x * jax.nn.sigmoid(x)


def _bcast_row(x, row):
    return jnp.broadcast_to(x[row:row + 1, :], x.shape)


def _cmul(ar, ai, br, bi):
    return ar * br - ai * bi, ar * bi + ai * br


def _to_rows(x_prompt, x_sample):
    xp = x_prompt.reshape(BATCH, TILES, SUBLANES, SEG, -1).transpose(0, 1, 3, 2, 4).reshape(N_PROMPT, -1)
    xs = x_sample.transpose(1, 0, 2).reshape(N_SAMPLE, -1)
    return xp, xs


def _from_rows(yp, ys):
    yp = yp.reshape(BATCH, TILES, SEG, SUBLANES, -1).transpose(0, 1, 3, 2, 4).reshape(BATCH, SEQ, -1)
    ys = ys.reshape(DEC_SEQ, DEC_BATCH, -1).transpose(1, 0, 2)
    return yp, ys


def _two_part_specs(tm, width):
    na = N_PROMPT // tm
    return [pl.BlockSpec((tm, width), lambda i: (jnp.minimum(i, na - 1), 0)),
            pl.BlockSpec((tm, width), lambda i: (jnp.maximum(i - na, 0), 0))], na


def _pick(na, a_ref, b_ref):
    return jnp.where(pl.program_id(0) < na, a_ref[...], b_ref[...])


def _norm_kernel(na, xa_ref, xb_ref, g_ref, o_ref):
    x = _pick(na, xa_ref, xb_ref)
    ms = jnp.mean(x * x, axis=-1, keepdims=True)
    o_ref[...] = (x * lax.rsqrt(ms + NORM_EPS) * g_ref[...]).astype(o_ref.dtype)


def _rmsnorm_call(xp, xs, g, tm=512):
    specs, na = _two_part_specs(tm, D_MODEL)
    return pl.pallas_call(
        functools.partial(_norm_kernel, na),
        grid=(N_TOK // tm,),
        in_specs=specs + [_const_spec((1, D_MODEL))],
        out_specs=pl.BlockSpec((tm, D_MODEL), lambda i: (i, 0)),
        out_shape=jax.ShapeDtypeStruct((N_TOK, D_MODEL), BF16),
        compiler_params=_params(("parallel",)),
        name="rmsnorm",
    )(xp, xs, g.reshape(1, D_MODEL))


def _cast_kernel(mm_piece, *refs):
    n = len(refs) // 2
    for k in range(n):
        for p in range(k * MM_PIECES // n, (k + 1) * MM_PIECES // n):
            mm_piece(p)
        refs[n + k][...] = refs[k][...].astype(BF16)


def _cast_mixer(layer, weights, steps):
    shapes = [w.shape[1:] for w in weights]
    return dict(
        mixer=_cast_kernel,
        in_specs=[pl.BlockSpec((None, r // steps, c), lambda s: (layer, s, 0)) for r, c in shapes],
        args=list(weights),
        out_specs=[pl.BlockSpec((r // steps, c), lambda s: (s, 0)) for r, c in shapes],
        out_shape=[jax.ShapeDtypeStruct((r, c), BF16) for r, c in shapes])


def _inproj_kernel(mixer, n_in, n_out, *refs):
    xn_ref, w_ref = refs[n_in:n_in + 2]
    o_ref = refs[n_in + 2 + n_out]
    wbf_ref = refs[n_in + 3 + n_out]

    @pl.when(pl.program_id(0) % MM_UNITS == 0)
    def _():
        wbf_ref[...] = w_ref[...].astype(BF16)

    def piece(k):
        rows = slice((k // MM_COL_PIECES) * MM_PIECE_ROWS, (k // MM_COL_PIECES + 1) * MM_PIECE_ROWS)
        cols = slice((k % MM_COL_PIECES) * MM_PIECE_COLS, (k % MM_COL_PIECES + 1) * MM_PIECE_COLS)
        o_ref[rows, cols] = jnp.dot(xn_ref[rows, :], wbf_ref[:, cols],
                                    preferred_element_type=F32).astype(o_ref.dtype)

    if mixer is None:
        for k in range(MM_PIECES):
            piece(k)
    else:
        mixer(piece, *refs[:n_in], *refs[n_in + 2:n_in + 2 + n_out], *refs[n_in + 4 + n_out:])


def _inproj_call(name, xn, w_in, layer, group, mixer=None, in_specs=(), args=(), out_specs=(), out_shape=(),
                 scratch=()):
    first, tiles = PROJ_GROUPS[group]
    mm_in = [pl.BlockSpec((MM_ROWS, D_MODEL), lambda s: (s % MM_UNITS, 0)),
             pl.BlockSpec((None, D_MODEL, MM_COLS), lambda s: (layer, 0, first + s // MM_UNITS))]
    mm_out = pl.BlockSpec((MM_ROWS, MM_COLS), lambda s: (s % MM_UNITS, s // MM_UNITS))
    res = pl.pallas_call(
        functools.partial(_inproj_kernel, mixer, len(in_specs), len(out_specs)),
        grid=(tiles * MM_UNITS,),
        in_specs=list(in_specs) + mm_in,
        out_specs=list(out_specs) + [mm_out],
        out_shape=list(out_shape) + [jax.ShapeDtypeStruct((N_TOK, tiles * MM_COLS), BF16)],
        scratch_shapes=[pltpu.VMEM((D_MODEL, MM_COLS), BF16)] + list(scratch),
        compiler_params=_params(("arbitrary",)),
        name=name,
    )(*args, xn, w_in)
    return res[:-1], res[-1]


def _s5_prep(lam_re, lam_im, log_dt, b_re, b_im, c_re, c_im):
    dt = jnp.exp(log_dt)[:, None]
    e = jnp.exp(lam_re * dt)
    lbr = e * jnp.cos(lam_im * dt)
    lbi = e * jnp.sin(lam_im * dt)
    nr, ni = lbr - 1.0, lbi
    den = lam_re * lam_re + lam_im * lam_im
    cr = (nr * lam_re + ni * lam_im) / den
    ci = (ni * lam_re - nr * lam_im) / den
    bbr = cr[..., None] * b_re - ci[..., None] * b_im
    bbi = cr[..., None] * b_im + ci[..., None] * b_re
    gpb = S5_GROUPS // S5_KB

    hsel = jnp.arange(gpb)
    t = jnp.stack([bbr, bbi]).astype(BF16).reshape(2, S5_KB, gpb, S5_STATE, S5_GROUP)
    t = t.transpose(1, 2, 4, 0, 3).reshape(S5_KB, S5_KW, 2, 1, S5_STATE)
    same = (jnp.arange(S5_KW)[:, None] // S5_GROUP) == hsel[None, :]
    b_blk = jnp.where(same[None, :, None, :, None], t, 0).reshape(S5_KB, S5_KW, 2 * S5_NW)
    t = jnp.stack([c_re, -c_im]).astype(BF16).reshape(2, S5_KB, gpb, S5_GROUP, S5_STATE)
    t = t.transpose(1, 0, 2, 4, 3).reshape(S5_KB, 2 * S5_NW, 1, S5_GROUP)
    same = ((jnp.arange(2 * S5_NW)[:, None] % S5_NW) // S5_STATE) == hsel[None, :]
    c_blk = jnp.where(same[None, :, :, None], t, 0).reshape(S5_KB, 2 * S5_NW, S5_KW)

    lr, li = lbr.reshape(-1), lbi.reshape(-1)
    full = lambda v: jnp.broadcast_to(v, (SUBLANES, S5_LANES))
    lam = jnp.stack([full(lr), full(li)])
    sr, si = lr, li
    for _ in range(SEG.bit_length() - 1):
        sr, si = _cmul(sr, si, sr, si)
    pr, pi = [sr], [si]
    for _ in range(SUBLANES - 1):
        r_, i_ = _cmul(pr[-1], pi[-1], sr, si)
        pr.append(r_)
        pi.append(i_)
    row = jnp.arange(SUBLANES)[:, None]
    ak = jnp.stack([jnp.stack([jnp.where(row >= k, pr[k - 1][None, :], 0.0),
                               jnp.where(row >= k, pi[k - 1][None, :], 0.0)]) for k in (1, 2, 4)])
    pw = jnp.stack([jnp.stack(pr), jnp.stack(pi)])
    return b_blk, c_blk, lam, ak, pw


def _s5_drive_block(kb, u, bblk_ref, scr):
    res = jnp.dot(u[:, kb * S5_KW:(kb + 1) * S5_KW], bblk_ref[kb], preferred_element_type=F32)
    scr[:, kb * S5_NW:(kb + 1) * S5_NW] = res[:, :S5_NW]
    scr[:, S5_LANES + kb * S5_NW:S5_LANES + (kb + 1) * S5_NW] = res[:, S5_NW:]


def _s5_readout_block(kb, scr, cblk_ref):
    hcat = jnp.concatenate(
        [scr[:, kb * S5_NW:(kb + 1) * S5_NW],
         scr[:, S5_LANES + kb * S5_NW:S5_LANES + (kb + 1) * S5_NW]], axis=1).astype(BF16)
    return jnp.dot(hcat, cblk_ref[kb], preferred_element_type=F32)


def _s5_finish(parts, u, z, d_ref, wglu_ref, bglu_ref):
    y = jnp.concatenate(parts, axis=1) + d_ref[...] * u.astype(F32)
    y = jax.nn.gelu(y, approximate=True)
    glu = jnp.dot(y.astype(BF16), wglu_ref[...], preferred_element_type=F32) + bglu_ref[...]
    y = y * _sigmoid(glu)
    return (y * _silu(z.astype(F32))).astype(BF16)


def _s5_prompt_kernel(mm_piece, u_ref, z_ref, bblk_ref, cblk_ref, lam_ref, ak_ref, pw_ref, d_ref, wglu_ref,
                      bglu_ref, y_ref, hr_ref, hi_ref, scr, car):
    t = pl.program_id(0) % TILES

    @pl.when(t == 0)
    def _():
        car[...] = jnp.zeros_like(car)

    w = SCAN_W
    rowi = lax.broadcasted_iota(jnp.int32, (SUBLANES, w), 0)
    u = u_ref[...]
    parts = []
    for kb in range(S5_KB):
        _s5_drive_block(kb, u, bblk_ref, scr)
        for c in range(kb * (S5_NW // w), (kb + 1) * (S5_NW // w)):
            mm_piece(c)
            sl_re = slice(c * w, (c + 1) * w)
            sl_im = slice(S5_LANES + c * w, S5_LANES + (c + 1) * w)
            lr, li = lam_ref[0, :, sl_re], lam_ref[1, :, sl_re]

            xr = xi = jnp.zeros((SUBLANES, w), F32)
            for r in range(SEG):
                rows = slice(r * SUBLANES, (r + 1) * SUBLANES)
                pr, pi = _cmul(lr, li, xr, xi)
                xr = scr[rows, sl_re] + pr
                xi = scr[rows, sl_im] + pi
                scr[rows, sl_re] = xr
                scr[rows, sl_im] = xi

            for k, idx in zip((1, 2, 4), range(3)):
                pr, pi = _cmul(ak_ref[idx, 0, :, sl_re], ak_ref[idx, 1, :, sl_re],
                               pltpu.roll(xr, k, 0), pltpu.roll(xi, k, 0))
                xr, xi = xr + pr, xi + pi
            cr, ci = car[0, :, sl_re], car[1, :, sl_re]
            pr, pi = _cmul(pw_ref[0, :, sl_re], pw_ref[1, :, sl_re], cr, ci)
            fr, fi = xr + pr, xi + pi
            dr = jnp.where(rowi == 0, cr, pltpu.roll(fr, 1, 0))
            di = jnp.where(rowi == 0, ci, pltpu.roll(fi, 1, 0))
            car[0, :, sl_re] = _bcast_row(fr, SUBLANES - 1)
            car[1, :, sl_re] = _bcast_row(fi, SUBLANES - 1)

            for r in range(SEG):
                rows = slice(r * SUBLANES, (r + 1) * SUBLANES)
                dr, di = _cmul(lr, li, dr, di)
                scr[rows, sl_re] = scr[rows, sl_re] + dr
                scr[rows, sl_im] = scr[rows, sl_im] + di
        parts.append(_s5_readout_block(kb, scr, cblk_ref))

    y_ref[...] = _s5_finish(parts, u, z_ref[...], d_ref, wglu_ref, bglu_ref)

    @pl.when(t == TILES - 1)
    def _():
        hr_ref[0] = car[0, 0:1, :]
        hi_ref[0] = car[1, 0:1, :]


S5_SB = 32


def _s5_sample_kernel(u_ref, z_ref, h0r_ref, h0i_ref, bblk_ref, cblk_ref, lam_ref, d_ref, wglu_ref, bglu_ref,
                      y_ref, hr_ref, hi_ref, scr):
    rows = DEC_SEQ * S5_SB
    u = u_ref[...].reshape(rows, S5_WIDTH)
    w = SCAN_W
    parts = []
    for kb in range(S5_KB):
        _s5_drive_block(kb, u, bblk_ref, scr)
        for c in range(kb * (S5_NW // w), (kb + 1) * (S5_NW // w)):
            sl_re = slice(c * w, (c + 1) * w)
            sl_im = slice(S5_LANES + c * w, S5_LANES + (c + 1) * w)
            lr, li = lam_ref[0, :, sl_re], lam_ref[1, :, sl_re]
            for g in range(S5_SB // SUBLANES):
                seqs = slice(g * SUBLANES, (g + 1) * SUBLANES)
                hr, hi = h0r_ref[seqs, sl_re], h0i_ref[seqs, sl_re]
                for j in range(DEC_SEQ):
                    rows_j = slice(j * S5_SB + g * SUBLANES, j * S5_SB + (g + 1) * SUBLANES)
                    pr, pi = _cmul(lr, li, hr, hi)
                    hr = scr[rows_j, sl_re] + pr
                    hi = scr[rows_j, sl_im] + pi
                    scr[rows_j, sl_re] = hr
                    scr[rows_j, sl_im] = hi
                hr_ref[seqs, sl_re] = hr
                hi_ref[seqs, sl_re] = hi
        parts.append(_s5_readout_block(kb, scr, cblk_ref))

    y = _s5_finish(parts, u, z_ref[...].reshape(rows, S5_WIDTH), d_ref, wglu_ref, bglu_ref)
    y_ref[...] = y.reshape(DEC_SEQ, S5_SB, S5_WIDTH)


def _tile_spec(col):
    return pl.BlockSpec((TILE, 1024), lambda s: (s, col))


def _state_spec(*shape):
    return pl.BlockSpec((1,) + shape, lambda s: (s // TILES,) + (0,) * len(shape))


def _sample3(proj):
    return proj.reshape(N_TOK // DEC_BATCH, DEC_BATCH, proj.shape[-1])


def _sample_spec(nseq, col):
    return pl.BlockSpec((DEC_SEQ, nseq, 1024), lambda i: (SAMPLE_ROW0 // DEC_SEQ, i, col))


def _s5_specs(prep, d, wglu, bglu, single):
    b_blk, c_blk, lam, ak, pw = prep
    spec = functools.partial(_const_spec, single=single)
    lam_spec = spec((2, SUBLANES, S5_LANES))
    mm = ([spec((S5_KB, S5_KW, 2 * S5_NW)), spec((S5_KB, 2 * S5_NW, S5_KW))], [b_blk, c_blk])
    chain = ([spec((3, 2, SUBLANES, S5_LANES)), lam_spec], [ak, pw])
    tail = ([spec((1, S5_WIDTH)), spec((S5_WIDTH, S5_WIDTH)), spec((1, S5_WIDTH))],
            [d.reshape(1, -1), wglu, bglu.reshape(1, -1)])
    return mm, (lam_spec, lam), chain, tail


def _s5_prompt_mixer(proj_a, prep, d, wglu, bglu):
    mm, (lam_spec, lam), chain, tail = _s5_specs(prep, d, wglu, bglu, single=True)
    return dict(
        mixer=_s5_prompt_kernel,
        in_specs=[_tile_spec(0), _tile_spec(1)] + mm[0] + [lam_spec] + chain[0] + tail[0],
        args=[proj_a, proj_a] + mm[1] + [lam] + chain[1] + tail[1],
        out_specs=[_tile_spec(0), _state_spec(1, S5_LANES), _state_spec(1, S5_LANES)],
        out_shape=[jax.ShapeDtypeStruct((N_PROMPT, 1024), BF16),
                   jax.ShapeDtypeStruct((BATCH, 1, S5_LANES), F32),
                   jax.ShapeDtypeStruct((BATCH, 1, S5_LANES), F32)],
        scratch=[pltpu.VMEM((TILE, 2 * S5_LANES), F32), pltpu.VMEM((2, SUBLANES, S5_LANES), F32)])


def _s5_sample_call(proj_a, prep, d, wglu, bglu, h0r, h0i):
    mm, (lam_spec, lam), _, tail = _s5_specs(prep, d, wglu, bglu, single=False)
    proj3 = _sample3(proj_a)
    st = pl.BlockSpec((S5_SB, S5_LANES), lambda i: (i, 0))
    ys, hr_s, hi_s = pl.pallas_call(
        _s5_sample_kernel, grid=(DEC_BATCH // S5_SB,),
        in_specs=[_sample_spec(S5_SB, 0), _sample_spec(S5_SB, 1), st, st] + mm[0] + [lam_spec] + tail[0],
        out_specs=[pl.BlockSpec((DEC_SEQ, S5_SB, 1024), lambda i: (0, i, 0)), st, st],
        out_shape=[jax.ShapeDtypeStruct((DEC_SEQ, DEC_BATCH, 1024), BF16),
                   jax.ShapeDtypeStruct((DEC_BATCH, S5_LANES), F32),
                   jax.ShapeDtypeStruct((DEC_BATCH, S5_LANES), F32)],
        scratch_shapes=[pltpu.VMEM((DEC_SEQ * S5_SB, 2 * S5_LANES), F32)],
        compiler_params=_params(("parallel",)), name="s5_sample",
    )(proj3, proj3, h0r, h0i, *mm[1], lam, *tail[1])
    return ys.reshape(N_SAMPLE, 1024), hr_s, hi_s


def _lru_gates_block(n, xc, wg_ref, ba_ref, bx_ref, lam_ref):
    sl = slice(n * LRU_BLOCK, (n + 1) * LRU_BLOCK)
    xb = xc[:, sl]
    res = jnp.dot(xb.astype(BF16), wg_ref[n], preferred_element_type=F32)
    r_gate = _sigmoid(res[:, :LRU_BLOCK] + ba_ref[:, sl])
    i_gate = _sigmoid(res[:, LRU_BLOCK:] + bx_ref[:, sl])
    nl = -lam_ref[:, sl]
    softplus = jnp.maximum(nl, 0.0) + jnp.log1p(jnp.exp(-jnp.abs(nl)))
    a = jnp.exp(-LRU_C * r_gate * softplus)
    return a, jnp.sqrt(1.0 - a * a) * (i_gate * xb)


def _lru_gates(xc, wg_ref, ba_ref, bx_ref, lam_ref):
    ab = [_lru_gates_block(n, xc, wg_ref, ba_ref, bx_ref, lam_ref) for n in range(LRU_BLOCKS)]
    return jnp.concatenate([a for a, _ in ab], axis=1), jnp.concatenate([b for _, b in ab], axis=1)


def _lru_prompt_kernel(mm_piece, u_ref, z_ref, cw_ref, cb_ref, wg_ref, ba_ref, bx_ref, lam_ref,
                       y_ref, hl_ref, cv_ref, a_scr, b_scr, tail, car):
    t = pl.program_id(0) % TILES
    nwrap = CONV_WIDTH - 1

    @pl.when(t == 0)
    def _():
        tail[...] = jnp.zeros_like(tail)
        car[...] = jnp.zeros_like(car)

    x = u_ref[...].astype(F32)
    rowi = lax.broadcasted_iota(jnp.int32, (SUBLANES, LRU_WIDTH), 0)
    wrap = []
    for k in range(nwrap):
        cur = x[TILE - (nwrap - k) * SUBLANES:TILE - (nwrap - k - 1) * SUBLANES, :]
        prev = tail[k * SUBLANES:(k + 1) * SUBLANES, :]
        wrap.append(jnp.where(rowi == 0, pltpu.roll(prev, 1, 0), pltpu.roll(cur, 1, 0)))
    xc = cw_ref[nwrap:nwrap + 1, :] * x + cb_ref[...]
    for s in range(1, CONV_WIDTH):
        shifted = jnp.concatenate(wrap[nwrap - s:] + [x[:TILE - s * SUBLANES, :]], axis=0)
        xc = xc + cw_ref[nwrap - s:nwrap - s + 1, :] * shifted
    tail[...] = x[TILE - nwrap * SUBLANES:, :]

    for n in range(LRU_BLOCKS):
        mm_piece(n)
        sl = slice(n * LRU_BLOCK, (n + 1) * LRU_BLOCK)
        a_scr[:, sl], b_scr[:, sl] = _lru_gates_block(n, xc, wg_ref, ba_ref, bx_ref, lam_ref)

    w = SCAN_W
    rw = lax.broadcasted_iota(jnp.int32, (SUBLANES, w), 0)
    for c in range(LRU_WIDTH // w):
        sl = slice(c * w, (c + 1) * w)

        bv = jnp.zeros((SUBLANES, w), F32)
        av = jnp.ones((SUBLANES, w), F32)
        for r in range(SEG):
            rows = slice(r * SUBLANES, (r + 1) * SUBLANES)
            ar = a_scr[rows, sl]
            bv = ar * bv + b_scr[rows, sl]
            av = ar * av
            b_scr[rows, sl] = bv
            a_scr[rows, sl] = av

        for k in (1, 2, 4):
            sa = jnp.where(rw >= k, pltpu.roll(av, k, 0), 1.0)
            sb = jnp.where(rw >= k, pltpu.roll(bv, k, 0), 0.0)
            bv = bv + av * sb
            av = av * sa
        cv = car[:, sl]
        full = bv + av * cv
        enter = jnp.where(rw == 0, cv, pltpu.roll(full, 1, 0))
        car[:, sl] = _bcast_row(full, SUBLANES - 1)

        for r in range(SEG):
            rows = slice(r * SUBLANES, (r + 1) * SUBLANES)
            b_scr[rows, sl] = b_scr[rows, sl] + a_scr[rows, sl] * enter

    y_ref[...] = (b_scr[...] * _silu(z_ref[...].astype(F32))).astype(y_ref.dtype)

    @pl.when(t == TILES - 1)
    def _():
        hl_ref[0] = car[0:1, :]
        cv_ref[0] = tail[...]


def _lru_sample_kernel(u_ref, z_ref, h0_ref, cbuf_ref, cw_ref, cb_ref, wg_ref, ba_ref, bx_ref, lam_ref,
                       y_ref, hl_ref, cv_ref, ext):
    nb = DEC_BATCH
    nwrap = CONV_WIDTH - 1
    ext[0:nwrap * nb, :] = cbuf_ref[...]
    ext[nwrap * nb:, :] = u_ref[...].astype(F32)
    h = h0_ref[...]
    for j in range(DEC_SEQ):
        xc = cb_ref[...]
        for k in range(CONV_WIDTH):
            xc = xc + cw_ref[k:k + 1, :] * ext[(j + k) * nb:(j + k + 1) * nb, :]
        a, b = _lru_gates(xc, wg_ref, ba_ref, bx_ref, lam_ref)
        h = a * h + b
        y_ref[j * nb:(j + 1) * nb, :] = (h * _silu(z_ref[j * nb:(j + 1) * nb, :].astype(F32))).astype(y_ref.dtype)
    hl_ref[...] = h
    cv_ref[...] = ext[DEC_SEQ * nb:, :]


def _lru_weights(conv_w, conv_b, wg, b_a, b_x, lam):
    specs = [_const_spec((CONV_WIDTH, LRU_WIDTH)), _const_spec((1, LRU_WIDTH)),
             _const_spec((LRU_BLOCKS, LRU_BLOCK, 2 * LRU_BLOCK)),
             _const_spec((1, LRU_WIDTH)), _const_spec((1, LRU_WIDTH)), _const_spec((1, LRU_WIDTH))]
    args = [conv_w, conv_b.reshape(1, -1), wg, b_a.reshape(1, -1), b_x.reshape(1, -1), lam.reshape(1, -1)]
    return specs, args


def _lru_prompt_mixer(proj_b, weights):
    w_specs, w_args = weights
    nrows = (CONV_WIDTH - 1) * SUBLANES
    return dict(
        mixer=_lru_prompt_kernel,
        in_specs=[_tile_spec(0), _tile_spec(1)] + w_specs,
        args=[proj_b, proj_b] + w_args,
        out_specs=[_tile_spec(0), _state_spec(1, LRU_WIDTH), _state_spec(nrows, LRU_WIDTH)],
        out_shape=[jax.ShapeDtypeStruct((N_PROMPT, 1024), BF16),
                   jax.ShapeDtypeStruct((BATCH, 1, LRU_WIDTH), F32),
                   jax.ShapeDtypeStruct((BATCH, nrows, LRU_WIDTH), F32)],
        scratch=[pltpu.VMEM((TILE, LRU_WIDTH), F32), pltpu.VMEM((TILE, LRU_WIDTH), F32),
                 pltpu.VMEM((nrows, LRU_WIDTH), F32), pltpu.VMEM((SUBLANES, LRU_WIDTH), F32)])


def _lru_prompt_conv_state(cv_p):
    return cv_p.reshape(BATCH, CONV_WIDTH - 1, SUBLANES, LRU_WIDTH)[:, :, SUBLANES - 1, :]


def _lru_sample_call(proj_b, weights, h0, cbuf):
    w_specs, w_args = weights
    srow = N_PROMPT // N_SAMPLE
    ys, hl_s, cv_s = pl.pallas_call(
        _lru_sample_kernel, grid=(1,),
        in_specs=[pl.BlockSpec((N_SAMPLE, 1024), lambda i: (srow, 0)),
                  pl.BlockSpec((N_SAMPLE, 1024), lambda i: (srow, 1)),
                  _const_spec((DEC_BATCH, LRU_WIDTH)),
                  _const_spec(((CONV_WIDTH - 1) * DEC_BATCH, LRU_WIDTH))] + w_specs,
        out_specs=[_const_spec((N_SAMPLE, 1024)), _const_spec((DEC_BATCH, LRU_WIDTH)),
                   _const_spec(((CONV_WIDTH - 1) * DEC_BATCH, LRU_WIDTH))],
        out_shape=[jax.ShapeDtypeStruct((N_SAMPLE, 1024), BF16),
                   jax.ShapeDtypeStruct((DEC_BATCH, LRU_WIDTH), F32),
                   jax.ShapeDtypeStruct(((CONV_WIDTH - 1) * DEC_BATCH, LRU_WIDTH), F32)],
        scratch_shapes=[pltpu.VMEM(((CONV_WIDTH - 1) * DEC_BATCH + N_SAMPLE, LRU_WIDTH), F32)],
        compiler_params=_params(("arbitrary",)), name="lru_sample",
    )(proj_b, proj_b, h0, cbuf, *w_args)
    return ys, hl_s, cv_s.reshape(CONV_WIDTH - 1, DEC_BATCH, LRU_WIDTH).transpose(1, 0, 2)


RET_SB = 16
RET_SROWS = RET_SB * DEC_SEQ


def _ret_tables(seq, idx):
    n = idx.shape[0]
    chunk = jnp.max(idx) + 1.0
    log_g = jnp.log1p(-jnp.exp2(-5.0 - jnp.arange(RET_HEADS, dtype=F32)))
    diff = idx[:, None] - idx[None, :]
    same = seq[:, None] == seq[None, :]
    dmask = jnp.where((diff[None] >= 0) & same[None],
                      jnp.exp(jnp.maximum(diff, 0.0)[None] * log_g[:, None, None]), 0.0)
    xi = jnp.exp((idx[None, :] + 1.0) * log_g[:, None])
    zeta = jnp.exp((chunk - 1.0 - idx[None, :]) * log_g[:, None])
    gch = jnp.exp(chunk * log_g)
    full = lambda t: jnp.broadcast_to(t[:, :, None], (RET_HEADS, n, LANES))
    gc = jnp.broadcast_to(gch[:, None, None], (RET_HEADS, SUBLANES, LANES))
    return dmask, full(xi), full(zeta), gc


def _rope_tables(pos):
    half = RET_DK // 2
    freq = ROPE_BASE ** (-jnp.arange(half, dtype=F32) / half)
    ang = pos[:, None] * freq[None, :]
    cos, sin = jnp.cos(ang), jnp.sin(ang)
    return jnp.concatenate([cos, cos], axis=-1), jnp.concatenate([-sin, sin], axis=-1)


def _rope(x, cosf, sinf):
    return x * cosf + pltpu.roll(x, RET_DK // 2, 1) * sinf


def _ret_head(h, q, k, v, cos, sin, dmask_ref, zeta_ref):
    sl = slice(h * RET_DK, (h + 1) * RET_DK)
    qb = _rope(q[:, sl].astype(F32), cos, sin).astype(BF16)
    kh = _rope(k[:, sl].astype(F32), cos, sin) * (RET_DK ** -0.5)
    vb = v[:, sl]
    sc = lax.dot_general(qb, kh.astype(BF16), (((1,), (1,)), ((), ())), preferred_element_type=F32)
    inner = jnp.dot((sc * dmask_ref[h]).astype(BF16), vb, preferred_element_type=F32)
    kz = (kh * zeta_ref[h]).astype(BF16)
    return sl, qb, kz, vb, inner


def _ret_finish(o, z, g):
    mu = jnp.mean(o, axis=-1, keepdims=True)
    oc = o - mu
    var = jnp.mean(oc * oc, axis=-1, keepdims=True)
    on = oc * lax.rsqrt(var + GN_EPS) * g
    return (on * _silu(z.astype(F32))).astype(BF16)


def _ret_prompt_kernel(mm_piece, q_ref, k_ref, v_ref, z_ref, cos_ref, sin_ref, dmask_ref, xi_ref, zeta_ref,
                       gc_ref, gng_ref, y_ref, r_ref):
    @pl.when(pl.program_id(0) % TILES == 0)
    def _():
        r_ref[...] = jnp.zeros_like(r_ref)

    cos, sin = cos_ref[...], sin_ref[...]
    for h in range(RET_HEADS):
        mm_piece(h)
        sl, qb, kz, vb, inner = _ret_head(h, q_ref, k_ref, v_ref, cos, sin, dmask_ref, zeta_ref)
        r = r_ref[0, h]
        cross = jnp.dot(qb, r.astype(BF16), preferred_element_type=F32) * xi_ref[h]
        upd = lax.dot_general(kz, vb, (((0,), (0,)), ((), ())), preferred_element_type=F32)
        r_ref[0, h] = r * gc_ref[h, 0:1, :] + upd
        y_ref[:, sl] = _ret_finish(inner + cross, z_ref[:, sl], gng_ref[:, sl])


def _ret_sample_kernel(q_ref, k_ref, v_ref, z_ref, r0_ref, cos_ref, sin_ref, dmask_ref, xi_ref,
                       zeta_ref, gc_ref, gng_ref, *rest):
    y_ref, r_ref = rest[-2:]
    cos, sin = cos_ref[...], sin_ref[...]
    q = q_ref[...].reshape(RET_SROWS, RET_WIDTH)
    k = k_ref[...].reshape(RET_SROWS, RET_WIDTH)
    v = v_ref[...].reshape(RET_SROWS, RET_WIDTH)
    z = z_ref[...].reshape(RET_SROWS, RET_WIDTH)
    rowseq = lax.broadcasted_iota(jnp.int32, (RET_SROWS, RET_DV), 0) % RET_SB
    outs = []
    for h in range(RET_HEADS):
        sl, qb, kz, vb, inner = _ret_head(h, q, k, v, cos, sin, dmask_ref, zeta_ref)
        rcat = jnp.concatenate([r0_ref[s, h] for s in range(RET_SB)], axis=1).astype(BF16)
        call = jnp.dot(qb, rcat, preferred_element_type=F32)
        cross = jnp.zeros((RET_SROWS, RET_DV), F32)
        for s in range(RET_SB):
            cross = jnp.where(rowseq == s, call[:, s * RET_DV:(s + 1) * RET_DV], cross)
        cross = cross * xi_ref[h]
        vf = vb.astype(F32)
        vexp = jnp.concatenate([jnp.where(rowseq == s, vf, 0.0) for s in range(RET_SB)],
                               axis=1).astype(BF16)
        upd = lax.dot_general(kz, vexp, (((0,), (0,)), ((), ())), preferred_element_type=F32)
        gc = gc_ref[h, 0:1, :]
        for s in range(RET_SB):
            r_ref[s, h] = r0_ref[s, h] * gc + upd[:, s * RET_DV:(s + 1) * RET_DV]
        outs.append(_ret_finish(inner + cross, z[:, sl], gng_ref[:, sl]))
    y_ref[...] = jnp.concatenate(outs, axis=1).reshape(DEC_SEQ, RET_SB, RET_WIDTH)


def _ret_table_specs(n):
    return [_const_spec((RET_HEADS, n, n)), _const_spec((RET_HEADS, n, LANES)),
            _const_spec((RET_HEADS, n, LANES)), _const_spec((RET_HEADS, SUBLANES, LANES)),
            _const_spec((1, RET_WIDTH))]


def _ret_prompt_mixer(proj_b, proj_c, gn_g):
    rows = jnp.arange(TILE)
    tok = ((rows % SUBLANES) * SEG + rows // SUBLANES).astype(F32)
    tabs = _ret_tables(jnp.zeros((TILE,), jnp.int32), tok)
    pos = (jnp.arange(TILES, dtype=F32)[:, None] * TILE + tok[None, :]).reshape(SEQ) + 0.0
    cosf, sinf = _rope_tables(pos)
    rope_spec = pl.BlockSpec((TILE, LANES), lambda s: (s % TILES, 0))
    return dict(
        mixer=_ret_prompt_kernel,
        in_specs=[_tile_spec(2), _tile_spec(3), _tile_spec(0), _tile_spec(1), rope_spec, rope_spec]
                 + _ret_table_specs(TILE),
        args=[proj_b, proj_b, proj_c, proj_c, cosf, sinf, *tabs, gn_g.reshape(1, -1)],
        out_specs=[_tile_spec(0), _state_spec(RET_HEADS, RET_DK, RET_DV)],
        out_shape=[jax.ShapeDtypeStruct((N_PROMPT, 1024), BF16),
                   jax.ShapeDtypeStruct((BATCH, RET_HEADS, RET_DK, RET_DV), F32)])


def _ret_sample_call(proj_b, proj_c, gn_g, r0, layer, r_all):
    rows = jnp.arange(RET_SROWS)
    tabs = _ret_tables(rows % RET_SB, (rows // RET_SB).astype(F32))
    cosf, sinf = _rope_tables((rows // RET_SB).astype(F32) + float(PAST_LEN))
    pb3, pc3 = _sample3(proj_b), _sample3(proj_c)
    st_block = (None, RET_SB, RET_HEADS, RET_DK, RET_DV)
    st_spec = pl.BlockSpec(st_block, lambda i: (layer, i, 0, 0, 0))
    in_specs = [_sample_spec(RET_SB, 2), _sample_spec(RET_SB, 3), _sample_spec(RET_SB, 0), _sample_spec(RET_SB, 1),
                st_spec, _const_spec((RET_SROWS, LANES)), _const_spec((RET_SROWS, LANES))] \
        + _ret_table_specs(RET_SROWS)
    args = [pb3, pb3, pc3, pc3, r0, cosf, sinf, *tabs, gn_g.reshape(1, -1)]
    aliases = {}
    if r_all is not None:
        in_specs.append(pl.BlockSpec(memory_space=pl.ANY))
        args.append(r_all)
        aliases = {len(args) - 1: 1}
    ys, r_all = pl.pallas_call(
        _ret_sample_kernel, grid=(DEC_BATCH // RET_SB,),
        in_specs=in_specs,
        out_specs=[pl.BlockSpec((DEC_SEQ, RET_SB, 1024), lambda i: (0, i, 0)), st_spec],
        out_shape=[jax.ShapeDtypeStruct((DEC_SEQ, DEC_BATCH, 1024), BF16),
                   jax.ShapeDtypeStruct((DEPTH, DEC_BATCH, RET_HEADS, RET_DK, RET_DV), F32)],
        input_output_aliases=aliases,
        compiler_params=_params(("parallel",)), name="ret_sample",
    )(*args)
    return ys.reshape(N_SAMPLE, 1024), r_all


def _merge_kernel(na, ysp, yss, ylp, yls, yrp, yrs, gs_ref, gl_ref, gr_ref, ws_ref, wl_ref, wr_ref, o_ref):
    acc = None
    for a_ref, b_ref, g_ref, w_ref in ((ysp, yss, gs_ref, ws_ref), (ylp, yls, gl_ref, wl_ref),
                                       (yrp, yrs, gr_ref, wr_ref)):
        b = jnp.dot(_pick(na, a_ref, b_ref), w_ref[...], preferred_element_type=F32)
        term = _sigmoid(g_ref[...].astype(F32)) * b
        acc = term if acc is None else acc + term
    o_ref[...] = acc.astype(o_ref.dtype)


def _merge_call(ys, yl, yr, proj_c, proj_d, wb, tm=512):
    specs, na = _two_part_specs(tm, 1024)
    gsp = lambda j: pl.BlockSpec((tm, D_MODEL), lambda i: (i, j))
    return pl.pallas_call(
        functools.partial(_merge_kernel, na), grid=(N_TOK // tm,),
        in_specs=specs * 3 + [gsp(1), gsp(0), gsp(1)] + [_const_spec((1024, D_MODEL))] * 3,
        out_specs=pl.BlockSpec((tm, D_MODEL), lambda i: (i, 0)),
        out_shape=jax.ShapeDtypeStruct((N_TOK, D_MODEL), BF16),
        compiler_params=_params(("parallel",)), name="merge",
    )(*ys, *yl, *yr, proj_c, proj_d, proj_d, *wb)


def _outproj_kernel(nx, na, m_ref, w_ref, *refs):
    g_ref, oa_ref, ob_ref = refs[nx:]
    x = refs[0][...] if nx == 1 else _pick(N_PROMPT // m_ref.shape[0], refs[0], refs[1])
    x = x + jnp.dot(m_ref[...], w_ref[...], preferred_element_type=F32)
    ms = jnp.mean(x * x, axis=-1, keepdims=True)
    xn = x * lax.rsqrt(ms + NORM_EPS) * g_ref[...]
    if na is None:
        oa_ref[...] = x
        ob_ref[...] = xn.astype(ob_ref.dtype)
    else:
        @pl.when(pl.program_id(0) < na)
        def _():
            oa_ref[...] = xn

        @pl.when(pl.program_id(0) >= na)
        def _():
            ob_ref[...] = xn


def _outproj_call(merged, w_out, x, g_next, final, tm=512):
    tok = pl.BlockSpec((tm, D_MODEL), lambda i: (i, 0))
    x = x if isinstance(x, tuple) else (x,)
    x_specs = [tok] if len(x) == 1 else _two_part_specs(tm, D_MODEL)[0]
    if final:
        out_specs, na = _two_part_specs(tm, D_MODEL)
        out_shape = [jax.ShapeDtypeStruct((N_PROMPT, D_MODEL), F32), jax.ShapeDtypeStruct((N_SAMPLE, D_MODEL), F32)]
    else:
        out_specs, na = [tok, tok], None
        out_shape = [jax.ShapeDtypeStruct((N_TOK, D_MODEL), F32), jax.ShapeDtypeStruct((N_TOK, D_MODEL), BF16)]
    return pl.pallas_call(
        functools.partial(_outproj_kernel, len(x), na), grid=(N_TOK // tm,),
        in_specs=[tok, _const_spec((D_MODEL, D_MODEL))] + x_specs + [_const_spec((1, D_MODEL))],
        out_specs=out_specs, out_shape=out_shape,
        compiler_params=_params(("arbitrary",)), name="outproj",
    )(merged, w_out, *x, g_next.reshape(1, D_MODEL))


def kernel(x_prompt, x_sample, state_s5_re, state_s5_im, state_lru, state_conv, state_ret, norm_g, w_in, s5_lambda_re, s5_lambda_im, s5_log_dt, s5_b_re, s5_b_im, s5_c_re, s5_c_im, s5_d, s5_w_glu, s5_b_glu, lru_conv_w, lru_conv_b, lru_w_a, lru_b_a, lru_w_x, lru_b_x, lru_lambda, ret_gn_g, w_branch_s5, w_branch_lru, w_branch_ret, w_out, final_norm_g):
    x = _to_rows(x_prompt, x_sample)
    xn = _rmsnorm_call(*x, norm_g[0])
    outs_p = [[] for _ in range(5)]
    outs_s = [[] for _ in range(4)]
    r_s = None
    for l in range(DEPTH):
        (*wb, wo, wglu), proj_a = _inproj_call(
            "inproj_a_cast", xn, w_in, l, 0,
            **_cast_mixer(l, (w_branch_s5, w_branch_lru, w_branch_ret, w_out, s5_w_glu),
                          PROJ_GROUPS[0][1] * MM_UNITS))
        s5_w = (_s5_prep(s5_lambda_re[l], s5_lambda_im[l], s5_log_dt[l], s5_b_re[l], s5_b_im[l],
                         s5_c_re[l], s5_c_im[l]), s5_d[l], wglu, s5_b_glu[l])
        wg = jnp.concatenate([lru_w_a[l], lru_w_x[l]], axis=-1).astype(BF16)
        lru_w = _lru_weights(lru_conv_w[l], lru_conv_b[l], wg, lru_b_a[l], lru_b_x[l], lru_lambda[l])
        cbuf = state_conv[l].transpose(1, 0, 2).reshape((CONV_WIDTH - 1) * DEC_BATCH, LRU_WIDTH)

        (ys_p, hr_p, hi_p), proj_b = _inproj_call("inproj_b_s5", xn, w_in, l, 1, **_s5_prompt_mixer(proj_a, *s5_w))
        ys_s, hr_s, hi_s = _s5_sample_call(proj_a, *s5_w, state_s5_re[l].reshape(DEC_BATCH, S5_LANES),
                                           state_s5_im[l].reshape(DEC_BATCH, S5_LANES))
        (yl_p, hl_p, conv_p), proj_c = _inproj_call("inproj_c_lru", xn, w_in, l, 2,
                                                    **_lru_prompt_mixer(proj_b, lru_w))
        conv_p = _lru_prompt_conv_state(conv_p)
        yl_s, hl_s, conv_s = _lru_sample_call(proj_b, lru_w, state_lru[l], cbuf)
        (yr_p, r_p), proj_d = _inproj_call("inproj_d_ret", xn, w_in, l, 3,
                                           **_ret_prompt_mixer(proj_b, proj_c, ret_gn_g[l]))
        yr_s, r_s = _ret_sample_call(proj_b, proj_c, ret_gn_g[l], state_ret, l, r_s)

        merged = _merge_call((ys_p, ys_s), (yl_p, yl_s), (yr_p, yr_s), proj_c, proj_d, wb)
        final = l == DEPTH - 1
        g_next = final_norm_g if final else norm_g[l + 1]
        x, xn = _outproj_call(merged, wo, x, g_next, final)

        st = (S5_GROUPS, S5_STATE)
        for lst, vals in ((outs_p, (hr_p.reshape(BATCH, *st), hi_p.reshape(BATCH, *st),
                                    hl_p.reshape(BATCH, LRU_WIDTH), conv_p, r_p)),
                          (outs_s, (hr_s.reshape(DEC_BATCH, *st), hi_s.reshape(DEC_BATCH, *st), hl_s, conv_s))):
            for j, v in enumerate(vals):
                lst[j].append(v)

    y_prompt, y_sample = _from_rows(x, xn)
    sp = [jnp.stack(t, axis=0) for t in outs_p]
    ss = [jnp.stack(t, axis=0) for t in outs_s]
    return (y_prompt, y_sample, *sp, *ss, r_s)
```

```python
import functools

import jax
import jax.numpy as jnp
from jax import lax
from jax.experimental import pallas as pl
from jax.experimental.pallas import tpu as pltpu

F32 = jnp.float32
BF16 = jnp.bfloat16

D_MODEL = 2048
BATCH = 4
SEQ = 2048
DEPTH = 2
DEC_BATCH = 128
DEC_SEQ = 8
PAST_LEN = 16384
S5_WIDTH = 1024
S5_GROUP = 16
S5_GROUPS = 64
S5_STATE = 64
S5_LANES = S5_GROUPS * S5_STATE
LRU_WIDTH = 1024
LRU_BLOCKS = 8
LRU_BLOCK = 128
CONV_WIDTH = 4
LRU_C = 8.0
RET_HEADS = 8
RET_DK = 128
RET_DV = 128
RET_WIDTH = 1024
ROPE_BASE = 10000.0
NORM_EPS = 1e-6
GN_EPS = 1e-5
N_IN = 14336

N_PROMPT = BATCH * SEQ
N_SAMPLE = DEC_BATCH * DEC_SEQ
N_TOK = N_PROMPT + N_SAMPLE

SUBLANES = 8
LANES = 128
VMEM_LIMIT = 56 * 1024 * 1024

TILE = 256
SEG = TILE // SUBLANES
TILES = SEQ // TILE

PROJ_GROUPS = ((0, 2), (2, 4), (6, 4), (10, 4))
MM_COLS = 1024
MM_UNITS = TILES
MM_ROWS = N_TOK // MM_UNITS
MM_COL_PIECES = 2
MM_PIECE_COLS = MM_COLS // MM_COL_PIECES
MM_ROW_PIECES = 4
MM_PIECE_ROWS = MM_ROWS // MM_ROW_PIECES
MM_PIECES = MM_COL_PIECES * MM_ROW_PIECES

S5_KB = 4
S5_KW = S5_WIDTH // S5_KB
S5_NW = S5_LANES // S5_KB
SCAN_W = 512

SAMPLE_ROW0 = N_PROMPT // DEC_BATCH


def _params(sem, vmem=VMEM_LIMIT):
    return pltpu.CompilerParams(dimension_semantics=sem, vmem_limit_bytes=vmem)


def _const_spec(shape, single=False):
    return pl.BlockSpec(shape, lambda *_: (0,) * len(shape), pipeline_mode=pl.Buffered(1) if single else None)


def _sigmoid(x):
    return jax.nn.sigmoid(x)


def _silu(x):
    return x * jax.nn.sigmoid(x)


def _bcast_row(x, row):
    return jnp.broadcast_to(x[row:row + 1, :], x.shape)


def _cmul(ar, ai, br, bi):
    return ar * br - ai * bi, ar * bi + ai * br


def _to_rows(x_prompt, x_sample):
    xp = x_prompt.reshape(BATCH, TILES, SUBLANES, SEG, -1).transpose(0, 1, 3, 2, 4).reshape(N_PROMPT, -1)
    xs = x_sample.transpose(1, 0, 2).reshape(N_SAMPLE, -1)
    return xp, xs


def _from_rows(yp, ys):
    yp = yp.reshape(BATCH, TILES, SEG, SUBLANES, -1).transpose(0, 1, 3, 2, 4).reshape(BATCH, SEQ, -1)
    ys = ys.reshape(DEC_SEQ, DEC_BATCH, -1).transpose(1, 0, 2)
    return yp, ys


def _two_part_specs(tm, width):
    na = N_PROMPT // tm
    return [pl.BlockSpec((tm, width), lambda i: (jnp.minimum(i, na - 1), 0)),
            pl.BlockSpec((tm, width), lambda i: (jnp.maximum(i - na, 0), 0))], na


def _pick(na, a_ref, b_ref):
    return jnp.where(pl.program_id(0) < na, a_ref[...], b_ref[...])


def _norm_kernel(na, xa_ref, xb_ref, g_ref, o_ref):
    x = _pick(na, xa_ref, xb_ref)
    ms = jnp.mean(x * x, axis=-1, keepdims=True)
    o_ref[...] = (x * lax.rsqrt(ms + NORM_EPS) * g_ref[...]).astype(o_ref.dtype)


def _rmsnorm_call(xp, xs, g, tm=1024):
    specs, na = _two_part_specs(tm, D_MODEL)
    return pl.pallas_call(
        functools.partial(_norm_kernel, na),
        grid=(N_TOK // tm,),
        in_specs=specs + [_const_spec((1, D_MODEL))],
        out_specs=pl.BlockSpec((tm, D_MODEL), lambda i: (i, 0)),
        out_shape=jax.ShapeDtypeStruct((N_TOK, D_MODEL), BF16),
        compiler_params=_params(("parallel",)),
        name="rmsnorm",
    )(xp, xs, g.reshape(1, D_MODEL))


def _prep_kernel(n_cast, mm_piece, *refs):
    n_in = n_cast + 3
    ins, outs = refs[:n_in], refs[n_in:]
    jobs = n_cast + 2

    def pieces(k):
        for p in range(k * MM_PIECES // jobs, (k + 1) * MM_PIECES // jobs):
            mm_piece(p)

    for k in range(n_cast):
        pieces(k)
        outs[k][...] = ins[k][...].astype(BF16)

    def group_mask(nrows, rows_per_group, period, lanes_per_group, j):
        row0 = pl.program_id(0) * nrows
        row = row0 + lax.broadcasted_iota(jnp.int32, (nrows, LANES), 0)
        g_row = jnp.right_shift(jnp.bitwise_and(row, period - 1), rows_per_group.bit_length() - 1)
        g_lane = jnp.right_shift(lax.broadcasted_iota(jnp.int32, (nrows, LANES), 1), lanes_per_group.bit_length() - 1)
        return g_row == (LANES // lanes_per_group) * j + g_lane

    tbr_ref, tbi_ref, tc_ref = ins[n_cast:]
    ob_ref, oc_ref = outs[n_cast:]
    pieces(n_cast)
    tr, ti = tbr_ref[...], tbi_ref[...]
    for j in range(S5_NW // LANES):
        m = group_mask(tr.shape[0], S5_GROUP, S5_KW, S5_STATE, j)
        ob_ref[:, j * LANES:(j + 1) * LANES] = jnp.where(m, tr, 0.0).astype(BF16)
        ob_ref[:, S5_NW + j * LANES:S5_NW + (j + 1) * LANES] = jnp.where(m, ti, 0.0).astype(BF16)
    pieces(n_cast + 1)
    tc = tc_ref[...]
    for j in range(S5_KW // LANES):
        m = group_mask(tc.shape[0], S5_STATE, S5_NW, S5_GROUP, j)
        oc_ref[:, j * LANES:(j + 1) * LANES] = jnp.where(m, tc, 0.0).astype(BF16)


def _prep_mixer(layer, weights, compact, steps):
    shapes = [w.shape[1:] for w in weights]
    tbr, tbi, tc = compact
    row_spec = lambda a: pl.BlockSpec((a.shape[0] // steps, a.shape[1]), lambda s: (s, 0))
    nb, nc = S5_KB * S5_KW, S5_KB * 2 * S5_NW
    return dict(
        mixer=functools.partial(_prep_kernel, len(weights)),
        in_specs=[pl.BlockSpec((None, r // steps, c), lambda s: (layer, s, 0)) for r, c in shapes]
                 + [row_spec(tbr), row_spec(tbi), row_spec(tc)],
        args=list(weights) + [tbr, tbi, tc],
        out_specs=[pl.BlockSpec((r // steps, c), lambda s: (s, 0)) for r, c in shapes]
                  + [pl.BlockSpec((nb // steps, 2 * S5_NW), lambda s: (s, 0)),
                     pl.BlockSpec((nc // steps, S5_KW), lambda s: (s, 0))],
        out_shape=[jax.ShapeDtypeStruct((r, c), BF16) for r, c in shapes]
                  + [jax.ShapeDtypeStruct((nb, 2 * S5_NW), BF16), jax.ShapeDtypeStruct((nc, S5_KW), BF16)])


def _inproj_kernel(mixer, n_in, n_out, *refs):
    xn_ref, w_ref = refs[n_in:n_in + 2]
    o_ref = refs[n_in + 2 + n_out]
    wbf_ref = refs[n_in + 3 + n_out]

    @pl.when(pl.program_id(0) % MM_UNITS == 0)
    def _():
        wbf_ref[...] = w_ref[...].astype(BF16)

    def piece(k):
        rows = slice((k // MM_COL_PIECES) * MM_PIECE_ROWS, (k // MM_COL_PIECES + 1) * MM_PIECE_ROWS)
        cols = slice((k % MM_COL_PIECES) * MM_PIECE_COLS, (k % MM_COL_PIECES + 1) * MM_PIECE_COLS)
        o_ref[rows, cols] = jnp.dot(xn_ref[rows, :], wbf_ref[:, cols],
                                    preferred_element_type=F32).astype(o_ref.dtype)

    if mixer is None:
        for k in range(MM_PIECES):
            piece(k)
    else:
        mixer(piece, *refs[:n_in], *refs[n_in + 2:n_in + 2 + n_out], *refs[n_in + 4 + n_out:])


def _inproj_call(name, xn, w_in, layer, group, mixer=None, in_specs=(), args=(), out_specs=(), out_shape=(),
                 scratch=()):
    first, tiles = PROJ_GROUPS[group]
    mm_in = [pl.BlockSpec((MM_ROWS, D_MODEL), lambda s: (s % MM_UNITS, 0)),
             pl.BlockSpec((None, D_MODEL, MM_COLS), lambda s: (layer, 0, first + s // MM_UNITS))]
    mm_out = pl.BlockSpec((MM_ROWS, MM_COLS), lambda s: (s % MM_UNITS, s // MM_UNITS))
    res = pl.pallas_call(
        functools.partial(_inproj_kernel, mixer, len(in_specs), len(out_specs)),
        grid=(tiles * MM_UNITS,),
        in_specs=list(in_specs) + mm_in,
        out_specs=list(out_specs) + [mm_out],
        out_shape=list(out_shape) + [jax.ShapeDtypeStruct((N_TOK, tiles * MM_COLS), BF16)],
        scratch_shapes=[pltpu.VMEM((D_MODEL, MM_COLS), BF16)] + list(scratch),
        compiler_params=_params(("arbitrary",)),
        name=name,
    )(*args, xn, w_in)
    return res[:-1], res[-1]


def _s5_prep(lam_re, lam_im, log_dt, b_re, b_im, c_re, c_im):
    dt = jnp.exp(log_dt)[:, None]
    e = jnp.exp(lam_re * dt)
    lbr = e * jnp.cos(lam_im * dt)
    lbi = e * jnp.sin(lam_im * dt)
    nr, ni = lbr - 1.0, lbi
    den = lam_re * lam_re + lam_im * lam_im
    cr = (nr * lam_re + ni * lam_im) / den
    ci = (ni * lam_re - nr * lam_im) / den
    bbr = cr[..., None] * b_re - ci[..., None] * b_im
    bbi = cr[..., None] * b_im + ci[..., None] * b_re
    gpb = S5_GROUPS // S5_KB

    def drive(bb):
        t = bb.reshape(S5_KB, gpb, S5_STATE, S5_GROUP).transpose(0, 1, 3, 2).reshape(S5_KB * S5_KW, S5_STATE)
        return jnp.tile(t, (1, LANES // S5_STATE))

    t = jnp.stack([c_re, -c_im]).reshape(2, S5_KB, gpb, S5_GROUP, S5_STATE).transpose(1, 0, 2, 4, 3)
    compact = (drive(bbr), drive(bbi), jnp.tile(t.reshape(S5_KB * 2 * S5_NW, S5_GROUP), (1, LANES // S5_GROUP)))

    lr, li = lbr.reshape(-1), lbi.reshape(-1)
    full = lambda v: jnp.broadcast_to(v, (SUBLANES, S5_LANES))
    lam = jnp.stack([full(lr), full(li)])
    sr, si = lr, li
    for _ in range(SEG.bit_length() - 1):
        sr, si = _cmul(sr, si, sr, si)
    pr, pi = [sr], [si]
    for _ in range(SUBLANES - 1):
        r_, i_ = _cmul(pr[-1], pi[-1], sr, si)
        pr.append(r_)
        pi.append(i_)
    row = jnp.arange(SUBLANES)[:, None]
    ak = jnp.stack([jnp.stack([jnp.where(row >= k, pr[k - 1][None, :], 0.0),
                               jnp.where(row >= k, pi[k - 1][None, :], 0.0)]) for k in (1, 2, 4)])
    pw = jnp.stack([jnp.stack(pr), jnp.stack(pi)])
    return compact, (lam, ak, pw)


def _s5_drive_block(kb, u, bblk_ref, scr):
    res = jnp.dot(u[:, kb * S5_KW:(kb + 1) * S5_KW], bblk_ref[kb], preferred_element_type=F32)
    scr[:, kb * S5_NW:(kb + 1) * S5_NW] = res[:, :S5_NW]
    scr[:, S5_LANES + kb * S5_NW:S5_LANES + (kb + 1) * S5_NW] = res[:, S5_NW:]


def _s5_readout_block(kb, scr, cblk_ref):
    hcat = jnp.concatenate(
        [scr[:, kb * S5_NW:(kb + 1) * S5_NW],
         scr[:, S5_LANES + kb * S5_NW:S5_LANES + (kb + 1) * S5_NW]], axis=1).astype(BF16)
    return jnp.dot(hcat, cblk_ref[kb], preferred_element_type=F32)


def _s5_finish(parts, u, z, d_ref, wglu_ref, bglu_ref):
    y = jnp.concatenate(parts, axis=1) + d_ref[...] * u.astype(F32)
    y = jax.nn.gelu(y, approximate=True)
    glu = jnp.dot(y.astype(BF16), wglu_ref[...], preferred_element_type=F32) + bglu_ref[...]
    y = y * _sigmoid(glu)
    return (y * _silu(z.astype(F32))).astype(BF16)


def _s5_prompt_kernel(mm_piece, u_ref, z_ref, bblk_ref, cblk_ref, lam_ref, ak_ref, pw_ref, d_ref, wglu_ref,
                      bglu_ref, y_ref, hr_ref, hi_ref, scr, car):
    t = pl.program_id(0) % TILES

    @pl.when(t == 0)
    def _():
        car[...] = jnp.zeros_like(car)

    w = SCAN_W
    rowi = lax.broadcasted_iota(jnp.int32, (SUBLANES, w), 0)
    u = u_ref[...]
    parts = []
    for kb in range(S5_KB):
        _s5_drive_block(kb, u, bblk_ref, scr)
        for c in range(kb * (S5_NW // w), (kb + 1) * (S5_NW // w)):
            mm_piece(c)
            sl_re = slice(c * w, (c + 1) * w)
            sl_im = slice(S5_LANES + c * w, S5_LANES + (c + 1) * w)
            lr, li = lam_ref[0, :, sl_re], lam_ref[1, :, sl_re]

            xr = xi = jnp.zeros((SUBLANES, w), F32)
            for r in range(SEG):
                rows = slice(r * SUBLANES, (r + 1) * SUBLANES)
                pr, pi = _cmul(lr, li, xr, xi)
                xr = scr[rows, sl_re] + pr
                xi = scr[rows, sl_im] + pi
                scr[rows, sl_re] = xr
                scr[rows, sl_im] = xi

            for k, idx in zip((1, 2, 4), range(3)):
                pr, pi = _cmul(ak_ref[idx, 0, :, sl_re], ak_ref[idx, 1, :, sl_re],
                               pltpu.roll(xr, k, 0), pltpu.roll(xi, k, 0))
                xr, xi = xr + pr, xi + pi
            cr, ci = car[0, :, sl_re], car[1, :, sl_re]
            pr, pi = _cmul(pw_ref[0, :, sl_re], pw_ref[1, :, sl_re], cr, ci)
            fr, fi = xr + pr, xi + pi
            dr = jnp.where(rowi == 0, cr, pltpu.roll(fr, 1, 0))
            di = jnp.where(rowi == 0, ci, pltpu.roll(fi, 1, 0))
            car[0, :, sl_re] = _bcast_row(fr, SUBLANES - 1)
            car[1, :, sl_re] = _bcast_row(fi, SUBLANES - 1)

            for r in range(SEG):
                rows = slice(r * SUBLANES, (r + 1) * SUBLANES)
                dr, di = _cmul(lr, li, dr, di)
                scr[rows, sl_re] = scr[rows, sl_re] + dr
                scr[rows, sl_im] = scr[rows, sl_im] + di
        parts.append(_s5_readout_block(kb, scr, cblk_ref))

    y_ref[...] = _s5_finish(parts, u, z_ref[...], d_ref, wglu_ref, bglu_ref)

    @pl.when(t == TILES - 1)
    def _():
        hr_ref[0] = car[0, 0:1, :]
        hi_ref[0] = car[1, 0:1, :]


S5_SB = 32


def _s5_sample_kernel(u_ref, z_ref, h0r_ref, h0i_ref, bblk_ref, cblk_ref, lam_ref, d_ref, wglu_ref, bglu_ref,
                      y_ref, hr_ref, hi_ref, scr):
    rows = DEC_SEQ * S5_SB
    u = u_ref[...].reshape(rows, S5_WIDTH)
    w = SCAN_W
    parts = []
    for kb in range(S5_KB):
        _s5_drive_block(kb, u, bblk_ref, scr)
        for c in range(kb * (S5_NW // w), (kb + 1) * (S5_NW // w)):
            sl_re = slice(c * w, (c + 1) * w)
            sl_im = slice(S5_LANES + c * w, S5_LANES + (c + 1) * w)
            lr, li = lam_ref[0, :, sl_re], lam_ref[1, :, sl_re]
            for g in range(S5_SB // SUBLANES):
                seqs = slice(g * SUBLANES, (g + 1) * SUBLANES)
                hr, hi = h0r_ref[seqs, sl_re], h0i_ref[seqs, sl_re]
                for j in range(DEC_SEQ):
                    rows_j = slice(j * S5_SB + g * SUBLANES, j * S5_SB + (g + 1) * SUBLANES)
                    pr, pi = _cmul(lr, li, hr, hi)
                    hr = scr[rows_j, sl_re] + pr
                    hi = scr[rows_j, sl_im] + pi
                    scr[rows_j, sl_re] = hr
                    scr[rows_j, sl_im] = hi
                hr_ref[seqs, sl_re] = hr
                hi_ref[seqs, sl_re] = hi
        parts.append(_s5_readout_block(kb, scr, cblk_ref))

    y = _s5_finish(parts, u, z_ref[...].reshape(rows, S5_WIDTH), d_ref, wglu_ref, bglu_ref)
    y_ref[...] = y.reshape(DEC_SEQ, S5_SB, S5_WIDTH)


def _tile_spec(col):
    return pl.BlockSpec((TILE, 1024), lambda s: (s, col))


def _state_spec(*shape):
    return pl.BlockSpec((1,) + shape, lambda s: (s // TILES,) + (0,) * len(shape))


def _sample3(proj):
    return proj.reshape(N_TOK // DEC_BATCH, DEC_BATCH, proj.shape[-1])


def _sample_spec(nseq, col):
    return pl.BlockSpec((DEC_SEQ, nseq, 1024), lambda i: (SAMPLE_ROW0 // DEC_SEQ, i, col))


def _s5_specs(prep, d, wglu, bglu, single):
    b_blk, c_blk, lam, ak, pw = prep
    spec = functools.partial(_const_spec, single=single)
    lam_spec = spec((2, SUBLANES, S5_LANES))
    mm = ([spec((S5_KB, S5_KW, 2 * S5_NW)), spec((S5_KB, 2 * S5_NW, S5_KW))], [b_blk, c_blk])
    chain = ([spec((3, 2, SUBLANES, S5_LANES)), lam_spec], [ak, pw])
    tail = ([spec((1, S5_WIDTH)), spec((S5_WIDTH, S5_WIDTH)), spec((1, S5_WIDTH))],
            [d.reshape(1, -1), wglu, bglu.reshape(1, -1)])
    return mm, (lam_spec, lam), chain, tail


def _s5_prompt_mixer(proj_a, prep, d, wglu, bglu):
    mm, (lam_spec, lam), chain, tail = _s5_specs(prep, d, wglu, bglu, single=True)
    return dict(
        mixer=_s5_prompt_kernel,
        in_specs=[_tile_spec(0), _tile_spec(1)] + mm[0] + [lam_spec] + chain[0] + tail[0],
        args=[proj_a, proj_a] + mm[1] + [lam] + chain[1] + tail[1],
        out_specs=[_tile_spec(0), _state_spec(1, S5_LANES), _state_spec(1, S5_LANES)],
        out_shape=[jax.ShapeDtypeStruct((N_PROMPT, 1024), BF16),
                   jax.ShapeDtypeStruct((BATCH, 1, S5_LANES), F32),
                   jax.ShapeDtypeStruct((BATCH, 1, S5_LANES), F32)],
        scratch=[pltpu.VMEM((TILE, 2 * S5_LANES), F32), pltpu.VMEM((2, SUBLANES, S5_LANES), F32)])


def _s5_sample_call(proj_a, prep, d, wglu, bglu, h0r, h0i):
    mm, (lam_spec, lam), _, tail = _s5_specs(prep, d, wglu, bglu, single=False)
    proj3 = _sample3(proj_a)
    st = pl.BlockSpec((S5_SB, S5_LANES), lambda i: (i, 0))
    ys, hr_s, hi_s = pl.pallas_call(
        _s5_sample_kernel, grid=(DEC_BATCH // S5_SB,),
        in_specs=[_sample_spec(S5_SB, 0), _sample_spec(S5_SB, 1), st, st] + mm[0] + [lam_spec] + tail[0],
        out_specs=[pl.BlockSpec((DEC_SEQ, S5_SB, 1024), lambda i: (0, i, 0)), st, st],
        out_shape=[jax.ShapeDtypeStruct((DEC_SEQ, DEC_BATCH, 1024), BF16),
                   jax.ShapeDtypeStruct((DEC_BATCH, S5_LANES), F32),
                   jax.ShapeDtypeStruct((DEC_BATCH, S5_LANES), F32)],
        scratch_shapes=[pltpu.VMEM((DEC_SEQ * S5_SB, 2 * S5_LANES), F32)],
        compiler_params=_params(("parallel",)), name="s5_sample",
    )(proj3, proj3, h0r, h0i, *mm[1], lam, *tail[1])
    return ys.reshape(N_SAMPLE, 1024), hr_s, hi_s


def _lru_gates_block(n, xc, wg_ref, ba_ref, bx_ref, lam_ref):
    sl = slice(n * LRU_BLOCK, (n + 1) * LRU_BLOCK)
    xb = xc[:, sl]
    res = jnp.dot(xb.astype(BF16), wg_ref[n], preferred_element_type=F32)
    r_gate = _sigmoid(res[:, :LRU_BLOCK] + ba_ref[:, sl])
    i_gate = _sigmoid(res[:, LRU_BLOCK:] + bx_ref[:, sl])
    nl = -lam_ref[:, sl]
    softplus = jnp.maximum(nl, 0.0) + jnp.log1p(jnp.exp(-jnp.abs(nl)))
    a = jnp.exp(-LRU_C * r_gate * softplus)
    return a, jnp.sqrt(1.0 - a * a) * (i_gate * xb)


def _lru_gates(xc, wg_ref, ba_ref, bx_ref, lam_ref):
    ab = [_lru_gates_block(n, xc, wg_ref, ba_ref, bx_ref, lam_ref) for n in range(LRU_BLOCKS)]
    return jnp.concatenate([a for a, _ in ab], axis=1), jnp.concatenate([b for _, b in ab], axis=1)


def _lru_prompt_kernel(mm_piece, u_ref, z_ref, cw_ref, cb_ref, wg_ref, ba_ref, bx_ref, lam_ref,
                       y_ref, hl_ref, cv_ref, a_scr, b_scr, tail, car):
    t = pl.program_id(0) % TILES
    nwrap = CONV_WIDTH - 1

    @pl.when(t == 0)
    def _():
        tail[...] = jnp.zeros_like(tail)
        car[...] = jnp.zeros_like(car)

    x = u_ref[...].astype(F32)
    rowi = lax.broadcasted_iota(jnp.int32, (SUBLANES, LRU_WIDTH), 0)
    wrap = []
    for k in range(nwrap):
        cur = x[TILE - (nwrap - k) * SUBLANES:TILE - (nwrap - k - 1) * SUBLANES, :]
        prev = tail[k * SUBLANES:(k + 1) * SUBLANES, :]
        wrap.append(jnp.where(rowi == 0, pltpu.roll(prev, 1, 0), pltpu.roll(cur, 1, 0)))
    xc = cw_ref[nwrap:nwrap + 1, :] * x + cb_ref[...]
    for s in range(1, CONV_WIDTH):
        shifted = jnp.concatenate(wrap[nwrap - s:] + [x[:TILE - s * SUBLANES, :]], axis=0)
        xc = xc + cw_ref[nwrap - s:nwrap - s + 1, :] * shifted
    tail[...] = x[TILE - nwrap * SUBLANES:, :]

    for n in range(LRU_BLOCKS):
        mm_piece(n)
        sl = slice(n * LRU_BLOCK, (n + 1) * LRU_BLOCK)
        a_scr[:, sl], b_scr[:, sl] = _lru_gates_block(n, xc, wg_ref, ba_ref, bx_ref, lam_ref)

    w = SCAN_W
    rw = lax.broadcasted_iota(jnp.int32, (SUBLANES, w), 0)
    for c in range(LRU_WIDTH // w):
        sl = slice(c * w, (c + 1) * w)

        bv = jnp.zeros((SUBLANES, w), F32)
        av = jnp.ones((SUBLANES, w), F32)
        for r in range(SEG):
            rows = slice(r * SUBLANES, (r + 1) * SUBLANES)
            ar = a_scr[rows, sl]
            bv = ar * bv + b_scr[rows, sl]
            av = ar * av
            b_scr[rows, sl] = bv
            a_scr[rows, sl] = av

        for k in (1, 2, 4):
            sa = jnp.where(rw >= k, pltpu.roll(av, k, 0), 1.0)
            sb = jnp.where(rw >= k, pltpu.roll(bv, k, 0), 0.0)
            bv = bv + av * sb
            av = av * sa
        cv = car[:, sl]
        full = bv + av * cv
        enter = jnp.where(rw == 0, cv, pltpu.roll(full, 1, 0))
        car[:, sl] = _bcast_row(full, SUBLANES - 1)

        for r in range(SEG):
            rows = slice(r * SUBLANES, (r + 1) * SUBLANES)
            b_scr[rows, sl] = b_scr[rows, sl] + a_scr[rows, sl] * enter

    y_ref[...] = (b_scr[...] * _silu(z_ref[...].astype(F32))).astype(y_ref.dtype)

    @pl.when(t == TILES - 1)
    def _():
        hl_ref[0] = car[0:1, :]
        cv_ref[0] = tail[...]


def _lru_sample_kernel(u_ref, z_ref, h0_ref, cbuf_ref, cw_ref, cb_ref, wg_ref, ba_ref, bx_ref, lam_ref,
                       y_ref, hl_ref, cv_ref, ext):
    nb = DEC_BATCH
    nwrap = CONV_WIDTH - 1
    ext[0:nwrap * nb, :] = cbuf_ref[...]
    ext[nwrap * nb:, :] = u_ref[...].astype(F32)
    h = h0_ref[...]
    for j in range(DEC_SEQ):
        xc = cb_ref[...]
        for k in range(CONV_WIDTH):
            xc = xc + cw_ref[k:k + 1, :] * ext[(j + k) * nb:(j + k + 1) * nb, :]
        a, b = _lru_gates(xc, wg_ref, ba_ref, bx_ref, lam_ref)
        h = a * h + b
        y_ref[j * nb:(j + 1) * nb, :] = (h * _silu(z_ref[j * nb:(j + 1) * nb, :].astype(F32))).astype(y_ref.dtype)
    hl_ref[...] = h
    cv_ref[...] = ext[DEC_SEQ * nb:, :]


def _lru_weights(conv_w, conv_b, wg, b_a, b_x, lam):
    specs = [_const_spec((CONV_WIDTH, LRU_WIDTH)), _const_spec((1, LRU_WIDTH)),
             _const_spec((LRU_BLOCKS, LRU_BLOCK, 2 * LRU_BLOCK)),
             _const_spec((1, LRU_WIDTH)), _const_spec((1, LRU_WIDTH)), _const_spec((1, LRU_WIDTH))]
    args = [conv_w, conv_b.reshape(1, -1), wg, b_a.reshape(1, -1), b_x.reshape(1, -1), lam.reshape(1, -1)]
    return specs, args


def _lru_prompt_mixer(proj_b, weights):
    w_specs, w_args = weights
    nrows = (CONV_WIDTH - 1) * SUBLANES
    return dict(
        mixer=_lru_prompt_kernel,
        in_specs=[_tile_spec(0), _tile_spec(1)] + w_specs,
        args=[proj_b, proj_b] + w_args,
        out_specs=[_tile_spec(0), _state_spec(1, LRU_WIDTH), _state_spec(nrows, LRU_WIDTH)],
        out_shape=[jax.ShapeDtypeStruct((N_PROMPT, 1024), BF16),
                   jax.ShapeDtypeStruct((BATCH, 1, LRU_WIDTH), F32),
                   jax.ShapeDtypeStruct((BATCH, nrows, LRU_WIDTH), F32)],
        scratch=[pltpu.VMEM((TILE, LRU_WIDTH), F32), pltpu.VMEM((TILE, LRU_WIDTH), F32),
                 pltpu.VMEM((nrows, LRU_WIDTH), F32), pltpu.VMEM((SUBLANES, LRU_WIDTH), F32)])


def _lru_prompt_conv_state(cv_p):
    return cv_p.reshape(BATCH, CONV_WIDTH - 1, SUBLANES, LRU_WIDTH)[:, :, SUBLANES - 1, :]


def _lru_sample_call(proj_b, weights, h0, cbuf):
    w_specs, w_args = weights
    srow = N_PROMPT // N_SAMPLE
    ys, hl_s, cv_s = pl.pallas_call(
        _lru_sample_kernel, grid=(1,),
        in_specs=[pl.BlockSpec((N_SAMPLE, 1024), lambda i: (srow, 0)),
                  pl.BlockSpec((N_SAMPLE, 1024), lambda i: (srow, 1)),
                  _const_spec((DEC_BATCH, LRU_WIDTH)),
                  _const_spec(((CONV_WIDTH - 1) * DEC_BATCH, LRU_WIDTH))] + w_specs,
        out_specs=[_const_spec((N_SAMPLE, 1024)), _const_spec((DEC_BATCH, LRU_WIDTH)),
                   _const_spec(((CONV_WIDTH - 1) * DEC_BATCH, LRU_WIDTH))],
        out_shape=[jax.ShapeDtypeStruct((N_SAMPLE, 1024), BF16),
                   jax.ShapeDtypeStruct((DEC_BATCH, LRU_WIDTH), F32),
                   jax.ShapeDtypeStruct(((CONV_WIDTH - 1) * DEC_BATCH, LRU_WIDTH), F32)],
        scratch_shapes=[pltpu.VMEM(((CONV_WIDTH - 1) * DEC_BATCH + N_SAMPLE, LRU_WIDTH), F32)],
        compiler_params=_params(("arbitrary",)), name="lru_sample",
    )(proj_b, proj_b, h0, cbuf, *w_args)
    return ys, hl_s, cv_s.reshape(CONV_WIDTH - 1, DEC_BATCH, LRU_WIDTH).transpose(1, 0, 2)


RET_SB = 16
RET_SROWS = RET_SB * DEC_SEQ


def _ret_tables(seq, idx):
    n = idx.shape[0]
    chunk = jnp.max(idx) + 1.0
    log_g = jnp.log1p(-jnp.exp2(-5.0 - jnp.arange(RET_HEADS, dtype=F32)))
    diff = idx[:, None] - idx[None, :]
    same = seq[:, None] == seq[None, :]
    dmask = jnp.where((diff[None] >= 0) & same[None],
                      jnp.exp(jnp.maximum(diff, 0.0)[None] * log_g[:, None, None]), 0.0)
    xi = jnp.exp((idx[None, :] + 1.0) * log_g[:, None])
    zeta = jnp.exp((chunk - 1.0 - idx[None, :]) * log_g[:, None])
    gch = jnp.exp(chunk * log_g)
    full = lambda t: jnp.broadcast_to(t[:, :, None], (RET_HEADS, n, LANES))
    gc = jnp.broadcast_to(gch[:, None, None], (RET_HEADS, SUBLANES, LANES))
    return dmask, full(xi), full(zeta), gc


def _rope_tables(pos):
    half = RET_DK // 2
    freq = ROPE_BASE ** (-jnp.arange(half, dtype=F32) / half)
    ang = pos[:, None] * freq[None, :]
    cos, sin = jnp.cos(ang), jnp.sin(ang)
    return jnp.concatenate([cos, cos], axis=-1), jnp.concatenate([-sin, sin], axis=-1)


def _rope(x, cosf, sinf):
    return x * cosf + pltpu.roll(x, RET_DK // 2, 1) * sinf


def _ret_head(h, q, k, v, cos, sin, dmask_ref, zeta_ref):
    sl = slice(h * RET_DK, (h + 1) * RET_DK)
    qb = _rope(q[:, sl].astype(F32), cos, sin).astype(BF16)
    kh = _rope(k[:, sl].astype(F32), cos, sin) * (RET_DK ** -0.5)
    vb = v[:, sl]
    sc = lax.dot_general(qb, kh.astype(BF16), (((1,), (1,)), ((), ())), preferred_element_type=F32)
    inner = jnp.dot((sc * dmask_ref[h]).astype(BF16), vb, preferred_element_type=F32)
    kz = (kh * zeta_ref[h]).astype(BF16)
    return sl, qb, kz, vb, inner


def _ret_finish(o, z, g):
    mu = jnp.mean(o, axis=-1, keepdims=True)
    oc = o - mu
    var = jnp.mean(oc * oc, axis=-1, keepdims=True)
    on = oc * lax.rsqrt(var + GN_EPS) * g
    return (on * _silu(z.astype(F32))).astype(BF16)


def _ret_prompt_kernel(mm_piece, q_ref, k_ref, v_ref, z_ref, cos_ref, sin_ref, dmask_ref, xi_ref, zeta_ref,
                       gc_ref, gng_ref, y_ref, r_ref):
    @pl.when(pl.program_id(0) % TILES == 0)
    def _():
        r_ref[...] = jnp.zeros_like(r_ref)

    cos, sin = cos_ref[...], sin_ref[...]
    for h in range(RET_HEADS):
        mm_piece(h)
        sl, qb, kz, vb, inner = _ret_head(h, q_ref, k_ref, v_ref, cos, sin, dmask_ref, zeta_ref)
        r = r_ref[0, h]
        cross = jnp.dot(qb, r.astype(BF16), preferred_element_type=F32) * xi_ref[h]
        upd = lax.dot_general(kz, vb, (((0,), (0,)), ((), ())), preferred_element_type=F32)
        r_ref[0, h] = r * gc_ref[h, 0:1, :] + upd
        y_ref[:, sl] = _ret_finish(inner + cross, z_ref[:, sl], gng_ref[:, sl])


def _ret_sample_kernel(q_ref, k_ref, v_ref, z_ref, r0_ref, cos_ref, sin_ref, dmask_ref, xi_ref,
                       zeta_ref, gc_ref, gng_ref, *rest):
    y_ref, r_ref = rest[-2:]
    cos, sin = cos_ref[...], sin_ref[...]
    q = q_ref[...].reshape(RET_SROWS, RET_WIDTH)
    k = k_ref[...].reshape(RET_SROWS, RET_WIDTH)
    v = v_ref[...].reshape(RET_SROWS, RET_WIDTH)
    z = z_ref[...].reshape(RET_SROWS, RET_WIDTH)
    rowseq = lax.broadcasted_iota(jnp.int32, (RET_SROWS, RET_DV), 0) % RET_SB
    outs = []
    for h in range(RET_HEADS):
        sl, qb, kz, vb, inner = _ret_head(h, q, k, v, cos, sin, dmask_ref, zeta_ref)
        rcat = jnp.concatenate([r0_ref[s, h] for s in range(RET_SB)], axis=1).astype(BF16)
        call = jnp.dot(qb, rcat, preferred_element_type=F32)
        cross = jnp.zeros((RET_SROWS, RET_DV), F32)
        for s in range(RET_SB):
            cross = jnp.where(rowseq == s, call[:, s * RET_DV:(s + 1) * RET_DV], cross)
        cross = cross * xi_ref[h]
        vf = vb.astype(F32)
        vexp = jnp.concatenate([jnp.where(rowseq == s, vf, 0.0) for s in range(RET_SB)],
                               axis=1).astype(BF16)
        upd = lax.dot_general(kz, vexp, (((0,), (0,)), ((), ())), preferred_element_type=F32)
        gc = gc_ref[h, 0:1, :]
        for s in range(RET_SB):
            r_ref[s, h] = r0_ref[s, h] * gc + upd[:, s * RET_DV:(s + 1) * RET_DV]
        outs.append(_ret_finish(inner + cross, z[:, sl], gng_ref[:, sl]))
    y_ref[...] = jnp.concatenate(outs, axis=1).reshape(DEC_SEQ, RET_SB, RET_WIDTH)


def _ret_table_specs(n):
    return [_const_spec((RET_HEADS, n, n)), _const_spec((RET_HEADS, n, LANES)),
            _const_spec((RET_HEADS, n, LANES)), _const_spec((RET_HEADS, SUBLANES, LANES)),
            _const_spec((1, RET_WIDTH))]


def _ret_prompt_mixer(proj_b, proj_c, gn_g):
    rows = jnp.arange(TILE)
    tok = ((rows % SUBLANES) * SEG + rows // SUBLANES).astype(F32)
    tabs = _ret_tables(jnp.zeros((TILE,), jnp.int32), tok)
    pos = (jnp.arange(TILES, dtype=F32)[:, None] * TILE + tok[None, :]).reshape(SEQ) + 0.0
    cosf, sinf = _rope_tables(pos)
    rope_spec = pl.BlockSpec((TILE, LANES), lambda s: (s % TILES, 0))
    return dict(
        mixer=_ret_prompt_kernel,
        in_specs=[_tile_spec(2), _tile_spec(3), _tile_spec(0), _tile_spec(1), rope_spec, rope_spec]
                 + _ret_table_specs(TILE),
        args=[proj_b, proj_b, proj_c, proj_c, cosf, sinf, *tabs, gn_g.reshape(1, -1)],
        out_specs=[_tile_spec(0), _state_spec(RET_HEADS, RET_DK, RET_DV)],
        out_shape=[jax.ShapeDtypeStruct((N_PROMPT, 1024), BF16),
                   jax.ShapeDtypeStruct((BATCH, RET_HEADS, RET_DK, RET_DV), F32)])


def _ret_sample_call(proj_b, proj_c, gn_g, r0, layer, r_all):
    rows = jnp.arange(RET_SROWS)
    tabs = _ret_tables(rows % RET_SB, (rows // RET_SB).astype(F32))
    cosf, sinf = _rope_tables((rows // RET_SB).astype(F32) + float(PAST_LEN))
    pb3, pc3 = _sample3(proj_b), _sample3(proj_c)
    st_block = (None, RET_SB, RET_HEADS, RET_DK, RET_DV)
    st_spec = pl.BlockSpec(st_block, lambda i: (layer, i, 0, 0, 0))
    in_specs = [_sample_spec(RET_SB, 2), _sample_spec(RET_SB, 3), _sample_spec(RET_SB, 0), _sample_spec(RET_SB, 1),
                st_spec, _const_spec((RET_SROWS, LANES)), _const_spec((RET_SROWS, LANES))] \
        + _ret_table_specs(RET_SROWS)
    args = [pb3, pb3, pc3, pc3, r0, cosf, sinf, *tabs, gn_g.reshape(1, -1)]
    aliases = {}
    if r_all is not None:
        in_specs.append(pl.BlockSpec(memory_space=pl.ANY))
        args.append(r_all)
        aliases = {len(args) - 1: 1}
    ys, r_all = pl.pallas_call(
        _ret_sample_kernel, grid=(DEC_BATCH // RET_SB,),
        in_specs=in_specs,
        out_specs=[pl.BlockSpec((DEC_SEQ, RET_SB, 1024), lambda i: (0, i, 0)), st_spec],
        out_shape=[jax.ShapeDtypeStruct((DEC_SEQ, DEC_BATCH, 1024), BF16),
                   jax.ShapeDtypeStruct((DEPTH, DEC_BATCH, RET_HEADS, RET_DK, RET_DV), F32)],
        input_output_aliases=aliases,
        compiler_params=_params(("parallel",)), name="ret_sample",
    )(*args)
    return ys.reshape(N_SAMPLE, 1024), r_all


def _merge_kernel(na, ysp, yss, ylp, yls, yrp, yrs, gs_ref, gl_ref, gr_ref, ws_ref, wl_ref, wr_ref, o_ref):
    acc = None
    for a_ref, b_ref, g_ref, w_ref in ((ysp, yss, gs_ref, ws_ref), (ylp, yls, gl_ref, wl_ref),
                                       (yrp, yrs, gr_ref, wr_ref)):
        b = jnp.dot(_pick(na, a_ref, b_ref), w_ref[...], preferred_element_type=F32)
        term = _sigmoid(g_ref[...].astype(F32)) * b
        acc = term if acc is None else acc + term
    o_ref[...] = acc.astype(o_ref.dtype)


def _merge_call(ys, yl, yr, proj_c, proj_d, wb, tm=512):
    specs, na = _two_part_specs(tm, 1024)
    gsp = lambda j: pl.BlockSpec((tm, D_MODEL), lambda i: (i, j))
    return pl.pallas_call(
        functools.partial(_merge_kernel, na), grid=(N_TOK // tm,),
        in_specs=specs * 3 + [gsp(1), gsp(0), gsp(1)] + [_const_spec((1024, D_MODEL))] * 3,
        out_specs=pl.BlockSpec((tm, D_MODEL), lambda i: (i, 0)),
        out_shape=jax.ShapeDtypeStruct((N_TOK, D_MODEL), BF16),
        compiler_params=_params(("parallel",)), name="merge",
    )(*ys, *yl, *yr, proj_c, proj_d, proj_d, *wb)


def _outproj_kernel(nx, na, m_ref, w_ref, *refs):
    g_ref, oa_ref, ob_ref = refs[nx:]
    x = refs[0][...] if nx == 1 else _pick(N_PROMPT // m_ref.shape[0], refs[0], refs[1])
    x = x + jnp.dot(m_ref[...], w_ref[...], preferred_element_type=F32)
    ms = jnp.mean(x * x, axis=-1, keepdims=True)
    xn = x * lax.rsqrt(ms + NORM_EPS) * g_ref[...]
    if na is None:
        oa_ref[...] = x
        ob_ref[...] = xn.astype(ob_ref.dtype)
    else:
        @pl.when(pl.program_id(0) < na)
        def _():
            oa_ref[...] = xn

        @pl.when(pl.program_id(0) >= na)
        def _():
            ob_ref[...] = xn


def _outproj_call(merged, w_out, x, g_next, final, tm=512):
    tok = pl.BlockSpec((tm, D_MODEL), lambda i: (i, 0))
    x = x if isinstance(x, tuple) else (x,)
    x_specs = [tok] if len(x) == 1 else _two_part_specs(tm, D_MODEL)[0]
    if final:
        out_specs, na = _two_part_specs(tm, D_MODEL)
        out_shape = [jax.ShapeDtypeStruct((N_PROMPT, D_MODEL), F32), jax.ShapeDtypeStruct((N_SAMPLE, D_MODEL), F32)]
    else:
        out_specs, na = [tok, tok], None
        out_shape = [jax.ShapeDtypeStruct((N_TOK, D_MODEL), F32), jax.ShapeDtypeStruct((N_TOK, D_MODEL), BF16)]
    return pl.pallas_call(
        functools.partial(_outproj_kernel, len(x), na), grid=(N_TOK // tm,),
        in_specs=[tok, _const_spec((D_MODEL, D_MODEL))] + x_specs + [_const_spec((1, D_MODEL))],
        out_specs=out_specs, out_shape=out_shape,
        compiler_params=_params(("arbitrary",)), name="outproj",
    )(merged, w_out, *x, g_next.reshape(1, D_MODEL))


def kernel(x_prompt, x_sample, state_s5_re, state_s5_im, state_lru, state_conv, state_ret, norm_g, w_in, s5_lambda_re, s5_lambda_im, s5_log_dt, s5_b_re, s5_b_im, s5_c_re, s5_c_im, s5_d, s5_w_glu, s5_b_glu, lru_conv_w, lru_conv_b, lru_w_a, lru_b_a, lru_w_x, lru_b_x, lru_lambda, ret_gn_g, w_branch_s5, w_branch_lru, w_branch_ret, w_out, final_norm_g):
    x = _to_rows(x_prompt, x_sample)
    xn = _rmsnorm_call(*x, norm_g[0])
    outs_p = [[] for _ in range(5)]
    outs_s = [[] for _ in range(4)]
    r_s = None
    for l in range(DEPTH):
        compact, scan_consts = _s5_prep(s5_lambda_re[l], s5_lambda_im[l], s5_log_dt[l], s5_b_re[l], s5_b_im[l],
                                        s5_c_re[l], s5_c_im[l])
        (*wb, wo, wglu, b_blk, c_blk), proj_a = _inproj_call(
            "inproj_a_prep", xn, w_in, l, 0,
            **_prep_mixer(l, (w_branch_s5, w_branch_lru, w_branch_ret, w_out, s5_w_glu), compact,
                          PROJ_GROUPS[0][1] * MM_UNITS))
        prep = (b_blk.reshape(S5_KB, S5_KW, 2 * S5_NW), c_blk.reshape(S5_KB, 2 * S5_NW, S5_KW), *scan_consts)
        s5_w = (prep, s5_d[l], wglu, s5_b_glu[l])
        wg = jnp.concatenate([lru_w_a[l], lru_w_x[l]], axis=-1).astype(BF16)
        lru_w = _lru_weights(lru_conv_w[l], lru_conv_b[l], wg, lru_b_a[l], lru_b_x[l], lru_lambda[l])
        cbuf = state_conv[l].transpose(1, 0, 2).reshape((CONV_WIDTH - 1) * DEC_BATCH, LRU_WIDTH)

        (ys_p, hr_p, hi_p), proj_b = _inproj_call("inproj_b_s5", xn, w_in, l, 1, **_s5_prompt_mixer(proj_a, *s5_w))
        ys_s, hr_s, hi_s = _s5_sample_call(proj_a, *s5_w, state_s5_re[l].reshape(DEC_BATCH, S5_LANES),
                                           state_s5_im[l].reshape(DEC_BATCH, S5_LANES))
        (yl_p, hl_p, conv_p), proj_c = _inproj_call("inproj_c_lru", xn, w_in, l, 2,
                                                    **_lru_prompt_mixer(proj_b, lru_w))
        conv_p = _lru_prompt_conv_state(conv_p)
        yl_s, hl_s, conv_s = _lru_sample_call(proj_b, lru_w, state_lru[l], cbuf)
        (yr_p, r_p), proj_d = _inproj_call("inproj_d_ret", xn, w_in, l, 3,
                                           **_ret_prompt_mixer(proj_b, proj_c, ret_gn_g[l]))
        yr_s, r_s = _ret_sample_call(proj_b, proj_c, ret_gn_g[l], state_ret, l, r_s)

        merged = _merge_call((ys_p, ys_s), (yl_p, yl_s), (yr_p, yr_s), proj_c, proj_d, wb)
        final = l == DEPTH - 1
        g_next = final_norm_g if final else norm_g[l + 1]
        x, xn = _outproj_call(merged, wo, x, g_next, final)

        st = (S5_GROUPS, S5_STATE)
        for lst, vals in ((outs_p, (hr_p.reshape(BATCH, *st), hi_p.reshape(BATCH, *st),
                                    hl_p.reshape(BATCH, LRU_WIDTH), conv_p, r_p)),
                          (outs_s, (hr_s.reshape(DEC_BATCH, *st), hi_s.reshape(DEC_BATCH, *st), hl_s, conv_s))):
            for j, v in enumerate(vals):
                lst[j].append(v)

    y_prompt, y_sample = _from_rows(x, xn)
    sp = [jnp.stack(t, axis=0) for t in outs_p]
    ss = [jnp.stack(t, axis=0) for t in outs_s]
    return (y_prompt, y_sample, *sp, *ss, r_s)
```

```python
import functools

import jax
import jax.numpy as jnp
from jax import lax
from jax.experimental import pallas as pl
from jax.experimental.pallas import tpu as pltpu

F32 = jnp.float32
BF16 = jnp.bfloat16

D_MODEL = 2048
BATCH = 4
SEQ = 2048
DEPTH = 2
DEC_BATCH = 128
DEC_SEQ = 8
PAST_LEN = 16384
S5_WIDTH = 1024
S5_GROUP = 16
S5_GROUPS = 64
S5_STATE = 64
S5_LANES = S5_GROUPS * S5_STATE
LRU_WIDTH = 1024
LRU_BLOCKS = 8
LRU_BLOCK = 128
CONV_WIDTH = 4
LRU_C = 8.0
RET_HEADS = 8
RET_DK = 128
RET_DV = 128
RET_WIDTH = 1024
ROPE_BASE = 10000.0
NORM_EPS = 1e-6
GN_EPS = 1e-5
N_IN = 14336

N_PROMPT = BATCH * SEQ
N_SAMPLE = DEC_BATCH * DEC_SEQ
N_TOK = N_PROMPT + N_SAMPLE

SUBLANES = 8
LANES = 128
VMEM_LIMIT = 56 * 1024 * 1024

TILE = 256
SEG = TILE // SUBLANES
TILES = SEQ // TILE

PROJ_GROUPS = ((0, 2), (2, 4), (6, 4), (10, 4))
MM_COLS = 1024
MM_UNITS = TILES
MM_ROWS = N_TOK // MM_UNITS
MM_COL_PIECES = 2
MM_PIECE_COLS = MM_COLS // MM_COL_PIECES
MM_ROW_PIECES = 4
MM_PIECE_ROWS = MM_ROWS // MM_ROW_PIECES
MM_PIECES = MM_COL_PIECES * MM_ROW_PIECES

S5_KB = 4
S5_KW = S5_WIDTH // S5_KB
S5_NW = S5_LANES // S5_KB
SCAN_W = 512

SAMPLE_ROW0 = N_PROMPT // DEC_BATCH


def _params(sem, vmem=VMEM_LIMIT):
    return pltpu.CompilerParams(dimension_semantics=sem, vmem_limit_bytes=vmem)


def _const_spec(shape, single=False):
    return pl.BlockSpec(shape, lambda *_: (0,) * len(shape), pipeline_mode=pl.Buffered(1) if single else None)


def _sigmoid(x):
    return jax.nn.sigmoid(x)


def _silu(x):
    return x * jax.nn.sigmoid(x)


def _bcast_row(x, row):
    return jnp.broadcast_to(x[row:row + 1, :], x.shape)


def _cmul(ar, ai, br, bi):
    return ar * br - ai * bi, ar * bi + ai * br


def _to_rows(x_prompt, x_sample):
    xp = x_prompt.reshape(BATCH, TILES, SUBLANES, SEG, -1).transpose(0, 1, 3, 2, 4).reshape(N_PROMPT, -1)
    xs = x_sample.transpose(1, 0, 2).reshape(N_SAMPLE, -1)
    return xp, xs


def _from_rows(yp, ys):
    yp = yp.reshape(BATCH, TILES, SEG, SUBLANES, -1).transpose(0, 1, 3, 2, 4).reshape(BATCH, SEQ, -1)
    ys = ys.reshape(DEC_SEQ, DEC_BATCH, -1).transpose(1, 0, 2)
    return yp, ys


def _two_part_specs(tm, width):
    na = N_PROMPT // tm
    return [pl.BlockSpec((tm, width), lambda i: (jnp.minimum(i, na - 1), 0)),
            pl.BlockSpec((tm, width), lambda i: (jnp.maximum(i - na, 0), 0))], na


def _pick(na, a_ref, b_ref):
    return jnp.where(pl.program_id(0) < na, a_ref[...], b_ref[...])


def _norm_kernel(na, xa_ref, xb_ref, g_ref, o_ref):
    x = _pick(na, xa_ref, xb_ref)
    ms = jnp.mean(x * x, axis=-1, keepdims=True)
    o_ref[...] = (x * lax.rsqrt(ms + NORM_EPS) * g_ref[...]).astype(o_ref.dtype)


def _rmsnorm_call(xp, xs, g, tm=1024):
    specs, na = _two_part_specs(tm, D_MODEL)
    return pl.pallas_call(
        functools.partial(_norm_kernel, na),
        grid=(N_TOK // tm,),
        in_specs=specs + [_const_spec((1, D_MODEL))],
        out_specs=pl.BlockSpec((tm, D_MODEL), lambda i: (i, 0)),
        out_shape=jax.ShapeDtypeStruct((N_TOK, D_MODEL), BF16),
        compiler_params=_params(("parallel",)),
        name="rmsnorm",
    )(xp, xs, g.reshape(1, D_MODEL))


def _prep_kernel(n_cast, mm_piece, *refs):
    n_in = n_cast + 3
    ins, outs = refs[:n_in], refs[n_in:]
    jobs = n_cast + 2

    def pieces(k):
        for p in range(k * MM_PIECES // jobs, (k + 1) * MM_PIECES // jobs):
            mm_piece(p)

    for k in range(n_cast):
        pieces(k)
        outs[k][...] = ins[k][...].astype(BF16)

    def group_mask(nrows, rows_per_group, period, lanes_per_group, j):
        row0 = pl.program_id(0) * nrows
        row = row0 + lax.broadcasted_iota(jnp.int32, (nrows, LANES), 0)
        g_row = jnp.right_shift(jnp.bitwise_and(row, period - 1), rows_per_group.bit_length() - 1)
        g_lane = jnp.right_shift(lax.broadcasted_iota(jnp.int32, (nrows, LANES), 1), lanes_per_group.bit_length() - 1)
        return g_row == (LANES // lanes_per_group) * j + g_lane

    tbr_ref, tbi_ref, tc_ref = ins[n_cast:]
    ob_ref, oc_ref = outs[n_cast:]
    pieces(n_cast)
    tr, ti = tbr_ref[...], tbi_ref[...]
    for j in range(S5_NW // LANES):
        m = group_mask(tr.shape[0], S5_GROUP, S5_KW, S5_STATE, j)
        ob_ref[:, j * LANES:(j + 1) * LANES] = jnp.where(m, tr, 0.0).astype(BF16)
        ob_ref[:, S5_NW + j * LANES:S5_NW + (j + 1) * LANES] = jnp.where(m, ti, 0.0).astype(BF16)
    pieces(n_cast + 1)
    tc = tc_ref[...].astype(F32)
    for j in range(S5_KW // LANES):
        m = group_mask(tc.shape[0], S5_STATE, S5_NW, S5_GROUP, j)
        oc_ref[:, j * LANES:(j + 1) * LANES] = jnp.where(m, tc, 0.0).astype(BF16)


def _prep_mixer(layer, weights, compact, steps):
    shapes = [w.shape[1:] for w in weights]
    tbr, tbi, tc = compact
    row_spec = lambda a: pl.BlockSpec((a.shape[0] // steps, a.shape[1]), lambda s: (s, 0))
    nb, nc = S5_KB * S5_KW, S5_KB * 2 * S5_NW
    return dict(
        mixer=functools.partial(_prep_kernel, len(weights)),
        in_specs=[pl.BlockSpec((None, r // steps, c), lambda s: (layer, s, 0)) for r, c in shapes]
                 + [row_spec(tbr), row_spec(tbi), row_spec(tc)],
        args=list(weights) + [tbr, tbi, tc],
        out_specs=[pl.BlockSpec((r // steps, c), lambda s: (s, 0)) for r, c in shapes]
                  + [pl.BlockSpec((nb // steps, 2 * S5_NW), lambda s: (s, 0)),
                     pl.BlockSpec((nc // steps, S5_KW), lambda s: (s, 0))],
        out_shape=[jax.ShapeDtypeStruct((r, c), BF16) for r, c in shapes]
                  + [jax.ShapeDtypeStruct((nb, 2 * S5_NW), BF16), jax.ShapeDtypeStruct((nc, S5_KW), BF16)])


def _inproj_kernel(mixer, n_in, n_out, *refs):
    xn_ref, w_ref = refs[n_in:n_in + 2]
    o_ref = refs[n_in + 2 + n_out]
    wbf_ref = refs[n_in + 3 + n_out]

    @pl.when(pl.program_id(0) % MM_UNITS == 0)
    def _():
        wbf_ref[...] = w_ref[...].astype(BF16)

    def piece(k):
        rows = slice((k // MM_COL_PIECES) * MM_PIECE_ROWS, (k // MM_COL_PIECES + 1) * MM_PIECE_ROWS)
        cols = slice((k % MM_COL_PIECES) * MM_PIECE_COLS, (k % MM_COL_PIECES + 1) * MM_PIECE_COLS)
        o_ref[rows, cols] = jnp.dot(xn_ref[rows, :], wbf_ref[:, cols],
                                    preferred_element_type=F32).astype(o_ref.dtype)

    if mixer is None:
        for k in range(MM_PIECES):
            piece(k)
    else:
        mixer(piece, *refs[:n_in], *refs[n_in + 2:n_in + 2 + n_out], *refs[n_in + 4 + n_out:])


def _inproj_call(name, xn, w_in, layer, group, mixer=None, in_specs=(), args=(), out_specs=(), out_shape=(),
                 scratch=()):
    first, tiles = PROJ_GROUPS[group]
    mm_in = [pl.BlockSpec((MM_ROWS, D_MODEL), lambda s: (s % MM_UNITS, 0)),
             pl.BlockSpec((None, D_MODEL, MM_COLS), lambda s: (layer, 0, first + s // MM_UNITS))]
    mm_out = pl.BlockSpec((MM_ROWS, MM_COLS), lambda s: (s % MM_UNITS, s // MM_UNITS))
    res = pl.pallas_call(
        functools.partial(_inproj_kernel, mixer, len(in_specs), len(out_specs)),
        grid=(tiles * MM_UNITS,),
        in_specs=list(in_specs) + mm_in,
        out_specs=list(out_specs) + [mm_out],
        out_shape=list(out_shape) + [jax.ShapeDtypeStruct((N_TOK, tiles * MM_COLS), BF16)],
        scratch_shapes=[pltpu.VMEM((D_MODEL, MM_COLS), BF16)] + list(scratch),
        compiler_params=_params(("arbitrary",)),
        name=name,
    )(*args, xn, w_in)
    return res[:-1], res[-1]


def _s5_prep(lam_re, lam_im, log_dt, b_re, b_im, c_re, c_im):
    dt = jnp.exp(log_dt)[:, None]
    e = jnp.exp(lam_re * dt)
    lbr = e * jnp.cos(lam_im * dt)
    lbi = e * jnp.sin(lam_im * dt)
    nr, ni = lbr - 1.0, lbi
    den = lam_re * lam_re + lam_im * lam_im
    cr = (nr * lam_re + ni * lam_im) / den
    ci = (ni * lam_re - nr * lam_im) / den
    bbr = cr[..., None] * b_re - ci[..., None] * b_im
    bbi = cr[..., None] * b_im + ci[..., None] * b_re
    gpb = S5_GROUPS // S5_KB

    def drive(bb):
        t = bb.reshape(S5_KB, gpb, S5_STATE, S5_GROUP).transpose(0, 1, 3, 2).reshape(S5_KB * S5_KW, S5_STATE)
        return jnp.tile(t, (1, LANES // S5_STATE))

    t = jnp.stack([c_re, -c_im]).reshape(2, S5_KB, gpb, S5_GROUP, S5_STATE).transpose(1, 0, 2, 4, 3)
    t = t.reshape(S5_KB * 2 * S5_NW, S5_GROUP).astype(BF16)
    compact = (drive(bbr), drive(bbi), jnp.tile(t, (1, LANES // S5_GROUP)))

    lr, li = lbr.reshape(-1), lbi.reshape(-1)
    full = lambda v: jnp.broadcast_to(v, (SUBLANES, S5_LANES))
    lam = jnp.stack([full(lr), full(li)])
    sr, si = lr, li
    for _ in range(SEG.bit_length() - 1):
        sr, si = _cmul(sr, si, sr, si)
    pr, pi = [sr], [si]
    for _ in range(SUBLANES - 1):
        r_, i_ = _cmul(pr[-1], pi[-1], sr, si)
        pr.append(r_)
        pi.append(i_)
    row = jnp.arange(SUBLANES)[:, None]
    ak = jnp.stack([jnp.stack([jnp.where(row >= k, pr[k - 1][None, :], 0.0),
                               jnp.where(row >= k, pi[k - 1][None, :], 0.0)]) for k in (1, 2, 4)])
    pw = jnp.stack([jnp.stack(pr), jnp.stack(pi)])
    return compact, (lam, ak, pw)


def _s5_drive_block(kb, u, bblk_ref, scr):
    res = jnp.dot(u[:, kb * S5_KW:(kb + 1) * S5_KW], bblk_ref[kb], preferred_element_type=F32)
    scr[:, kb * S5_NW:(kb + 1) * S5_NW] = res[:, :S5_NW]
    scr[:, S5_LANES + kb * S5_NW:S5_LANES + (kb + 1) * S5_NW] = res[:, S5_NW:]


def _s5_readout_block(kb, scr, cblk_ref):
    hcat = jnp.concatenate(
        [scr[:, kb * S5_NW:(kb + 1) * S5_NW],
         scr[:, S5_LANES + kb * S5_NW:S5_LANES + (kb + 1) * S5_NW]], axis=1).astype(BF16)
    return jnp.dot(hcat, cblk_ref[kb], preferred_element_type=F32)


def _s5_finish(parts, u, z, d_ref, wglu_ref, bglu_ref):
    y = jnp.concatenate(parts, axis=1) + d_ref[...] * u.astype(F32)
    y = jax.nn.gelu(y, approximate=True)
    glu = jnp.dot(y.astype(BF16), wglu_ref[...], preferred_element_type=F32) + bglu_ref[...]
    y = y * _sigmoid(glu)
    return (y * _silu(z.astype(F32))).astype(BF16)


def _s5_prompt_kernel(mm_piece, u_ref, z_ref, bblk_ref, cblk_ref, lam_ref, ak_ref, pw_ref, d_ref, wglu_ref,
                      bglu_ref, y_ref, hr_ref, hi_ref, scr, car):
    t = pl.program_id(0) % TILES

    @pl.when(t == 0)
    def _():
        car[...] = jnp.zeros_like(car)

    w = SCAN_W
    rowi = lax.broadcasted_iota(jnp.int32, (SUBLANES, w), 0)
    u = u_ref[...]
    parts = []
    for kb in range(S5_KB):
        _s5_drive_block(kb, u, bblk_ref, scr)
        for c in range(kb * (S5_NW // w), (kb + 1) * (S5_NW // w)):
            mm_piece(c)
            sl_re = slice(c * w, (c + 1) * w)
            sl_im = slice(S5_LANES + c * w, S5_LANES + (c + 1) * w)
            lr, li = lam_ref[0, :, sl_re], lam_ref[1, :, sl_re]

            xr = xi = jnp.zeros((SUBLANES, w), F32)
            for r in range(SEG):
                rows = slice(r * SUBLANES, (r + 1) * SUBLANES)
                pr, pi = _cmul(lr, li, xr, xi)
                xr = scr[rows, sl_re] + pr
                xi = scr[rows, sl_im] + pi
                scr[rows, sl_re] = xr
                scr[rows, sl_im] = xi

            for k, idx in zip((1, 2, 4), range(3)):
                pr, pi = _cmul(ak_ref[idx, 0, :, sl_re], ak_ref[idx, 1, :, sl_re],
                               pltpu.roll(xr, k, 0), pltpu.roll(xi, k, 0))
                xr, xi = xr + pr, xi + pi
            cr, ci = car[0, :, sl_re], car[1, :, sl_re]
            pr, pi = _cmul(pw_ref[0, :, sl_re], pw_ref[1, :, sl_re], cr, ci)
            fr, fi = xr + pr, xi + pi
            dr = jnp.where(rowi == 0, cr, pltpu.roll(fr, 1, 0))
            di = jnp.where(rowi == 0, ci, pltpu.roll(fi, 1, 0))
            car[0, :, sl_re] = _bcast_row(fr, SUBLANES - 1)
            car[1, :, sl_re] = _bcast_row(fi, SUBLANES - 1)

            for r in range(SEG):
                rows = slice(r * SUBLANES, (r + 1) * SUBLANES)
                dr, di = _cmul(lr, li, dr, di)
                scr[rows, sl_re] = scr[rows, sl_re] + dr
                scr[rows, sl_im] = scr[rows, sl_im] + di
        parts.append(_s5_readout_block(kb, scr, cblk_ref))

    y_ref[...] = _s5_finish(parts, u, z_ref[...], d_ref, wglu_ref, bglu_ref)

    @pl.when(t == TILES - 1)
    def _():
        hr_ref[0] = car[0, 0:1, :]
        hi_ref[0] = car[1, 0:1, :]


S5_SB = 32


def _s5_sample_kernel(u_ref, z_ref, h0r_ref, h0i_ref, bblk_ref, cblk_ref, lam_ref, d_ref, wglu_ref, bglu_ref,
                      y_ref, hr_ref, hi_ref, scr):
    rows = DEC_SEQ * S5_SB
    u = u_ref[...].reshape(rows, S5_WIDTH)
    w = SCAN_W
    parts = []
    for kb in range(S5_KB):
        _s5_drive_block(kb, u, bblk_ref, scr)
        for c in range(kb * (S5_NW // w), (kb + 1) * (S5_NW // w)):
            sl_re = slice(c * w, (c + 1) * w)
            sl_im = slice(S5_LANES + c * w, S5_LANES + (c + 1) * w)
            lr, li = lam_ref[0, :, sl_re], lam_ref[1, :, sl_re]
            for g in range(S5_SB // SUBLANES):
                seqs = slice(g * SUBLANES, (g + 1) * SUBLANES)
                hr, hi = h0r_ref[seqs, sl_re], h0i_ref[seqs, sl_re]
                for j in range(DEC_SEQ):
                    rows_j = slice(j * S5_SB + g * SUBLANES, j * S5_SB + (g + 1) * SUBLANES)
                    pr, pi = _cmul(lr, li, hr, hi)
                    hr = scr[rows_j, sl_re] + pr
                    hi = scr[rows_j, sl_im] + pi
                    scr[rows_j, sl_re] = hr
                    scr[rows_j, sl_im] = hi
                hr_ref[seqs, sl_re] = hr
                hi_ref[seqs, sl_re] = hi
        parts.append(_s5_readout_block(kb, scr, cblk_ref))

    y = _s5_finish(parts, u, z_ref[...].reshape(rows, S5_WIDTH), d_ref, wglu_ref, bglu_ref)
    y_ref[...] = y.reshape(DEC_SEQ, S5_SB, S5_WIDTH)


def _tile_spec(col):
    return pl.BlockSpec((TILE, 1024), lambda s: (s, col))


def _state_spec(*shape):
    return pl.BlockSpec((1,) + shape, lambda s: (s // TILES,) + (0,) * len(shape))


def _sample3(proj):
    return proj.reshape(N_TOK // DEC_BATCH, DEC_BATCH, proj.shape[-1])


def _sample_spec(nseq, col):
    return pl.BlockSpec((DEC_SEQ, nseq, 1024), lambda i: (SAMPLE_ROW0 // DEC_SEQ, i, col))


def _s5_specs(prep, d, wglu, bglu, single):
    b_blk, c_blk, lam, ak, pw = prep
    spec = functools.partial(_const_spec, single=single)
    lam_spec = spec((2, SUBLANES, S5_LANES))
    mm = ([spec((S5_KB, S5_KW, 2 * S5_NW)), spec((S5_KB, 2 * S5_NW, S5_KW))], [b_blk, c_blk])
    chain = ([spec((3, 2, SUBLANES, S5_LANES)), lam_spec], [ak, pw])
    tail = ([spec((1, S5_WIDTH)), spec((S5_WIDTH, S5_WIDTH)), spec((1, S5_WIDTH))],
            [d.reshape(1, -1), wglu, bglu.reshape(1, -1)])
    return mm, (lam_spec, lam), chain, tail


def _s5_prompt_mixer(proj_a, prep, d, wglu, bglu):
    mm, (lam_spec, lam), chain, tail = _s5_specs(prep, d, wglu, bglu, single=True)
    return dict(
        mixer=_s5_prompt_kernel,
        in_specs=[_tile_spec(0), _tile_spec(1)] + mm[0] + [lam_spec] + chain[0] + tail[0],
        args=[proj_a, proj_a] + mm[1] + [lam] + chain[1] + tail[1],
        out_specs=[_tile_spec(0), _state_spec(1, S5_LANES), _state_spec(1, S5_LANES)],
        out_shape=[jax.ShapeDtypeStruct((N_PROMPT, 1024), BF16),
                   jax.ShapeDtypeStruct((BATCH, 1, S5_LANES), F32),
                   jax.ShapeDtypeStruct((BATCH, 1, S5_LANES), F32)],
        scratch=[pltpu.VMEM((TILE, 2 * S5_LANES), F32), pltpu.VMEM((2, SUBLANES, S5_LANES), F32)])


def _s5_sample_call(proj_a, prep, d, wglu, bglu, h0r, h0i):
    mm, (lam_spec, lam), _, tail = _s5_specs(prep, d, wglu, bglu, single=False)
    proj3 = _sample3(proj_a)
    st = pl.BlockSpec((S5_SB, S5_LANES), lambda i: (i, 0))
    ys, hr_s, hi_s = pl.pallas_call(
        _s5_sample_kernel, grid=(DEC_BATCH // S5_SB,),
        in_specs=[_sample_spec(S5_SB, 0), _sample_spec(S5_SB, 1), st, st] + mm[0] + [lam_spec] + tail[0],
        out_specs=[pl.BlockSpec((DEC_SEQ, S5_SB, 1024), lambda i: (0, i, 0)), st, st],
        out_shape=[jax.ShapeDtypeStruct((DEC_SEQ, DEC_BATCH, 1024), BF16),
                   jax.ShapeDtypeStruct((DEC_BATCH, S5_LANES), F32),
                   jax.ShapeDtypeStruct((DEC_BATCH, S5_LANES), F32)],
        scratch_shapes=[pltpu.VMEM((DEC_SEQ * S5_SB, 2 * S5_LANES), F32)],
        compiler_params=_params(("parallel",)), name="s5_sample",
    )(proj3, proj3, h0r, h0i, *mm[1], lam, *tail[1])
    return ys.reshape(N_SAMPLE, 1024), hr_s, hi_s


def _lru_gates_block(n, xc, wg_ref, ba_ref, bx_ref, lam_ref):
    sl = slice(n * LRU_BLOCK, (n + 1) * LRU_BLOCK)
    xb = xc[:, sl]
    res = jnp.dot(xb.astype(BF16), wg_ref[n], preferred_element_type=F32)
    r_gate = _sigmoid(res[:, :LRU_BLOCK] + ba_ref[:, sl])
    i_gate = _sigmoid(res[:, LRU_BLOCK:] + bx_ref[:, sl])
    nl = -lam_ref[:, sl]
    softplus = jnp.maximum(nl, 0.0) + jnp.log1p(jnp.exp(-jnp.abs(nl)))
    a = jnp.exp(-LRU_C * r_gate * softplus)
    return a, jnp.sqrt(1.0 - a * a) * (i_gate * xb)


def _lru_gates(xc, wg_ref, ba_ref, bx_ref, lam_ref):
    ab = [_lru_gates_block(n, xc, wg_ref, ba_ref, bx_ref, lam_ref) for n in range(LRU_BLOCKS)]
    return jnp.concatenate([a for a, _ in ab], axis=1), jnp.concatenate([b for _, b in ab], axis=1)


def _lru_prompt_kernel(mm_piece, u_ref, z_ref, cw_ref, cb_ref, wg_ref, ba_ref, bx_ref, lam_ref,
                       y_ref, hl_ref, cv_ref, a_scr, b_scr, tail, car):
    t = pl.program_id(0) % TILES
    nwrap = CONV_WIDTH - 1

    @pl.when(t == 0)
    def _():
        tail[...] = jnp.zeros_like(tail)
        car[...] = jnp.zeros_like(car)

    x = u_ref[...].astype(F32)
    rowi = lax.broadcasted_iota(jnp.int32, (SUBLANES, LRU_WIDTH), 0)
    wrap = []
    for k in range(nwrap):
        cur = x[TILE - (nwrap - k) * SUBLANES:TILE - (nwrap - k - 1) * SUBLANES, :]
        prev = tail[k * SUBLANES:(k + 1) * SUBLANES, :]
        wrap.append(jnp.where(rowi == 0, pltpu.roll(prev, 1, 0), pltpu.roll(cur, 1, 0)))
    xc = cw_ref[nwrap:nwrap + 1, :] * x + cb_ref[...]
    for s in range(1, CONV_WIDTH):
        shifted = jnp.concatenate(wrap[nwrap - s:] + [x[:TILE - s * SUBLANES, :]], axis=0)
        xc = xc + cw_ref[nwrap - s:nwrap - s + 1, :] * shifted
    tail[...] = x[TILE - nwrap * SUBLANES:, :]

    for n in range(LRU_BLOCKS):
        mm_piece(n)
        sl = slice(n * LRU_BLOCK, (n + 1) * LRU_BLOCK)
        a_scr[:, sl], b_scr[:, sl] = _lru_gates_block(n, xc, wg_ref, ba_ref, bx_ref, lam_ref)

    w = SCAN_W
    rw = lax.broadcasted_iota(jnp.int32, (SUBLANES, w), 0)
    for c in range(LRU_WIDTH // w):
        sl = slice(c * w, (c + 1) * w)

        bv = jnp.zeros((SUBLANES, w), F32)
        av = jnp.ones((SUBLANES, w), F32)
        for r in range(SEG):
            rows = slice(r * SUBLANES, (r + 1) * SUBLANES)
            ar = a_scr[rows, sl]
            bv = ar * bv + b_scr[rows, sl]
            av = ar * av
            b_scr[rows, sl] = bv
            a_scr[rows, sl] = av

        for k in (1, 2, 4):
            sa = jnp.where(rw >= k, pltpu.roll(av, k, 0), 1.0)
            sb = jnp.where(rw >= k, pltpu.roll(bv, k, 0), 0.0)
            bv = bv + av * sb
            av = av * sa
        cv = car[:, sl]
        full = bv + av * cv
        enter = jnp.where(rw == 0, cv, pltpu.roll(full, 1, 0))
        car[:, sl] = _bcast_row(full, SUBLANES - 1)

        for r in range(SEG):
            rows = slice(r * SUBLANES, (r + 1) * SUBLANES)
            b_scr[rows, sl] = b_scr[rows, sl] + a_scr[rows, sl] * enter

    y_ref[...] = (b_scr[...] * _silu(z_ref[...].astype(F32))).astype(y_ref.dtype)

    @pl.when(t == TILES - 1)
    def _():
        hl_ref[0] = car[0:1, :]
        cv_ref[0] = tail[...]


def _lru_sample_kernel(u_ref, z_ref, h0_ref, cbuf_ref, cw_ref, cb_ref, wg_ref, ba_ref, bx_ref, lam_ref,
                       y_ref, hl_ref, cv_ref, ext):
    nb = DEC_BATCH
    nwrap = CONV_WIDTH - 1
    ext[0:nwrap * nb, :] = cbuf_ref[...]
    ext[nwrap * nb:, :] = u_ref[...].astype(F32)
    h = h0_ref[...]
    for j in range(DEC_SEQ):
        xc = cb_ref[...]
        for k in range(CONV_WIDTH):
            xc = xc + cw_ref[k:k + 1, :] * ext[(j + k) * nb:(j + k + 1) * nb, :]
        a, b = _lru_gates(xc, wg_ref, ba_ref, bx_ref, lam_ref)
        h = a * h + b
        y_ref[j * nb:(j + 1) * nb, :] = (h * _silu(z_ref[j * nb:(j + 1) * nb, :].astype(F32))).astype(y_ref.dtype)
    hl_ref[...] = h
    cv_ref[...] = ext[DEC_SEQ * nb:, :]


def _lru_weights(conv_w, conv_b, wg, b_a, b_x, lam):
    specs = [_const_spec((CONV_WIDTH, LRU_WIDTH)), _const_spec((1, LRU_WIDTH)),
             _const_spec((LRU_BLOCKS, LRU_BLOCK, 2 * LRU_BLOCK)),
             _const_spec((1, LRU_WIDTH)), _const_spec((1, LRU_WIDTH)), _const_spec((1, LRU_WIDTH))]
    args = [conv_w, conv_b.reshape(1, -1), wg, b_a.reshape(1, -1), b_x.reshape(1, -1), lam.reshape(1, -1)]
    return specs, args


def _lru_prompt_mixer(proj_b, weights):
    w_specs, w_args = weights
    nrows = (CONV_WIDTH - 1) * SUBLANES
    return dict(
        mixer=_lru_prompt_kernel,
        in_specs=[_tile_spec(0), _tile_spec(1)] + w_specs,
        args=[proj_b, proj_b] + w_args,
        out_specs=[_tile_spec(0), _state_spec(1, LRU_WIDTH), _state_spec(nrows, LRU_WIDTH)],
        out_shape=[jax.ShapeDtypeStruct((N_PROMPT, 1024), BF16),
                   jax.ShapeDtypeStruct((BATCH, 1, LRU_WIDTH), F32),
                   jax.ShapeDtypeStruct((BATCH, nrows, LRU_WIDTH), F32)],
        scratch=[pltpu.VMEM((TILE, LRU_WIDTH), F32), pltpu.VMEM((TILE, LRU_WIDTH), F32),
                 pltpu.VMEM((nrows, LRU_WIDTH), F32), pltpu.VMEM((SUBLANES, LRU_WIDTH), F32)])


def _lru_prompt_conv_state(cv_p):
    return cv_p.reshape(BATCH, CONV_WIDTH - 1, SUBLANES, LRU_WIDTH)[:, :, SUBLANES - 1, :]


def _lru_sample_call(proj_b, weights, h0, cbuf):
    w_specs, w_args = weights
    srow = N_PROMPT // N_SAMPLE
    ys, hl_s, cv_s = pl.pallas_call(
        _lru_sample_kernel, grid=(1,),
        in_specs=[pl.BlockSpec((N_SAMPLE, 1024), lambda i: (srow, 0)),
                  pl.BlockSpec((N_SAMPLE, 1024), lambda i: (srow, 1)),
                  _const_spec((DEC_BATCH, LRU_WIDTH)),
                  _const_spec(((CONV_WIDTH - 1) * DEC_BATCH, LRU_WIDTH))] + w_specs,
        out_specs=[_const_spec((N_SAMPLE, 1024)), _const_spec((DEC_BATCH, LRU_WIDTH)),
                   _const_spec(((CONV_WIDTH - 1) * DEC_BATCH, LRU_WIDTH))],
        out_shape=[jax.ShapeDtypeStruct((N_SAMPLE, 1024), BF16),
                   jax.ShapeDtypeStruct((DEC_BATCH, LRU_WIDTH), F32),
                   jax.ShapeDtypeStruct(((CONV_WIDTH - 1) * DEC_BATCH, LRU_WIDTH), F32)],
        scratch_shapes=[pltpu.VMEM(((CONV_WIDTH - 1) * DEC_BATCH + N_SAMPLE, LRU_WIDTH), F32)],
        compiler_params=_params(("arbitrary",)), name="lru_sample",
    )(proj_b, proj_b, h0, cbuf, *w_args)
    return ys, hl_s, cv_s.reshape(CONV_WIDTH - 1, DEC_BATCH, LRU_WIDTH).transpose(1, 0, 2)


RET_SB = 16
RET_SROWS = RET_SB * DEC_SEQ


def _ret_tables(seq, idx):
    n = idx.shape[0]
    chunk = jnp.max(idx) + 1.0
    log_g = jnp.log1p(-jnp.exp2(-5.0 - jnp.arange(RET_HEADS, dtype=F32)))
    diff = idx[:, None] - idx[None, :]
    same = seq[:, None] == seq[None, :]
    dmask = jnp.where((diff[None] >= 0) & same[None],
                      jnp.exp(jnp.maximum(diff, 0.0)[None] * log_g[:, None, None]), 0.0)
    xi = jnp.exp((idx[None, :] + 1.0) * log_g[:, None])
    zeta = jnp.exp((chunk - 1.0 - idx[None, :]) * log_g[:, None])
    gch = jnp.exp(chunk * log_g)
    full = lambda t: jnp.broadcast_to(t[:, :, None], (RET_HEADS, n, LANES))
    gc = jnp.broadcast_to(gch[:, None, None], (RET_HEADS, SUBLANES, LANES))
    return dmask, full(xi), full(zeta), gc


def _rope_tables(pos):
    half = RET_DK // 2
    freq = ROPE_BASE ** (-jnp.arange(half, dtype=F32) / half)
    ang = pos[:, None] * freq[None, :]
    cos, sin = jnp.cos(ang), jnp.sin(ang)
    return jnp.concatenate([cos, cos], axis=-1), jnp.concatenate([-sin, sin], axis=-1)


def _rope(x, cosf, sinf):
    return x * cosf + pltpu.roll(x, RET_DK // 2, 1) * sinf


def _ret_head(h, q, k, v, cos, sin, dmask_ref, zeta_ref):
    sl = slice(h * RET_DK, (h + 1) * RET_DK)
    qb = _rope(q[:, sl].astype(F32), cos, sin).astype(BF16)
    kh = _rope(k[:, sl].astype(F32), cos, sin) * (RET_DK ** -0.5)
    vb = v[:, sl]
    sc = lax.dot_general(qb, kh.astype(BF16), (((1,), (1,)), ((), ())), preferred_element_type=F32)
    inner = jnp.dot((sc * dmask_ref[h]).astype(BF16), vb, preferred_element_type=F32)
    kz = (kh * zeta_ref[h]).astype(BF16)
    return sl, qb, kz, vb, inner


def _ret_finish(o, z, g):
    mu = jnp.mean(o, axis=-1, keepdims=True)
    oc = o - mu
    var = jnp.mean(oc * oc, axis=-1, keepdims=True)
    on = oc * lax.rsqrt(var + GN_EPS) * g
    return (on * _silu(z.astype(F32))).astype(BF16)


def _ret_prompt_kernel(mm_piece, q_ref, k_ref, v_ref, z_ref, cos_ref, sin_ref, dmask_ref, xi_ref, zeta_ref,
                       gc_ref, gng_ref, y_ref, r_ref):
    @pl.when(pl.program_id(0) % TILES == 0)
    def _():
        r_ref[...] = jnp.zeros_like(r_ref)

    cos, sin = cos_ref[...], sin_ref[...]
    for h in range(RET_HEADS):
        mm_piece(h)
        sl, qb, kz, vb, inner = _ret_head(h, q_ref, k_ref, v_ref, cos, sin, dmask_ref, zeta_ref)
        r = r_ref[0, h]
        cross = jnp.dot(qb, r.astype(BF16), preferred_element_type=F32) * xi_ref[h]
        upd = lax.dot_general(kz, vb, (((0,), (0,)), ((), ())), preferred_element_type=F32)
        r_ref[0, h] = r * gc_ref[h, 0:1, :] + upd
        y_ref[:, sl] = _ret_finish(inner + cross, z_ref[:, sl], gng_ref[:, sl])


def _ret_sample_kernel(q_ref, k_ref, v_ref, z_ref, r0_ref, cos_ref, sin_ref, dmask_ref, xi_ref,
                       zeta_ref, gc_ref, gng_ref, *rest):
    y_ref, r_ref = rest[-2:]
    cos, sin = cos_ref[...], sin_ref[...]
    q = q_ref[...].reshape(RET_SROWS, RET_WIDTH)
    k = k_ref[...].reshape(RET_SROWS, RET_WIDTH)
    v = v_ref[...].reshape(RET_SROWS, RET_WIDTH)
    z = z_ref[...].reshape(RET_SROWS, RET_WIDTH)
    rowseq = lax.broadcasted_iota(jnp.int32, (RET_SROWS, RET_DV), 0) % RET_SB
    outs = []
    for h in range(RET_HEADS):
        sl, qb, kz, vb, inner = _ret_head(h, q, k, v, cos, sin, dmask_ref, zeta_ref)
        rcat = jnp.concatenate([r0_ref[s, h] for s in range(RET_SB)], axis=1).astype(BF16)
        call = jnp.dot(qb, rcat, preferred_element_type=F32)
        cross = jnp.zeros((RET_SROWS, RET_DV), F32)
        for s in range(RET_SB):
            cross = jnp.where(rowseq == s, call[:, s * RET_DV:(s + 1) * RET_DV], cross)
        cross = cross * xi_ref[h]
        vf = vb.astype(F32)
        vexp = jnp.concatenate([jnp.where(rowseq == s, vf, 0.0) for s in range(RET_SB)],
                               axis=1).astype(BF16)
        upd = lax.dot_general(kz, vexp, (((0,), (0,)), ((), ())), preferred_element_type=F32)
        gc = gc_ref[h, 0:1, :]
        for s in range(RET_SB):
            r_ref[s, h] = r0_ref[s, h] * gc + upd[:, s * RET_DV:(s + 1) * RET_DV]
        outs.append(_ret_finish(inner + cross, z[:, sl], gng_ref[:, sl]))
    y_ref[...] = jnp.concatenate(outs, axis=1).reshape(DEC_SEQ, RET_SB, RET_WIDTH)


def _ret_table_specs(n):
    return [_const_spec((RET_HEADS, n, n)), _const_spec((RET_HEADS, n, LANES)),
            _const_spec((RET_HEADS, n, LANES)), _const_spec((RET_HEADS, SUBLANES, LANES)),
            _const_spec((1, RET_WIDTH))]


def _ret_prompt_mixer(proj_b, proj_c, gn_g):
    rows = jnp.arange(TILE)
    tok = ((rows % SUBLANES) * SEG + rows // SUBLANES).astype(F32)
    tabs = _ret_tables(jnp.zeros((TILE,), jnp.int32), tok)
    pos = (jnp.arange(TILES, dtype=F32)[:, None] * TILE + tok[None, :]).reshape(SEQ) + 0.0
    cosf, sinf = _rope_tables(pos)
    rope_spec = pl.BlockSpec((TILE, LANES), lambda s: (s % TILES, 0))
    return dict(
        mixer=_ret_prompt_kernel,
        in_specs=[_tile_spec(2), _tile_spec(3), _tile_spec(0), _tile_spec(1), rope_spec, rope_spec]
                 + _ret_table_specs(TILE),
        args=[proj_b, proj_b, proj_c, proj_c, cosf, sinf, *tabs, gn_g.reshape(1, -1)],
        out_specs=[_tile_spec(0), _state_spec(RET_HEADS, RET_DK, RET_DV)],
        out_shape=[jax.ShapeDtypeStruct((N_PROMPT, 1024), BF16),
                   jax.ShapeDtypeStruct((BATCH, RET_HEADS, RET_DK, RET_DV), F32)])


def _ret_sample_call(proj_b, proj_c, gn_g, r0, layer, r_all):
    rows = jnp.arange(RET_SROWS)
    tabs = _ret_tables(rows % RET_SB, (rows // RET_SB).astype(F32))
    cosf, sinf = _rope_tables((rows // RET_SB).astype(F32) + float(PAST_LEN))
    pb3, pc3 = _sample3(proj_b), _sample3(proj_c)
    st_block = (None, RET_SB, RET_HEADS, RET_DK, RET_DV)
    st_spec = pl.BlockSpec(st_block, lambda i: (layer, i, 0, 0, 0))
    in_specs = [_sample_spec(RET_SB, 2), _sample_spec(RET_SB, 3), _sample_spec(RET_SB, 0), _sample_spec(RET_SB, 1),
                st_spec, _const_spec((RET_SROWS, LANES)), _const_spec((RET_SROWS, LANES))] \
        + _ret_table_specs(RET_SROWS)
    args = [pb3, pb3, pc3, pc3, r0, cosf, sinf, *tabs, gn_g.reshape(1, -1)]
    aliases = {}
    if r_all is not None:
        in_specs.append(pl.BlockSpec(memory_space=pl.ANY))
        args.append(r_all)
        aliases = {len(args) - 1: 1}
    ys, r_all = pl.pallas_call(
        _ret_sample_kernel, grid=(DEC_BATCH // RET_SB,),
        in_specs=in_specs,
        out_specs=[pl.BlockSpec((DEC_SEQ, RET_SB, 1024), lambda i: (0, i, 0)), st_spec],
        out_shape=[jax.ShapeDtypeStruct((DEC_SEQ, DEC_BATCH, 1024), BF16),
                   jax.ShapeDtypeStruct((DEPTH, DEC_BATCH, RET_HEADS, RET_DK, RET_DV), F32)],
        input_output_aliases=aliases,
        compiler_params=_params(("parallel",)), name="ret_sample",
    )(*args)
    return ys.reshape(N_SAMPLE, 1024), r_all


def _merge_kernel(na, ysp, yss, ylp, yls, yrp, yrs, gs_ref, gl_ref, gr_ref, ws_ref, wl_ref, wr_ref, o_ref):
    acc = None
    for a_ref, b_ref, g_ref, w_ref in ((ysp, yss, gs_ref, ws_ref), (ylp, yls, gl_ref, wl_ref),
                                       (yrp, yrs, gr_ref, wr_ref)):
        b = jnp.dot(_pick(na, a_ref, b_ref), w_ref[...], preferred_element_type=F32)
        term = _sigmoid(g_ref[...].astype(F32)) * b
        acc = term if acc is None else acc + term
    o_ref[...] = acc.astype(o_ref.dtype)


def _merge_call(ys, yl, yr, proj_c, proj_d, wb, tm=512):
    specs, na = _two_part_specs(tm, 1024)
    gsp = lambda j: pl.BlockSpec((tm, D_MODEL), lambda i: (i, j))
    return pl.pallas_call(
        functools.partial(_merge_kernel, na), grid=(N_TOK // tm,),
        in_specs=specs * 3 + [gsp(1), gsp(0), gsp(1)] + [_const_spec((1024, D_MODEL))] * 3,
        out_specs=pl.BlockSpec((tm, D_MODEL), lambda i: (i, 0)),
        out_shape=jax.ShapeDtypeStruct((N_TOK, D_MODEL), BF16),
        compiler_params=_params(("parallel",)), name="merge",
    )(*ys, *yl, *yr, proj_c, proj_d, proj_d, *wb)


def _outproj_kernel(nx, na, m_ref, w_ref, *refs):
    g_ref, oa_ref, ob_ref = refs[nx:]
    x = refs[0][...] if nx == 1 else _pick(N_PROMPT // m_ref.shape[0], refs[0], refs[1])
    x = x + jnp.dot(m_ref[...], w_ref[...], preferred_element_type=F32)
    ms = jnp.mean(x * x, axis=-1, keepdims=True)
    xn = x * lax.rsqrt(ms + NORM_EPS) * g_ref[...]
    if na is None:
        oa_ref[...] = x
        ob_ref[...] = xn.astype(ob_ref.dtype)
    else:
        @pl.when(pl.program_id(0) < na)
        def _():
            oa_ref[...] = xn

        @pl.when(pl.program_id(0) >= na)
        def _():
            ob_ref[...] = xn


def _outproj_call(merged, w_out, x, g_next, final, tm=512):
    tok = pl.BlockSpec((tm, D_MODEL), lambda i: (i, 0))
    x = x if isinstance(x, tuple) else (x,)
    x_specs = [tok] if len(x) == 1 else _two_part_specs(tm, D_MODEL)[0]
    if final:
        out_specs, na = _two_part_specs(tm, D_MODEL)
        out_shape = [jax.ShapeDtypeStruct((N_PROMPT, D_MODEL), F32), jax.ShapeDtypeStruct((N_SAMPLE, D_MODEL), F32)]
    else:
        out_specs, na = [tok, tok], None
        out_shape = [jax.ShapeDtypeStruct((N_TOK, D_MODEL), F32), jax.ShapeDtypeStruct((N_TOK, D_MODEL), BF16)]
    return pl.pallas_call(
        functools.partial(_outproj_kernel, len(x), na), grid=(N_TOK // tm,),
        in_specs=[tok, _const_spec((D_MODEL, D_MODEL))] + x_specs + [_const_spec((1, D_MODEL))],
        out_specs=out_specs, out_shape=out_shape,
        compiler_params=_params(("arbitrary",)), name="outproj",
    )(merged, w_out, *x, g_next.reshape(1, D_MODEL))


def kernel(x_prompt, x_sample, state_s5_re, state_s5_im, state_lru, state_conv, state_ret, norm_g, w_in, s5_lambda_re, s5_lambda_im, s5_log_dt, s5_b_re, s5_b_im, s5_c_re, s5_c_im, s5_d, s5_w_glu, s5_b_glu, lru_conv_w, lru_conv_b, lru_w_a, lru_b_a, lru_w_x, lru_b_x, lru_lambda, ret_gn_g, w_branch_s5, w_branch_lru, w_branch_ret, w_out, final_norm_g):
    x = _to_rows(x_prompt, x_sample)
    xn = _rmsnorm_call(*x, norm_g[0])
    outs_p = [[] for _ in range(5)]
    outs_s = [[] for _ in range(4)]
    r_s = None
    for l in range(DEPTH):
        compact, scan_consts = _s5_prep(s5_lambda_re[l], s5_lambda_im[l], s5_log_dt[l], s5_b_re[l], s5_b_im[l],
                                        s5_c_re[l], s5_c_im[l])
        (*wb, wo, wglu, b_blk, c_blk), proj_a = _inproj_call(
            "inproj_a_prep", xn, w_in, l, 0,
            **_prep_mixer(l, (w_branch_s5, w_branch_lru, w_branch_ret, w_out, s5_w_glu), compact,
                          PROJ_GROUPS[0][1] * MM_UNITS))
        prep = (b_blk.reshape(S5_KB, S5_KW, 2 * S5_NW), c_blk.reshape(S5_KB, 2 * S5_NW, S5_KW), *scan_consts)
        s5_w = (prep, s5_d[l], wglu, s5_b_glu[l])
        wg = jnp.concatenate([lru_w_a[l], lru_w_x[l]], axis=-1).astype(BF16)
        lru_w = _lru_weights(lru_conv_w[l], lru_conv_b[l], wg, lru_b_a[l], lru_b_x[l], lru_lambda[l])
        cbuf = state_conv[l].transpose(1, 0, 2).reshape((CONV_WIDTH - 1) * DEC_BATCH, LRU_WIDTH)

        (ys_p, hr_p, hi_p), proj_b = _inproj_call("inproj_b_s5", xn, w_in, l, 1, **_s5_prompt_mixer(proj_a, *s5_w))
        ys_s, hr_s, hi_s = _s5_sample_call(proj_a, *s5_w, state_s5_re[l].reshape(DEC_BATCH, S5_LANES),
                                           state_s5_im[l].reshape(DEC_BATCH, S5_LANES))
        (yl_p, hl_p, conv_p), proj_c = _inproj_call("inproj_c_lru", xn, w_in, l, 2,
                                                    **_lru_prompt_mixer(proj_b, lru_w))
        conv_p = _lru_prompt_conv_state(conv_p)
        yl_s, hl_s, conv_s = _lru_sample_call(proj_b, lru_w, state_lru[l], cbuf)
        (yr_p, r_p), proj_d = _inproj_call("inproj_d_ret", xn, w_in, l, 3,
                                           **_ret_prompt_mixer(proj_b, proj_c, ret_gn_g[l]))
        yr_s, r_s = _ret_sample_call(proj_b, proj_c, ret_gn_g[l], state_ret, l, r_s)

        merged = _merge_call((ys_p, ys_s), (yl_p, yl_s), (yr_p, yr_s), proj_c, proj_d, wb)
        final = l == DEPTH - 1
        g_next = final_norm_g if final else norm_g[l + 1]
        x, xn = _outproj_call(merged, wo, x, g_next, final)

        st = (S5_GROUPS, S5_STATE)
        for lst, vals in ((outs_p, (hr_p.reshape(BATCH, *st), hi_p.reshape(BATCH, *st),
                                    hl_p.reshape(BATCH, LRU_WIDTH), conv_p, r_p)),
                          (outs_s, (hr_s.reshape(DEC_BATCH, *st), hi_s.reshape(DEC_BATCH, *st), hl_s, conv_s))):
            for j, v in enumerate(vals):
                lst[j].append(v)

    y_prompt, y_sample = _from_rows(x, xn)
    sp = [jnp.stack(t, axis=0) for t in outs_p]
    ss = [jnp.stack(t, axis=0) for t in outs_s]
    return (y_prompt, y_sample, *sp, *ss, r_s)
```

```python
import functools

import jax
import jax.numpy as jnp
from jax import lax
from jax.experimental import pallas as pl
from jax.experimental.pallas import tpu as pltpu

F32 = jnp.float32
BF16 = jnp.bfloat16

D_MODEL = 2048
BATCH = 4
SEQ = 2048
DEPTH = 2
DEC_BATCH = 128
DEC_SEQ = 8
PAST_LEN = 16384
S5_WIDTH = 1024
S5_GROUP = 16
S5_GROUPS = 64
S5_STATE = 64
S5_LANES = S5_GROUPS * S5_STATE
LRU_WIDTH = 1024
LRU_BLOCKS = 8
LRU_BLOCK = 128
CONV_WIDTH = 4
LRU_C = 8.0
RET_HEADS = 8
RET_DK = 128
RET_DV = 128
RET_WIDTH = 1024
ROPE_BASE = 10000.0
NORM_EPS = 1e-6
GN_EPS = 1e-5
N_IN = 14336

N_PROMPT = BATCH * SEQ
N_SAMPLE = DEC_BATCH * DEC_SEQ
N_TOK = N_PROMPT + N_SAMPLE

SUBLANES = 8
LANES = 128
VMEM_LIMIT = 56 * 1024 * 1024

TILE = 256
SEG = TILE // SUBLANES
TILES = SEQ // TILE

PROJ_GROUPS = ((0, 2), (2, 4), (6, 4), (10, 4))
MM_COLS = 1024
MM_UNITS = TILES
MM_ROWS = N_TOK // MM_UNITS
MM_COL_PIECES = 2
MM_PIECE_COLS = MM_COLS // MM_COL_PIECES
MM_ROW_PIECES = 4
MM_PIECE_ROWS = MM_ROWS // MM_ROW_PIECES
MM_PIECES = MM_COL_PIECES * MM_ROW_PIECES

S5_KB = 4
S5_KW = S5_WIDTH // S5_KB
S5_NW = S5_LANES // S5_KB
SCAN_W = 512

SAMPLE_ROW0 = N_PROMPT // DEC_BATCH


def _params(sem, vmem=VMEM_LIMIT):
    return pltpu.CompilerParams(dimension_semantics=sem, vmem_limit_bytes=vmem)


def _const_spec(shape, single=False):
    return pl.BlockSpec(shape, lambda *_: (0,) * len(shape), pipeline_mode=pl.Buffered(1) if single else None)


def _sigmoid(x):
    return jax.nn.sigmoid(x)


def _silu(x):
    return x * jax.nn.sigmoid(x)


def _bcast_row(x, row):
    return jnp.broadcast_to(x[row:row + 1, :], x.shape)


def _cmul(ar, ai, br, bi):
    return ar * br - ai * bi, ar * bi + ai * br


def _to_rows(x_prompt, x_sample):
    xp = x_prompt.reshape(BATCH, TILES, SUBLANES, SEG, -1).transpose(0, 1, 3, 2, 4).reshape(N_PROMPT, -1)
    xs = x_sample.transpose(1, 0, 2).reshape(N_SAMPLE, -1)
    return xp, xs


def _from_rows(yp, ys):
    yp = yp.reshape(BATCH, TILES, SEG, SUBLANES, -1).transpose(0, 1, 3, 2, 4).reshape(BATCH, SEQ, -1)
    ys = ys.reshape(DEC_SEQ, DEC_BATCH, -1).transpose(1, 0, 2)
    return yp, ys


def _two_part_specs(tm, width):
    na = N_PROMPT // tm
    return [pl.BlockSpec((tm, width), lambda i: (jnp.minimum(i, na - 1), 0)),
            pl.BlockSpec((tm, width), lambda i: (jnp.maximum(i - na, 0), 0))], na


def _pick(na, a_ref, b_ref):
    return jnp.where(pl.program_id(0) < na, a_ref[...], b_ref[...])


def _norm_kernel(na, xa_ref, xb_ref, g_ref, o_ref):
    x = _pick(na, xa_ref, xb_ref)
    ms = jnp.mean(x * x, axis=-1, keepdims=True)
    o_ref[...] = (x * lax.rsqrt(ms + NORM_EPS) * g_ref[...]).astype(o_ref.dtype)


def _rmsnorm_call(xp, xs, g, tm=1024):
    specs, na = _two_part_specs(tm, D_MODEL)
    return pl.pallas_call(
        functools.partial(_norm_kernel, na),
        grid=(N_TOK // tm,),
        in_specs=specs + [_const_spec((1, D_MODEL))],
        out_specs=pl.BlockSpec((tm, D_MODEL), lambda i: (i, 0)),
        out_shape=jax.ShapeDtypeStruct((N_TOK, D_MODEL), BF16),
        compiler_params=_params(("parallel",)),
        name="rmsnorm",
    )(xp, xs, g.reshape(1, D_MODEL))


def _prep_kernel(n_cast, mm_piece, *refs):
    n_in = n_cast + 3
    ins, outs = refs[:n_in], refs[n_in:]
    jobs = n_cast + 2

    def pieces(k):
        for p in range(k * MM_PIECES // jobs, (k + 1) * MM_PIECES // jobs):
            mm_piece(p)

    for k in range(n_cast):
        pieces(k)
        outs[k][...] = ins[k][...].astype(BF16)

    def group_mask(nrows, rows_per_group, period, lanes_per_group, j):
        row0 = pl.program_id(0) * nrows
        row = row0 + lax.broadcasted_iota(jnp.int32, (nrows, LANES), 0)
        g_row = jnp.right_shift(jnp.bitwise_and(row, period - 1), rows_per_group.bit_length() - 1)
        g_lane = jnp.right_shift(lax.broadcasted_iota(jnp.int32, (nrows, LANES), 1), lanes_per_group.bit_length() - 1)
        return g_row == (LANES // lanes_per_group) * j + g_lane

    tbr_ref, tbi_ref, tc_ref = ins[n_cast:]
    ob_ref, oc_ref = outs[n_cast:]
    pieces(n_cast)
    tr, ti = tbr_ref[...], tbi_ref[...]
    for j in range(S5_NW // LANES):
        m = group_mask(tr.shape[0], S5_GROUP, S5_KW, S5_STATE, j)
        ob_ref[:, j * LANES:(j + 1) * LANES] = jnp.where(m, tr, 0.0).astype(BF16)
        ob_ref[:, S5_NW + j * LANES:S5_NW + (j + 1) * LANES] = jnp.where(m, ti, 0.0).astype(BF16)
    pieces(n_cast + 1)
    tc = tc_ref[...].astype(F32)
    for j in range(S5_KW // LANES):
        m = group_mask(tc.shape[0], S5_STATE, S5_NW, S5_GROUP, j)
        oc_ref[:, j * LANES:(j + 1) * LANES] = jnp.where(m, tc, 0.0).astype(BF16)


def _prep_mixer(layer, weights, compact, steps):
    shapes = [w.shape[1:] for w in weights]
    tbr, tbi, tc = compact
    row_spec = lambda a: pl.BlockSpec((a.shape[0] // steps, a.shape[1]), lambda s: (s, 0))
    nb, nc = S5_KB * S5_KW, S5_KB * 2 * S5_NW
    return dict(
        mixer=functools.partial(_prep_kernel, len(weights)),
        in_specs=[pl.BlockSpec((None, r // steps, c), lambda s: (layer, s, 0)) for r, c in shapes]
                 + [row_spec(tbr), row_spec(tbi), row_spec(tc)],
        args=list(weights) + [tbr, tbi, tc],
        out_specs=[pl.BlockSpec((r // steps, c), lambda s: (s, 0)) for r, c in shapes]
                  + [pl.BlockSpec((nb // steps, 2 * S5_NW), lambda s: (s, 0)),
                     pl.BlockSpec((nc // steps, S5_KW), lambda s: (s, 0))],
        out_shape=[jax.ShapeDtypeStruct((r, c), BF16) for r, c in shapes]
                  + [jax.ShapeDtypeStruct((nb, 2 * S5_NW), BF16), jax.ShapeDtypeStruct((nc, S5_KW), BF16)])


def _inproj_kernel(mixer, n_in, n_out, *refs):
    xn_ref, w_ref = refs[n_in:n_in + 2]
    o_ref = refs[n_in + 2 + n_out]
    wbf_ref = refs[n_in + 3 + n_out]

    @pl.when(pl.program_id(0) % MM_UNITS == 0)
    def _():
        wbf_ref[...] = w_ref[...].astype(BF16)

    def piece(k):
        rows = slice((k // MM_COL_PIECES) * MM_PIECE_ROWS, (k // MM_COL_PIECES + 1) * MM_PIECE_ROWS)
        cols = slice((k % MM_COL_PIECES) * MM_PIECE_COLS, (k % MM_COL_PIECES + 1) * MM_PIECE_COLS)
        o_ref[rows, cols] = jnp.dot(xn_ref[rows, :], wbf_ref[:, cols],
                                    preferred_element_type=F32).astype(o_ref.dtype)

    if mixer is None:
        for k in range(MM_PIECES):
            piece(k)
    else:
        mixer(piece, *refs[:n_in], *refs[n_in + 2:n_in + 2 + n_out], *refs[n_in + 4 + n_out:])


def _inproj_call(name, xn, w_in, layer, group, mixer=None, in_specs=(), args=(), out_specs=(), out_shape=(),
                 scratch=()):
    first, tiles = PROJ_GROUPS[group]
    mm_in = [pl.BlockSpec((MM_ROWS, D_MODEL), lambda s: (s % MM_UNITS, 0)),
             pl.BlockSpec((None, D_MODEL, MM_COLS), lambda s: (layer, 0, first + s // MM_UNITS))]
    mm_out = pl.BlockSpec((MM_ROWS, MM_COLS), lambda s: (s % MM_UNITS, s // MM_UNITS))
    res = pl.pallas_call(
        functools.partial(_inproj_kernel, mixer, len(in_specs), len(out_specs)),
        grid=(tiles * MM_UNITS,),
        in_specs=list(in_specs) + mm_in,
        out_specs=list(out_specs) + [mm_out],
        out_shape=list(out_shape) + [jax.ShapeDtypeStruct((N_TOK, tiles * MM_COLS), BF16)],
        scratch_shapes=[pltpu.VMEM((D_MODEL, MM_COLS), BF16)] + list(scratch),
        compiler_params=_params(("arbitrary",)),
        name=name,
    )(*args, xn, w_in)
    return res[:-1], res[-1]


def _s5_prep(lam_re, lam_im, log_dt, b_re, b_im, c_re, c_im):
    dt = jnp.exp(log_dt)[:, None]
    e = jnp.exp(lam_re * dt)
    lbr = e * jnp.cos(lam_im * dt)
    lbi = e * jnp.sin(lam_im * dt)
    nr, ni = lbr - 1.0, lbi
    den = lam_re * lam_re + lam_im * lam_im
    cr = (nr * lam_re + ni * lam_im) / den
    ci = (ni * lam_re - nr * lam_im) / den
    bbr = cr[..., None] * b_re - ci[..., None] * b_im
    bbi = cr[..., None] * b_im + ci[..., None] * b_re
    gpb = S5_GROUPS // S5_KB

    def drive(bb):
        t = bb.reshape(S5_KB, gpb, S5_STATE, S5_GROUP).transpose(0, 1, 3, 2).reshape(S5_KB * S5_KW, S5_STATE)
        return jnp.tile(t, (1, LANES // S5_STATE))

    t = jnp.stack([c_re, -c_im]).reshape(2, S5_KB, gpb, S5_GROUP, S5_STATE).transpose(1, 0, 2, 4, 3)
    t = t.reshape(S5_KB * 2 * S5_NW, S5_GROUP).astype(BF16)
    compact = (drive(bbr), drive(bbi), jnp.tile(t, (1, LANES // S5_GROUP)))

    lr, li = lbr.reshape(-1), lbi.reshape(-1)
    full = lambda v: jnp.broadcast_to(v, (SUBLANES, S5_LANES))
    lam = jnp.stack([full(lr), full(li)])
    sr, si = lr, li
    for _ in range(SEG.bit_length() - 1):
        sr, si = _cmul(sr, si, sr, si)
    pr, pi = [sr], [si]
    for _ in range(SUBLANES - 1):
        r_, i_ = _cmul(pr[-1], pi[-1], sr, si)
        pr.append(r_)
        pi.append(i_)
    row = jnp.arange(SUBLANES)[:, None]
    ak = jnp.stack([jnp.stack([jnp.where(row >= k, pr[k - 1][None, :], 0.0),
                               jnp.where(row >= k, pi[k - 1][None, :], 0.0)]) for k in (1, 2, 4)])
    pw = jnp.stack([jnp.stack(pr), jnp.stack(pi)])
    return compact, (lam, ak, pw)


def _s5_drive_block(kb, u, bblk_ref, scr):
    res = jnp.dot(u[:, kb * S5_KW:(kb + 1) * S5_KW], bblk_ref[kb], preferred_element_type=F32)
    scr[:, kb * S5_NW:(kb + 1) * S5_NW] = res[:, :S5_NW]
    scr[:, S5_LANES + kb * S5_NW:S5_LANES + (kb + 1) * S5_NW] = res[:, S5_NW:]


def _s5_readout_block(kb, scr, cblk_ref):
    hcat = jnp.concatenate(
        [scr[:, kb * S5_NW:(kb + 1) * S5_NW],
         scr[:, S5_LANES + kb * S5_NW:S5_LANES + (kb + 1) * S5_NW]], axis=1).astype(BF16)
    return jnp.dot(hcat, cblk_ref[kb], preferred_element_type=F32)


def _s5_finish(parts, u, z, d_ref, wglu_ref, bglu_ref):
    y = jnp.concatenate(parts, axis=1) + d_ref[...] * u.astype(F32)
    y = jax.nn.gelu(y, approximate=True)
    glu = jnp.dot(y.astype(BF16), wglu_ref[...], preferred_element_type=F32) + bglu_ref[...]
    y = y * _sigmoid(glu)
    return (y * _silu(z.astype(F32))).astype(BF16)


def _s5_prompt_kernel(mm_piece, u_ref, z_ref, bblk_ref, cblk_ref, lam_ref, ak_ref, pw_ref, d_ref, wglu_ref,
                      bglu_ref, y_ref, hr_ref, hi_ref, scr, car):
    t = pl.program_id(0) % TILES

    @pl.when(t == 0)
    def _():
        car[...] = jnp.zeros_like(car)

    w = SCAN_W
    rowi = lax.broadcasted_iota(jnp.int32, (SUBLANES, w), 0)
    u = u_ref[...]
    parts = []
    for kb in range(S5_KB):
        _s5_drive_block(kb, u, bblk_ref, scr)
        for c in range(kb * (S5_NW // w), (kb + 1) * (S5_NW // w)):
            mm_piece(c)
            sl_re = slice(c * w, (c + 1) * w)
            sl_im = slice(S5_LANES + c * w, S5_LANES + (c + 1) * w)
            lr, li = lam_ref[0, :, sl_re], lam_ref[1, :, sl_re]

            xr = xi = jnp.zeros((SUBLANES, w), F32)
            for r in range(SEG):
                rows = slice(r * SUBLANES, (r + 1) * SUBLANES)
                pr, pi = _cmul(lr, li, xr, xi)
                xr = scr[rows, sl_re] + pr
                xi = scr[rows, sl_im] + pi
                scr[rows, sl_re] = xr
                scr[rows, sl_im] = xi

            for k, idx in zip((1, 2, 4), range(3)):
                pr, pi = _cmul(ak_ref[idx, 0, :, sl_re], ak_ref[idx, 1, :, sl_re],
                               pltpu.roll(xr, k, 0), pltpu.roll(xi, k, 0))
                xr, xi = xr + pr, xi + pi
            cr, ci = car[0, :, sl_re], car[1, :, sl_re]
            pr, pi = _cmul(pw_ref[0, :, sl_re], pw_ref[1, :, sl_re], cr, ci)
            fr, fi = xr + pr, xi + pi
            dr = jnp.where(rowi == 0, cr, pltpu.roll(fr, 1, 0))
            di = jnp.where(rowi == 0, ci, pltpu.roll(fi, 1, 0))
            car[0, :, sl_re] = _bcast_row(fr, SUBLANES - 1)
            car[1, :, sl_re] = _bcast_row(fi, SUBLANES - 1)

            for r in range(SEG):
                rows = slice(r * SUBLANES, (r + 1) * SUBLANES)
                dr, di = _cmul(lr, li, dr, di)
                scr[rows, sl_re] = scr[rows, sl_re] + dr
                scr[rows, sl_im] = scr[rows, sl_im] + di
        parts.append(_s5_readout_block(kb, scr, cblk_ref))

    y_ref[...] = _s5_finish(parts, u, z_ref[...], d_ref, wglu_ref, bglu_ref)

    @pl.when(t == TILES - 1)
    def _():
        hr_ref[0] = car[0, 0:1, :]
        hi_ref[0] = car[1, 0:1, :]


S5_SB = 32


def _s5_sample_kernel(layer, u_ref, z_ref, h0r_ref, h0i_ref, bblk_ref, cblk_ref, lam_ref, d_ref, wglu_ref, bglu_ref,
                      *rest):
    y_ref, hr_ref, hi_ref, scr = rest[-4:]
    if layer == 0:
        for ref in (hr_ref, hi_ref):
            ref[1:] = jnp.zeros((DEPTH - 1,) + ref.shape[1:], F32)
        hr_ref, hi_ref = hr_ref.at[0], hi_ref.at[0]
    rows = DEC_SEQ * S5_SB
    u = u_ref[...].reshape(rows, S5_WIDTH)
    w = SCAN_W
    parts = []
    for kb in range(S5_KB):
        _s5_drive_block(kb, u, bblk_ref, scr)
        for c in range(kb * (S5_NW // w), (kb + 1) * (S5_NW // w)):
            sl_re = slice(c * w, (c + 1) * w)
            sl_im = slice(S5_LANES + c * w, S5_LANES + (c + 1) * w)
            lr, li = lam_ref[0, :, sl_re], lam_ref[1, :, sl_re]
            for g in range(S5_SB // SUBLANES):
                seqs = slice(g * SUBLANES, (g + 1) * SUBLANES)
                hr, hi = h0r_ref[seqs, sl_re], h0i_ref[seqs, sl_re]
                for j in range(DEC_SEQ):
                    rows_j = slice(j * S5_SB + g * SUBLANES, j * S5_SB + (g + 1) * SUBLANES)
                    pr, pi = _cmul(lr, li, hr, hi)
                    hr = scr[rows_j, sl_re] + pr
                    hi = scr[rows_j, sl_im] + pi
                    scr[rows_j, sl_re] = hr
                    scr[rows_j, sl_im] = hi
                hr_ref[seqs, sl_re] = hr
                hi_ref[seqs, sl_re] = hi
        parts.append(_s5_readout_block(kb, scr, cblk_ref))

    y = _s5_finish(parts, u, z_ref[...].reshape(rows, S5_WIDTH), d_ref, wglu_ref, bglu_ref)
    y_ref[...] = y.reshape(DEC_SEQ, S5_SB, S5_WIDTH)


def _tile_spec(col):
    return pl.BlockSpec((TILE, 1024), lambda s: (s, col))


def _state_spec(*shape):
    return pl.BlockSpec((1,) + shape, lambda s: (s // TILES,) + (0,) * len(shape))


def _sample3(proj):
    return proj.reshape(N_TOK // DEC_BATCH, DEC_BATCH, proj.shape[-1])


def _sample_spec(nseq, col):
    return pl.BlockSpec((DEC_SEQ, nseq, 1024), lambda i: (SAMPLE_ROW0 // DEC_SEQ, i, col))


def _s5_specs(prep, d, wglu, bglu, single):
    b_blk, c_blk, lam, ak, pw = prep
    spec = functools.partial(_const_spec, single=single)
    lam_spec = spec((2, SUBLANES, S5_LANES))
    mm = ([spec((S5_KB, S5_KW, 2 * S5_NW)), spec((S5_KB, 2 * S5_NW, S5_KW))], [b_blk, c_blk])
    chain = ([spec((3, 2, SUBLANES, S5_LANES)), lam_spec], [ak, pw])
    tail = ([spec((1, S5_WIDTH)), spec((S5_WIDTH, S5_WIDTH)), spec((1, S5_WIDTH))],
            [d.reshape(1, -1), wglu, bglu.reshape(1, -1)])
    return mm, (lam_spec, lam), chain, tail


def _s5_prompt_mixer(proj_a, prep, d, wglu, bglu):
    mm, (lam_spec, lam), chain, tail = _s5_specs(prep, d, wglu, bglu, single=True)
    return dict(
        mixer=_s5_prompt_kernel,
        in_specs=[_tile_spec(0), _tile_spec(1)] + mm[0] + [lam_spec] + chain[0] + tail[0],
        args=[proj_a, proj_a] + mm[1] + [lam] + chain[1] + tail[1],
        out_specs=[_tile_spec(0), _state_spec(1, S5_LANES), _state_spec(1, S5_LANES)],
        out_shape=[jax.ShapeDtypeStruct((N_PROMPT, 1024), BF16),
                   jax.ShapeDtypeStruct((BATCH, 1, S5_LANES), F32),
                   jax.ShapeDtypeStruct((BATCH, 1, S5_LANES), F32)],
        scratch=[pltpu.VMEM((TILE, 2 * S5_LANES), F32), pltpu.VMEM((2, SUBLANES, S5_LANES), F32)])


def _s5_sample_call(proj_a, prep, d, wglu, bglu, h0r, h0i, layer, prev):
    mm, (lam_spec, lam), _, tail = _s5_specs(prep, d, wglu, bglu, single=False)
    proj3 = _sample3(proj_a)
    st = pl.BlockSpec((None, S5_SB, S5_LANES), lambda i: (layer, i, 0))
    in_specs = [_sample_spec(S5_SB, 0), _sample_spec(S5_SB, 1), st, st] + mm[0] + [lam_spec] + tail[0]
    args = [proj3, proj3, h0r, h0i, *mm[1], lam, *tail[1]]
    if layer == 0:
        st_out, aliases = pl.BlockSpec((DEPTH, S5_SB, S5_LANES), lambda i: (0, i, 0)), {}
    else:
        st_out = st
        in_specs += [pl.BlockSpec(memory_space=pl.ANY)] * 2
        aliases = {len(args): 1, len(args) + 1: 2}
        args += list(prev)
    state = jax.ShapeDtypeStruct((DEPTH, DEC_BATCH, S5_LANES), F32)
    ys, hr_s, hi_s = pl.pallas_call(
        functools.partial(_s5_sample_kernel, layer), grid=(DEC_BATCH // S5_SB,),
        in_specs=in_specs,
        out_specs=[pl.BlockSpec((DEC_SEQ, S5_SB, 1024), lambda i: (0, i, 0)), st_out, st_out],
        out_shape=[jax.ShapeDtypeStruct((DEC_SEQ, DEC_BATCH, 1024), BF16), state, state],
        scratch_shapes=[pltpu.VMEM((DEC_SEQ * S5_SB, 2 * S5_LANES), F32)],
        input_output_aliases=aliases,
        compiler_params=_params(("parallel",)), name="s5_sample",
    )(*args)
    return ys.reshape(N_SAMPLE, 1024), (hr_s, hi_s)


def _lru_gates_block(n, xc, wg_ref, ba_ref, bx_ref, lam_ref):
    sl = slice(n * LRU_BLOCK, (n + 1) * LRU_BLOCK)
    xb = xc[:, sl]
    res = jnp.dot(xb.astype(BF16), wg_ref[n], preferred_element_type=F32)
    r_gate = _sigmoid(res[:, :LRU_BLOCK] + ba_ref[:, sl])
    i_gate = _sigmoid(res[:, LRU_BLOCK:] + bx_ref[:, sl])
    nl = -lam_ref[:, sl]
    softplus = jnp.maximum(nl, 0.0) + jnp.log1p(jnp.exp(-jnp.abs(nl)))
    a = jnp.exp(-LRU_C * r_gate * softplus)
    return a, jnp.sqrt(1.0 - a * a) * (i_gate * xb)


def _lru_gates(xc, wg_ref, ba_ref, bx_ref, lam_ref):
    ab = [_lru_gates_block(n, xc, wg_ref, ba_ref, bx_ref, lam_ref) for n in range(LRU_BLOCKS)]
    return jnp.concatenate([a for a, _ in ab], axis=1), jnp.concatenate([b for _, b in ab], axis=1)


def _lru_prompt_kernel(mm_piece, u_ref, z_ref, cw_ref, cb_ref, wg_ref, ba_ref, bx_ref, lam_ref,
                       y_ref, hl_ref, cv_ref, a_scr, b_scr, tail, car):
    t = pl.program_id(0) % TILES
    nwrap = CONV_WIDTH - 1

    @pl.when(t == 0)
    def _():
        tail[...] = jnp.zeros_like(tail)
        car[...] = jnp.zeros_like(car)

    x = u_ref[...].astype(F32)
    rowi = lax.broadcasted_iota(jnp.int32, (SUBLANES, LRU_WIDTH), 0)
    wrap = []
    for k in range(nwrap):
        cur = x[TILE - (nwrap - k) * SUBLANES:TILE - (nwrap - k - 1) * SUBLANES, :]
        prev = tail[k * SUBLANES:(k + 1) * SUBLANES, :]
        wrap.append(jnp.where(rowi == 0, pltpu.roll(prev, 1, 0), pltpu.roll(cur, 1, 0)))
    xc = cw_ref[nwrap:nwrap + 1, :] * x + cb_ref[...]
    for s in range(1, CONV_WIDTH):
        shifted = jnp.concatenate(wrap[nwrap - s:] + [x[:TILE - s * SUBLANES, :]], axis=0)
        xc = xc + cw_ref[nwrap - s:nwrap - s + 1, :] * shifted
    tail[...] = x[TILE - nwrap * SUBLANES:, :]

    for n in range(LRU_BLOCKS):
        mm_piece(n)
        sl = slice(n * LRU_BLOCK, (n + 1) * LRU_BLOCK)
        a_scr[:, sl], b_scr[:, sl] = _lru_gates_block(n, xc, wg_ref, ba_ref, bx_ref, lam_ref)

    w = SCAN_W
    rw = lax.broadcasted_iota(jnp.int32, (SUBLANES, w), 0)
    for c in range(LRU_WIDTH // w):
        sl = slice(c * w, (c + 1) * w)

        bv = jnp.zeros((SUBLANES, w), F32)
        av = jnp.ones((SUBLANES, w), F32)
        for r in range(SEG):
            rows = slice(r * SUBLANES, (r + 1) * SUBLANES)
            ar = a_scr[rows, sl]
            bv = ar * bv + b_scr[rows, sl]
            av = ar * av
            b_scr[rows, sl] = bv
            a_scr[rows, sl] = av

        for k in (1, 2, 4):
            sa = jnp.where(rw >= k, pltpu.roll(av, k, 0), 1.0)
            sb = jnp.where(rw >= k, pltpu.roll(bv, k, 0), 0.0)
            bv = bv + av * sb
            av = av * sa
        cv = car[:, sl]
        full = bv + av * cv
        enter = jnp.where(rw == 0, cv, pltpu.roll(full, 1, 0))
        car[:, sl] = _bcast_row(full, SUBLANES - 1)

        for r in range(SEG):
            rows = slice(r * SUBLANES, (r + 1) * SUBLANES)
            b_scr[rows, sl] = b_scr[rows, sl] + a_scr[rows, sl] * enter

    y_ref[...] = (b_scr[...] * _silu(z_ref[...].astype(F32))).astype(y_ref.dtype)

    @pl.when(t == TILES - 1)
    def _():
        hl_ref[0] = car[0:1, :]
        cv_ref[0] = tail[...]


def _lru_sample_kernel(u_ref, z_ref, h0_ref, cbuf_ref, cw_ref, cb_ref, wg_ref, ba_ref, bx_ref, lam_ref,
                       y_ref, hl_ref, cv_ref, ext):
    nb = DEC_BATCH
    nwrap = CONV_WIDTH - 1
    ext[0:nwrap * nb, :] = cbuf_ref[...]
    ext[nwrap * nb:, :] = u_ref[...].astype(F32)
    h = h0_ref[...]
    for j in range(DEC_SEQ):
        xc = cb_ref[...]
        for k in range(CONV_WIDTH):
            xc = xc + cw_ref[k:k + 1, :] * ext[(j + k) * nb:(j + k + 1) * nb, :]
        a, b = _lru_gates(xc, wg_ref, ba_ref, bx_ref, lam_ref)
        h = a * h + b
        y_ref[j * nb:(j + 1) * nb, :] = (h * _silu(z_ref[j * nb:(j + 1) * nb, :].astype(F32))).astype(y_ref.dtype)
    hl_ref[...] = h
    cv_ref[...] = ext[DEC_SEQ * nb:, :]


def _lru_weights(conv_w, conv_b, wg, b_a, b_x, lam):
    specs = [_const_spec((CONV_WIDTH, LRU_WIDTH)), _const_spec((1, LRU_WIDTH)),
             _const_spec((LRU_BLOCKS, LRU_BLOCK, 2 * LRU_BLOCK)),
             _const_spec((1, LRU_WIDTH)), _const_spec((1, LRU_WIDTH)), _const_spec((1, LRU_WIDTH))]
    args = [conv_w, conv_b.reshape(1, -1), wg, b_a.reshape(1, -1), b_x.reshape(1, -1), lam.reshape(1, -1)]
    return specs, args


def _lru_prompt_mixer(proj_b, weights):
    w_specs, w_args = weights
    nrows = (CONV_WIDTH - 1) * SUBLANES
    return dict(
        mixer=_lru_prompt_kernel,
        in_specs=[_tile_spec(0), _tile_spec(1)] + w_specs,
        args=[proj_b, proj_b] + w_args,
        out_specs=[_tile_spec(0), _state_spec(1, LRU_WIDTH), _state_spec(nrows, LRU_WIDTH)],
        out_shape=[jax.ShapeDtypeStruct((N_PROMPT, 1024), BF16),
                   jax.ShapeDtypeStruct((BATCH, 1, LRU_WIDTH), F32),
                   jax.ShapeDtypeStruct((BATCH, nrows, LRU_WIDTH), F32)],
        scratch=[pltpu.VMEM((TILE, LRU_WIDTH), F32), pltpu.VMEM((TILE, LRU_WIDTH), F32),
                 pltpu.VMEM((nrows, LRU_WIDTH), F32), pltpu.VMEM((SUBLANES, LRU_WIDTH), F32)])


def _lru_prompt_conv_state(cv_p):
    return cv_p.reshape(BATCH, CONV_WIDTH - 1, SUBLANES, LRU_WIDTH)[:, :, SUBLANES - 1, :]


def _lru_sample_call(proj_b, weights, h0, cbuf):
    w_specs, w_args = weights
    srow = N_PROMPT // N_SAMPLE
    ys, hl_s, cv_s = pl.pallas_call(
        _lru_sample_kernel, grid=(1,),
        in_specs=[pl.BlockSpec((N_SAMPLE, 1024), lambda i: (srow, 0)),
                  pl.BlockSpec((N_SAMPLE, 1024), lambda i: (srow, 1)),
                  _const_spec((DEC_BATCH, LRU_WIDTH)),
                  _const_spec(((CONV_WIDTH - 1) * DEC_BATCH, LRU_WIDTH))] + w_specs,
        out_specs=[_const_spec((N_SAMPLE, 1024)), _const_spec((DEC_BATCH, LRU_WIDTH)),
                   _const_spec(((CONV_WIDTH - 1) * DEC_BATCH, LRU_WIDTH))],
        out_shape=[jax.ShapeDtypeStruct((N_SAMPLE, 1024), BF16),
                   jax.ShapeDtypeStruct((DEC_BATCH, LRU_WIDTH), F32),
                   jax.ShapeDtypeStruct(((CONV_WIDTH - 1) * DEC_BATCH, LRU_WIDTH), F32)],
        scratch_shapes=[pltpu.VMEM(((CONV_WIDTH - 1) * DEC_BATCH + N_SAMPLE, LRU_WIDTH), F32)],
        compiler_params=_params(("arbitrary",)), name="lru_sample",
    )(proj_b, proj_b, h0, cbuf, *w_args)
    return ys, hl_s, cv_s.reshape(CONV_WIDTH - 1, DEC_BATCH, LRU_WIDTH).transpose(1, 0, 2)


RET_SB = 16
RET_SROWS = RET_SB * DEC_SEQ


def _ret_tables(seq, idx):
    n = idx.shape[0]
    chunk = jnp.max(idx) + 1.0
    log_g = jnp.log1p(-jnp.exp2(-5.0 - jnp.arange(RET_HEADS, dtype=F32)))
    diff = idx[:, None] - idx[None, :]
    same = seq[:, None] == seq[None, :]
    dmask = jnp.where((diff[None] >= 0) & same[None],
                      jnp.exp(jnp.maximum(diff, 0.0)[None] * log_g[:, None, None]), 0.0)
    xi = jnp.exp((idx[None, :] + 1.0) * log_g[:, None])
    zeta = jnp.exp((chunk - 1.0 - idx[None, :]) * log_g[:, None])
    gch = jnp.exp(chunk * log_g)
    full = lambda t: jnp.broadcast_to(t[:, :, None], (RET_HEADS, n, LANES))
    gc = jnp.broadcast_to(gch[:, None, None], (RET_HEADS, SUBLANES, LANES))
    return dmask, full(xi), full(zeta), gc


def _rope_tables(pos):
    half = RET_DK // 2
    freq = ROPE_BASE ** (-jnp.arange(half, dtype=F32) / half)
    ang = pos[:, None] * freq[None, :]
    cos, sin = jnp.cos(ang), jnp.sin(ang)
    return jnp.concatenate([cos, cos], axis=-1), jnp.concatenate([-sin, sin], axis=-1)


def _rope(x, cosf, sinf):
    return x * cosf + pltpu.roll(x, RET_DK // 2, 1) * sinf


def _ret_head(h, q, k, v, cos, sin, dmask_ref, zeta_ref):
    sl = slice(h * RET_DK, (h + 1) * RET_DK)
    qb = _rope(q[:, sl].astype(F32), cos, sin).astype(BF16)
    kh = _rope(k[:, sl].astype(F32), cos, sin) * (RET_DK ** -0.5)
    vb = v[:, sl]
    sc = lax.dot_general(qb, kh.astype(BF16), (((1,), (1,)), ((), ())), preferred_element_type=F32)
    inner = jnp.dot((sc * dmask_ref[h]).astype(BF16), vb, preferred_element_type=F32)
    kz = (kh * zeta_ref[h]).astype(BF16)
    return sl, qb, kz, vb, inner


def _ret_finish(o, z, g):
    mu = jnp.mean(o, axis=-1, keepdims=True)
    oc = o - mu
    var = jnp.mean(oc * oc, axis=-1, keepdims=True)
    on = oc * lax.rsqrt(var + GN_EPS) * g
    return (on * _silu(z.astype(F32))).astype(BF16)


def _ret_prompt_kernel(mm_piece, q_ref, k_ref, v_ref, z_ref, cos_ref, sin_ref, dmask_ref, xi_ref, zeta_ref,
                       gc_ref, gng_ref, y_ref, r_ref):
    @pl.when(pl.program_id(0) % TILES == 0)
    def _():
        r_ref[...] = jnp.zeros_like(r_ref)

    cos, sin = cos_ref[...], sin_ref[...]
    for h in range(RET_HEADS):
        mm_piece(h)
        sl, qb, kz, vb, inner = _ret_head(h, q_ref, k_ref, v_ref, cos, sin, dmask_ref, zeta_ref)
        r = r_ref[0, h]
        cross = jnp.dot(qb, r.astype(BF16), preferred_element_type=F32) * xi_ref[h]
        upd = lax.dot_general(kz, vb, (((0,), (0,)), ((), ())), preferred_element_type=F32)
        r_ref[0, h] = r * gc_ref[h, 0:1, :] + upd
        y_ref[:, sl] = _ret_finish(inner + cross, z_ref[:, sl], gng_ref[:, sl])


def _ret_sample_kernel(q_ref, k_ref, v_ref, z_ref, r0_ref, cos_ref, sin_ref, dmask_ref, xi_ref,
                       zeta_ref, gc_ref, gng_ref, *rest):
    y_ref, r_ref = rest[-2:]
    cos, sin = cos_ref[...], sin_ref[...]
    q = q_ref[...].reshape(RET_SROWS, RET_WIDTH)
    k = k_ref[...].reshape(RET_SROWS, RET_WIDTH)
    v = v_ref[...].reshape(RET_SROWS, RET_WIDTH)
    z = z_ref[...].reshape(RET_SROWS, RET_WIDTH)
    rowseq = lax.broadcasted_iota(jnp.int32, (RET_SROWS, RET_DV), 0) % RET_SB
    outs = []
    for h in range(RET_HEADS):
        sl, qb, kz, vb, inner = _ret_head(h, q, k, v, cos, sin, dmask_ref, zeta_ref)
        rcat = jnp.concatenate([r0_ref[s, h] for s in range(RET_SB)], axis=1).astype(BF16)
        call = jnp.dot(qb, rcat, preferred_element_type=F32)
        cross = jnp.zeros((RET_SROWS, RET_DV), F32)
        for s in range(RET_SB):
            cross = jnp.where(rowseq == s, call[:, s * RET_DV:(s + 1) * RET_DV], cross)
        cross = cross * xi_ref[h]
        vf = vb.astype(F32)
        vexp = jnp.concatenate([jnp.where(rowseq == s, vf, 0.0) for s in range(RET_SB)],
                               axis=1).astype(BF16)
        upd = lax.dot_general(kz, vexp, (((0,), (0,)), ((), ())), preferred_element_type=F32)
        gc = gc_ref[h, 0:1, :]
        for s in range(RET_SB):
            r_ref[s, h] = r0_ref[s, h] * gc + upd[:, s * RET_DV:(s + 1) * RET_DV]
        outs.append(_ret_finish(inner + cross, z[:, sl], gng_ref[:, sl]))
    y_ref[...] = jnp.concatenate(outs, axis=1).reshape(DEC_SEQ, RET_SB, RET_WIDTH)


def _ret_table_specs(n):
    return [_const_spec((RET_HEADS, n, n)), _const_spec((RET_HEADS, n, LANES)),
            _const_spec((RET_HEADS, n, LANES)), _const_spec((RET_HEADS, SUBLANES, LANES)),
            _const_spec((1, RET_WIDTH))]


def _ret_prompt_mixer(proj_b, proj_c, gn_g):
    rows = jnp.arange(TILE)
    tok = ((rows % SUBLANES) * SEG + rows // SUBLANES).astype(F32)
    tabs = _ret_tables(jnp.zeros((TILE,), jnp.int32), tok)
    pos = (jnp.arange(TILES, dtype=F32)[:, None] * TILE + tok[None, :]).reshape(SEQ) + 0.0
    cosf, sinf = _rope_tables(pos)
    rope_spec = pl.BlockSpec((TILE, LANES), lambda s: (s % TILES, 0))
    return dict(
        mixer=_ret_prompt_kernel,
        in_specs=[_tile_spec(2), _tile_spec(3), _tile_spec(0), _tile_spec(1), rope_spec, rope_spec]
                 + _ret_table_specs(TILE),
        args=[proj_b, proj_b, proj_c, proj_c, cosf, sinf, *tabs, gn_g.reshape(1, -1)],
        out_specs=[_tile_spec(0), _state_spec(RET_HEADS, RET_DK, RET_DV)],
        out_shape=[jax.ShapeDtypeStruct((N_PROMPT, 1024), BF16),
                   jax.ShapeDtypeStruct((BATCH, RET_HEADS, RET_DK, RET_DV), F32)])


def _ret_sample_call(proj_b, proj_c, gn_g, r0, layer, r_all):
    rows = jnp.arange(RET_SROWS)
    tabs = _ret_tables(rows % RET_SB, (rows // RET_SB).astype(F32))
    cosf, sinf = _rope_tables((rows // RET_SB).astype(F32) + float(PAST_LEN))
    pb3, pc3 = _sample3(proj_b), _sample3(proj_c)
    st_block = (None, RET_SB, RET_HEADS, RET_DK, RET_DV)
    st_spec = pl.BlockSpec(st_block, lambda i: (layer, i, 0, 0, 0))
    in_specs = [_sample_spec(RET_SB, 2), _sample_spec(RET_SB, 3), _sample_spec(RET_SB, 0), _sample_spec(RET_SB, 1),
                st_spec, _const_spec((RET_SROWS, LANES)), _const_spec((RET_SROWS, LANES))] \
        + _ret_table_specs(RET_SROWS)
    args = [pb3, pb3, pc3, pc3, r0, cosf, sinf, *tabs, gn_g.reshape(1, -1)]
    aliases = {}
    if r_all is not None:
        in_specs.append(pl.BlockSpec(memory_space=pl.ANY))
        args.append(r_all)
        aliases = {len(args) - 1: 1}
    ys, r_all = pl.pallas_call(
        _ret_sample_kernel, grid=(DEC_BATCH // RET_SB,),
        in_specs=in_specs,
        out_specs=[pl.BlockSpec((DEC_SEQ, RET_SB, 1024), lambda i: (0, i, 0)), st_spec],
        out_shape=[jax.ShapeDtypeStruct((DEC_SEQ, DEC_BATCH, 1024), BF16),
                   jax.ShapeDtypeStruct((DEPTH, DEC_BATCH, RET_HEADS, RET_DK, RET_DV), F32)],
        input_output_aliases=aliases,
        compiler_params=_params(("parallel",)), name="ret_sample",
    )(*args)
    return ys.reshape(N_SAMPLE, 1024), r_all


def _merge_kernel(na, ysp, yss, ylp, yls, yrp, yrs, gs_ref, gl_ref, gr_ref, ws_ref, wl_ref, wr_ref, o_ref):
    acc = None
    for a_ref, b_ref, g_ref, w_ref in ((ysp, yss, gs_ref, ws_ref), (ylp, yls, gl_ref, wl_ref),
                                       (yrp, yrs, gr_ref, wr_ref)):
        b = jnp.dot(_pick(na, a_ref, b_ref), w_ref[...], preferred_element_type=F32)
        term = _sigmoid(g_ref[...].astype(F32)) * b
        acc = term if acc is None else acc + term
    o_ref[...] = acc.astype(o_ref.dtype)


def _merge_call(ys, yl, yr, proj_c, proj_d, wb, tm=512):
    specs, na = _two_part_specs(tm, 1024)
    gsp = lambda j: pl.BlockSpec((tm, D_MODEL), lambda i: (i, j))
    return pl.pallas_call(
        functools.partial(_merge_kernel, na), grid=(N_TOK // tm,),
        in_specs=specs * 3 + [gsp(1), gsp(0), gsp(1)] + [_const_spec((1024, D_MODEL))] * 3,
        out_specs=pl.BlockSpec((tm, D_MODEL), lambda i: (i, 0)),
        out_shape=jax.ShapeDtypeStruct((N_TOK, D_MODEL), BF16),
        compiler_params=_params(("parallel",)), name="merge",
    )(*ys, *yl, *yr, proj_c, proj_d, proj_d, *wb)


def _outproj_kernel(nx, na, m_ref, w_ref, *refs):
    g_ref, oa_ref, ob_ref = refs[nx:]
    x = refs[0][...] if nx == 1 else _pick(N_PROMPT // m_ref.shape[0], refs[0], refs[1])
    x = x + jnp.dot(m_ref[...], w_ref[...], preferred_element_type=F32)
    ms = jnp.mean(x * x, axis=-1, keepdims=True)
    xn = x * lax.rsqrt(ms + NORM_EPS) * g_ref[...]
    if na is None:
        oa_ref[...] = x
        ob_ref[...] = xn.astype(ob_ref.dtype)
    else:
        @pl.when(pl.program_id(0) < na)
        def _():
            oa_ref[...] = xn

        @pl.when(pl.program_id(0) >= na)
        def _():
            ob_ref[...] = xn


def _outproj_call(merged, w_out, x, g_next, final, tm=512):
    tok = pl.BlockSpec((tm, D_MODEL), lambda i: (i, 0))
    x = x if isinstance(x, tuple) else (x,)
    x_specs = [tok] if len(x) == 1 else _two_part_specs(tm, D_MODEL)[0]
    if final:
        out_specs, na = _two_part_specs(tm, D_MODEL)
        out_shape = [jax.ShapeDtypeStruct((N_PROMPT, D_MODEL), F32), jax.ShapeDtypeStruct((N_SAMPLE, D_MODEL), F32)]
    else:
        out_specs, na = [tok, tok], None
        out_shape = [jax.ShapeDtypeStruct((N_TOK, D_MODEL), F32), jax.ShapeDtypeStruct((N_TOK, D_MODEL), BF16)]
    return pl.pallas_call(
        functools.partial(_outproj_kernel, len(x), na), grid=(N_TOK // tm,),
        in_specs=[tok, _const_spec((D_MODEL, D_MODEL))] + x_specs + [_const_spec((1, D_MODEL))],
        out_specs=out_specs, out_shape=out_shape,
        compiler_params=_params(("arbitrary",)), name="outproj",
    )(merged, w_out, *x, g_next.reshape(1, D_MODEL))


def kernel(x_prompt, x_sample, state_s5_re, state_s5_im, state_lru, state_conv, state_ret, norm_g, w_in, s5_lambda_re, s5_lambda_im, s5_log_dt, s5_b_re, s5_b_im, s5_c_re, s5_c_im, s5_d, s5_w_glu, s5_b_glu, lru_conv_w, lru_conv_b, lru_w_a, lru_b_a, lru_w_x, lru_b_x, lru_lambda, ret_gn_g, w_branch_s5, w_branch_lru, w_branch_ret, w_out, final_norm_g):
    x = _to_rows(x_prompt, x_sample)
    xn = _rmsnorm_call(*x, norm_g[0])
    outs_p = [[] for _ in range(5)]
    outs_s = [[] for _ in range(2)]
    r_s = h_s = None
    h0_s5 = (state_s5_re.reshape(DEPTH, DEC_BATCH, S5_LANES), state_s5_im.reshape(DEPTH, DEC_BATCH, S5_LANES))
    for l in range(DEPTH):
        compact, scan_consts = _s5_prep(s5_lambda_re[l], s5_lambda_im[l], s5_log_dt[l], s5_b_re[l], s5_b_im[l],
                                        s5_c_re[l], s5_c_im[l])
        (*wb, wo, wglu, b_blk, c_blk), proj_a = _inproj_call(
            "inproj_a_prep", xn, w_in, l, 0,
            **_prep_mixer(l, (w_branch_s5, w_branch_lru, w_branch_ret, w_out, s5_w_glu), compact,
                          PROJ_GROUPS[0][1] * MM_UNITS))
        prep = (b_blk.reshape(S5_KB, S5_KW, 2 * S5_NW), c_blk.reshape(S5_KB, 2 * S5_NW, S5_KW), *scan_consts)
        s5_w = (prep, s5_d[l], wglu, s5_b_glu[l])
        wg = jnp.concatenate([lru_w_a[l], lru_w_x[l]], axis=-1).astype(BF16)
        lru_w = _lru_weights(lru_conv_w[l], lru_conv_b[l], wg, lru_b_a[l], lru_b_x[l], lru_lambda[l])
        cbuf = state_conv[l].transpose(1, 0, 2).reshape((CONV_WIDTH - 1) * DEC_BATCH, LRU_WIDTH)

        (ys_p, hr_p, hi_p), proj_b = _inproj_call("inproj_b_s5", xn, w_in, l, 1, **_s5_prompt_mixer(proj_a, *s5_w))
        ys_s, h_s = _s5_sample_call(proj_a, *s5_w, *h0_s5, l, h_s)
        (yl_p, hl_p, conv_p), proj_c = _inproj_call("inproj_c_lru", xn, w_in, l, 2,
                                                    **_lru_prompt_mixer(proj_b, lru_w))
        conv_p = _lru_prompt_conv_state(conv_p)
        yl_s, hl_s, conv_s = _lru_sample_call(proj_b, lru_w, state_lru[l], cbuf)
        (yr_p, r_p), proj_d = _inproj_call("inproj_d_ret", xn, w_in, l, 3,
                                           **_ret_prompt_mixer(proj_b, proj_c, ret_gn_g[l]))
        yr_s, r_s = _ret_sample_call(proj_b, proj_c, ret_gn_g[l], state_ret, l, r_s)

        merged = _merge_call((ys_p, ys_s), (yl_p, yl_s), (yr_p, yr_s), proj_c, proj_d, wb)
        final = l == DEPTH - 1
        g_next = final_norm_g if final else norm_g[l + 1]
        x, xn = _outproj_call(merged, wo, x, g_next, final)

        st = (S5_GROUPS, S5_STATE)
        for lst, vals in ((outs_p, (hr_p.reshape(BATCH, *st), hi_p.reshape(BATCH, *st),
                                    hl_p.reshape(BATCH, LRU_WIDTH), conv_p, r_p)),
                          (outs_s, (hl_s, conv_s))):
            for j, v in enumerate(vals):
                lst[j].append(v)

    y_prompt, y_sample = _from_rows(x, xn)
    sp = [jnp.stack(t, axis=0) for t in outs_p]
    ss = [jnp.stack(t, axis=0) for t in outs_s]
    h_s = [h.reshape(DEPTH, DEC_BATCH, S5_GROUPS, S5_STATE) for h in h_s]
    return (y_prompt, y_sample, *sp, *h_s, *ss, r_s)
```
